```python
import math
import jax, jax.numpy as jnp
from jax import lax
import numpy as np

D_MODEL = 2048
BATCH = 8
SEQ = 4096
DEPTH = 1

CHUNK = 64
D_MIX = D_MODEL
ATTN_WIDTH = D_MIX // 2
SSD_WIDTH = D_MIX - ATTN_WIDTH
ATTN_HEAD_DIM = 128
ATTN_Q_HEADS = ATTN_WIDTH // ATTN_HEAD_DIM
ATTN_KV_HEADS = 2
ROPE_THETA = 500000.0
ROPE_PARTIAL_DIV = 4
IDX_HEADS = 16
IDX_HEAD_DIM = 64
IDX_TOPK_MAX = 256
Q_BLOCK = 128
SSD_HEAD_DIM = 64
SSD_HEADS = SSD_WIDTH // SSD_HEAD_DIM
SSD_GROUPS = 4
SSD_HEADS_PER_GROUP = SSD_HEADS // SSD_GROUPS
SSD_STATE = 128
SSD_CONV = 4
SSD_CHUNK = CHUNK
SSD_XBC_WIDTH = SSD_WIDTH + 2 * SSD_GROUPS * SSD_STATE
DEEPNORM_ALPHA = (2.0 * DEPTH) ** 0.25
DEEPNORM_BETA = (8.0 * DEPTH) ** -0.25
LN_EPS = 1e-5
RMS_EPS = 1e-5
IN_WIDTHS = (
    ATTN_Q_HEADS * ATTN_HEAD_DIM,
    ATTN_KV_HEADS * ATTN_HEAD_DIM,
    ATTN_KV_HEADS * ATTN_HEAD_DIM,
    ATTN_WIDTH,
    IDX_HEADS * IDX_HEAD_DIM,
    IDX_HEAD_DIM,
    IDX_HEADS,
    SSD_WIDTH,
    SSD_XBC_WIDTH,
    SSD_HEADS,
)
IN_TOTAL = sum(IN_WIDTHS)
VALUE_COL = 2

kernel_name = "hymba_dsa_ssd_deepnorm_block"


def _split_cols(u, widths):
    out, start = [], 0
    for w in widths:
        out.append(u[..., start:start + w])
        start += w
    return out


def _layer_norm(x, g, b):
    x32 = x.astype(jnp.float32)
    mu = jnp.mean(x32, axis=-1, keepdims=True)
    var = jnp.mean(jnp.square(x32 - mu), axis=-1, keepdims=True)
    y = (x32 - mu) * lax.rsqrt(var + LN_EPS) * g.astype(jnp.float32) + b.astype(jnp.float32)
    return y.astype(x.dtype)


def _rope_partial(t, pos):
    d = t.shape[-1]
    rot = d // ROPE_PARTIAL_DIV
    half = rot // 2
    inv = ROPE_THETA ** (-jnp.arange(half, dtype=jnp.float32) * 2.0 / rot)
    ang = pos[:, None] * inv[None, :]
    cos = jnp.cos(ang)[:, None, :]
    sin = jnp.sin(ang)[:, None, :]
    t32 = t.astype(jnp.float32)
    x1 = t32[..., :half]
    x2 = t32[..., half:rot]
    out = jnp.concatenate([x1 * cos - x2 * sin, x2 * cos + x1 * sin, t32[..., rot:]], axis=-1)
    return out.astype(t.dtype)


def _dsa_attention(q, k, v, q_idx, k_idx, w_idx):
    bsz, seq = q.shape[0], q.shape[1]
    topk = min(IDX_TOPK_MAX, seq // 4)
    rep = ATTN_Q_HEADS // ATTN_KV_HEADS
    scale = ATTN_HEAD_DIM ** -0.5
    key_chunk = jnp.arange(seq) // CHUNK
    k_idx32 = k_idx.astype(jnp.float32)
    gather = jax.vmap(lambda a, ix: a[ix])

    def block(i):
        start = i * Q_BLOCK
        sl = lambda a: lax.dynamic_slice_in_dim(a, start, Q_BLOCK, axis=1)
        qb, qib, wb = sl(q), sl(q_idx), sl(w_idx)
        q_chunk = (start + jnp.arange(Q_BLOCK)) // CHUNK
        logits = jax.nn.relu(jnp.einsum('bqhd,bsd->bqhs', qib.astype(jnp.float32), k_idx32))
        score = jnp.einsum('bqhs,bqh->bqs', logits, wb.astype(jnp.float32))
        admissible = key_chunk[None, :] <= q_chunk[:, None]
        score = jnp.where(admissible[None], score, -jnp.inf)
        _, sel = lax.top_k(score, topk)
        valid = key_chunk[sel] <= q_chunk[None, :, None]
        kg = gather(k, sel)
        vg = gather(v, sel)
        qg = qb.reshape(bsz, Q_BLOCK, ATTN_KV_HEADS, rep, ATTN_HEAD_DIM)
        s = jnp.einsum('bqgrd,bqkgd->bqgrk', qg, kg).astype(jnp.float32) * scale
        s = jnp.where(valid[:, :, None, None, :], s, -jnp.inf)
        p = jax.nn.softmax(s, axis=-1).astype(vg.dtype)
        o = jnp.einsum('bqgrk,bqkgd->bqgrd', p, vg)
        return o.reshape(bsz, Q_BLOCK, ATTN_Q_HEADS * ATTN_HEAD_DIM)

    out = lax.map(block, jnp.arange(seq // Q_BLOCK))
    return out.transpose(1, 0, 2, 3).reshape(bsz, seq, ATTN_Q_HEADS * ATTN_HEAD_DIM)


def _causal_depthwise_conv(u, w, b):
    c = u.shape[-1]
    y = lax.conv_general_dilated(u, w[:, None, :], window_strides=(1,), padding=[(SSD_CONV - 1, 0)],
                                 dimension_numbers=('NWC', 'WIO', 'NWC'), feature_group_count=c)
    return y + b


def _ssd_mixer(xbc, z, dt_raw, conv_w, conv_b, dt_bias, a_log, d_skip, norm_w):
    bsz, seq, _ = xbc.shape
    G, R, P, N, l = SSD_GROUPS, SSD_HEADS_PER_GROUP, SSD_HEAD_DIM, SSD_STATE, SSD_CHUNK
    nc = seq // l
    xbc = jax.nn.silu(_causal_depthwise_conv(xbc, conv_w, conv_b))
    xs, bm, cm = _split_cols(xbc, (SSD_WIDTH, G * N, G * N))
    xs = xs.astype(jnp.float32).reshape(bsz, nc, l, G, R, P)
    bm = bm.astype(jnp.float32).reshape(bsz, nc, l, G, N)
    cm = cm.astype(jnp.float32).reshape(bsz, nc, l, G, N)
    dt = jax.nn.softplus(dt_raw.astype(jnp.float32) + dt_bias.astype(jnp.float32))
    dt = dt.reshape(bsz, nc, l, G, R)
    a = -jnp.exp(a_log.astype(jnp.float32)).reshape(G, R)
    xdt = xs * dt[..., None]
    a_cum = jnp.cumsum(dt * a, axis=2)
    seg = a_cum[:, :, :, None] - a_cum[:, :, None, :]
    tril = jnp.tril(jnp.ones((l, l), dtype=bool))[:, :, None, None]
    lmat = jnp.exp(jnp.where(tril, seg, -jnp.inf))
    cb = jnp.einsum('bclgn,bcsgn->bclsg', cm, bm)
    y_diag = jnp.einsum('bclsg,bclsgr,bcsgrp->bclgrp', cb, lmat, xdt)
    decay = jnp.exp(a_cum[:, :, -1:] - a_cum)
    states = jnp.einsum('bcsgn,bcsgr,bcsgrp->bcgrpn', bm, decay, xdt)
    chunk_decay = jnp.exp(a_cum[:, :, -1])

    def step(h, inp):
        st, dec = inp
        return h * dec[..., None, None] + st, h

    h0 = jnp.zeros((bsz, G, R, P, N), jnp.float32)
    _, prev = lax.scan(step, h0, (jnp.moveaxis(states, 1, 0), jnp.moveaxis(chunk_decay, 1, 0)))
    prev = jnp.moveaxis(prev, 0, 1)
    y_off = jnp.einsum('bclgn,bcgrpn,bclgr->bclgrp', cm, prev, jnp.exp(a_cum))
    y = y_diag + y_off + d_skip.astype(jnp.float32).reshape(G, R)[:, :, None] * xs
    y = y.reshape(bsz, seq, SSD_WIDTH)
    g = (y * jax.nn.silu(z.astype(jnp.float32))).reshape(bsz, seq, G, SSD_WIDTH // G)
    g = g * lax.rsqrt(jnp.mean(jnp.square(g), axis=-1, keepdims=True) + RMS_EPS)
    return (g.reshape(bsz, seq, SSD_WIDTH) * norm_w.astype(jnp.float32)).astype(z.dtype)


def setup_inputs(seed: int = 0) -> dict:
    key = jax.random.key(seed)
    ks = jax.random.split(key, 12)
    x = jax.random.normal(ks[0], (BATCH, SEQ, D_MODEL), jnp.float32)
    blk_keys = jax.random.split(ks[1], len(IN_WIDTHS))
    blocks = []
    for j, w in enumerate(IN_WIDTHS):
        scale = D_MODEL ** -0.5 * (DEEPNORM_BETA if j == VALUE_COL else 1.0)
        blocks.append(jax.random.normal(blk_keys[j], (DEPTH, D_MODEL, w), jnp.float32) * scale)
    w_in = jnp.concatenate(blocks, axis=-1)
    w_out = jax.random.normal(ks[2], (DEPTH, D_MIX, D_MODEL), jnp.float32) * (D_MIX ** -0.5 * DEEPNORM_BETA)
    conv_w = jax.random.normal(ks[3], (DEPTH, SSD_CONV, SSD_XBC_WIDTH), jnp.float32) * SSD_CONV ** -0.5
    conv_b = 0.01 * jax.random.normal(ks[4], (DEPTH, SSD_XBC_WIDTH), jnp.float32)
    u = jax.random.uniform(ks[5], (DEPTH, SSD_HEADS), jnp.float32)
    dt0 = jnp.exp(u * (math.log(0.1) - math.log(0.001)) + math.log(0.001))
    dt_bias = dt0 + jnp.log(-jnp.expm1(-dt0))
    a_log = jnp.log(jax.random.uniform(ks[6], (DEPTH, SSD_HEADS), jnp.float32, 1.0, 16.0))
    d_skip = 1.0 + 0.1 * jax.random.normal(ks[7], (DEPTH, SSD_HEADS), jnp.float32)
    ssd_norm_w = 1.0 + 0.1 * jax.random.normal(ks[8], (DEPTH, SSD_WIDTH), jnp.float32)
    ln_g = 1.0 + 0.1 * jax.random.normal(ks[9], (DEPTH, D_MODEL), jnp.float32)
    ln_b = 0.01 * jax.random.normal(ks[10], (DEPTH, D_MODEL), jnp.float32)
    return {"x": x, "w_in": w_in, "w_out": w_out, "conv_w": conv_w, "conv_b": conv_b,
            "dt_bias": dt_bias, "a_log": a_log, "d_skip": d_skip, "ssd_norm_w": ssd_norm_w,
            "ln_g": ln_g, "ln_b": ln_b}


def reference(x, w_in, w_out, conv_w, conv_b, dt_bias, a_log, d_skip, ssd_norm_w, ln_g, ln_b):
    bsz, seq, _ = x.shape
    pos = jnp.arange(seq, dtype=jnp.float32)
    h = x
    for layer in range(DEPTH):
        u = jnp.einsum('bsd,de->bse', h, w_in[layer])
        q, k, v, z_attn, q_idx, k_idx, w_idx, z_ssd, xbc, dt_raw = _split_cols(u, IN_WIDTHS)
        q = _rope_partial(q.reshape(bsz, seq, ATTN_Q_HEADS, ATTN_HEAD_DIM), pos)
        k = _rope_partial(k.reshape(bsz, seq, ATTN_KV_HEADS, ATTN_HEAD_DIM), pos)
        v = v.reshape(bsz, seq, ATTN_KV_HEADS, ATTN_HEAD_DIM)
        q_idx = _rope_partial(q_idx.reshape(bsz, seq, IDX_HEADS, IDX_HEAD_DIM), pos)
        k_idx = _rope_partial(k_idx[:, :, None, :], pos)[:, :, 0, :]
        w_idx = w_idx * (IDX_HEADS ** -0.5 * IDX_HEAD_DIM ** -0.5)
        o_attn = _dsa_attention(q, k, v, q_idx, k_idx, w_idx) * jax.nn.silu(z_attn)
        o_ssd = _ssd_mixer(xbc, z_ssd, dt_raw, conv_w[layer], conv_b[layer], dt_bias[layer],
                           a_log[layer], d_skip[layer], ssd_norm_w[layer])
        mixed = jnp.concatenate([o_attn, o_ssd.astype(o_attn.dtype)], axis=-1)
        sub = jnp.einsum('bse,ed->bsd', mixed, w_out[layer])
        h = _layer_norm(DEEPNORM_ALPHA * h + sub, ln_g[layer], ln_b[layer])
    return h
```

```python
import functools
import math

import jax
import jax.numpy as jnp
from jax import lax
from jax.experimental import pallas as pl
from jax.experimental.pallas import tpu as pltpu

F32 = jnp.float32
BF16 = jnp.bfloat16
I32 = jnp.int32

D_MODEL = 2048
CHUNK = 64
ATTN_WIDTH = 1024
SSD_WIDTH = 1024
HEAD_DIM = 128
Q_HEADS = 8
KV_HEADS = 2
ROPE_THETA = 500000.0
ROPE_DIV = 4
IDX_HEADS = 16
IDX_DIM = 64
TOPK_MAX = 256
SSD_P = 64
SSD_HEADS = 16
SSD_GROUPS = 4
SSD_N = 128
SSD_CONV = 4
XBC_WIDTH = SSD_WIDTH + 2 * SSD_GROUPS * SSD_N
LN_EPS = 1e-5
RMS_EPS = 1e-5

VMEM_LIMIT = 56 * 1024 * 1024
INT_MIN = -(2 ** 31)

QB = 256
KT = 128
KTA = 256


def _nt(a, b):
    return lax.dot_general(a, b, (((1,), (1,)), ((), ())), preferred_element_type=F32)


def _tn(a, b):
    return lax.dot_general(a, b, (((0,), (0,)), ((), ())), preferred_element_type=F32)


def _nn(a, b):
    return jnp.dot(a, b, preferred_element_type=F32)


def _silu(v):
    return v * (1.0 / (1.0 + jnp.exp(-v)))


def _rope_rows(t, cos, sin, nheads, hd, half):
    pieces = []
    for h in range(nheads):
        b = h * hd
        x1 = t[b:b + half]
        x2 = t[b + half:b + 2 * half]
        pieces += [x1 * cos - x2 * sin, x2 * cos + x1 * sin, t[b + 2 * half:b + hd]]
    return jnp.concatenate(pieces, axis=0)


def _proj_t_kernel(x_ref, wq_ref, wkv_ref, wqi_ref, wkw_ref, ca_ref, sa_ref, ci_ref, si_ref,
                   q_ref, k_ref, v_ref, qi_ref, kw_ref):
    x = x_ref[...]
    ca, sa, ci, si = ca_ref[...], sa_ref[...], ci_ref[...], si_ref[...]
    qscale = HEAD_DIM ** -0.5
    tq = _nt(wq_ref[...], x)
    q_ref[...] = (_rope_rows(tq, ca, sa, Q_HEADS, HEAD_DIM, 16) * qscale).astype(BF16)
    tkv = _nt(wkv_ref[...], x)
    k_ref[...] = _rope_rows(tkv[:KV_HEADS * HEAD_DIM], ca, sa, KV_HEADS, HEAD_DIM, 16).astype(BF16)
    for jj in range(v_ref.shape[0]):
        v_ref[jj] = tkv[KV_HEADS * HEAD_DIM:, jj * KTA:(jj + 1) * KTA].astype(BF16)
    tqi = _nt(wqi_ref[...], x)
    qi_ref[...] = _rope_rows(tqi, ci, si, IDX_HEADS, IDX_DIM, 8).astype(BF16)
    tkw = _nt(wkw_ref[...], x)
    kidx = _rope_rows(tkw[:IDX_DIM], ci, si, 1, IDX_DIM, 8)
    wscale = IDX_HEADS ** -0.5 * IDX_DIM ** -0.5
    kw_ref[...] = jnp.concatenate([kidx, tkw[IDX_DIM:] * wscale], axis=0)


def _proj_t(xb, wq, wkv, wqi, wkw, ca, sa, ci, si, seq, tm):
    m, kdim = xb.shape
    nper = seq // tm
    full = lambda a: pl.BlockSpec(a.shape, lambda i: (0, 0))
    tab = lambda a: pl.BlockSpec((a.shape[0], tm), lambda i: (0, i % nper))
    outs = [(1024, BF16), (256, BF16), (1024, BF16), (128, F32)]
    flat = lambda r: pl.BlockSpec((r, tm), lambda i: (0, i))
    vspec = pl.BlockSpec((tm // KTA, 256, KTA), lambda i: (i, 0, 0))
    vshape = jax.ShapeDtypeStruct((m // KTA, 256, KTA), BF16)
    shp = [jax.ShapeDtypeStruct((r, m), d) for r, d in outs]
    return pl.pallas_call(
        _proj_t_kernel,
        grid=(m // tm,),
        in_specs=[pl.BlockSpec((tm, kdim), lambda i: (i, 0)), full(wq), full(wkv), full(wqi), full(wkw),
                  tab(ca), tab(sa), tab(ci), tab(si)],
        out_specs=[flat(1024), flat(256), vspec, flat(1024), flat(128)],
        out_shape=[shp[0], shp[1], vshape, shp[2], shp[3]],
        compiler_params=pltpu.CompilerParams(dimension_semantics=("arbitrary",), vmem_limit_bytes=VMEM_LIMIT),
        name="proj_t",
    )(xb, wq, wkv, wqi, wkw, ca, sa, ci, si)


def _proj_n_kernel(x_ref, w_ref, o_ref):
    o_ref[...] = _nn(x_ref[...], w_ref[...]).astype(o_ref.dtype)


def _proj_n(xb, w, tm, tn):
    m, kdim = xb.shape
    n = w.shape[1]
    return pl.pallas_call(
        _proj_n_kernel,
        grid=(n // tn, m // tm),
        in_specs=[pl.BlockSpec((tm, kdim), lambda j, i: (i, 0)), pl.BlockSpec((kdim, tn), lambda j, i: (0, j))],
        out_specs=pl.BlockSpec((tm, tn), lambda j, i: (i, j)),
        out_shape=jax.ShapeDtypeStruct((m, n), F32),
        compiler_params=pltpu.CompilerParams(dimension_semantics=("arbitrary", "arbitrary"),
                                             vmem_limit_bytes=VMEM_LIMIT),
        name="proj_n",
    )(xb, w)


def _ordered_key(score):
    bits = pltpu.bitcast(score, I32)
    return bits ^ ((bits >> 31) & 0x7FFFFFFF)


def _attn_kernel(qT_ref, qiT_ref, wT_ref, z_ref, k_ref, vT_ref, ke_ref, ko_ref, o_ref,
                 key_ref, bias_ref, *, topk):
    i = pl.program_id(1)
    nfull = 2 * i
    q_chunk = (i * QB + lax.broadcasted_iota(I32, (1, QB), 1)) // CHUNK

    def score_tile(j):
        rows = pl.ds(pl.multiple_of(j * KT, KT), KT)
        ke = ke_ref[rows, :]
        ko = ko_ref[rows, :]
        acc = jnp.zeros((KT, QB), F32)
        for p in range(IDX_HEADS // 2):
            qp = qiT_ref[p * 128:(p + 1) * 128, :]
            acc += jnp.maximum(_nn(ke, qp), 0.0) * wT_ref[2 * p:2 * p + 1, :]
            acc += jnp.maximum(_nn(ko, qp), 0.0) * wT_ref[2 * p + 1:2 * p + 2, :]
        return rows, _ordered_key(acc)

    def full_tile(j, c):
        rows, key = score_tile(j)
        key_ref[rows, :] = key
        return c

    lax.fori_loop(0, nfull, full_tile, 0)
    for jj in range(QB // KT):
        j = nfull + jj
        rows, key = score_tile(j)
        k_chunk = (j * KT + lax.broadcasted_iota(I32, (KT, 1), 0)) // CHUNK
        key_ref[rows, :] = jnp.where(k_chunk <= q_chunk, key, INT_MIN)
    ntiles = nfull + QB // KT

    def bit_step(it, ans):
        cand = ans + lax.shift_left(jnp.int32(1), 31 - it)

        def count(j, c):
            rows = pl.ds(pl.multiple_of(j * KT, KT), KT)
            ge = (key_ref[rows, :] >= cand).astype(I32)
            return c + jnp.sum(ge.reshape(KT // 8, 8, QB), axis=0)

        cnt = jnp.sum(lax.fori_loop(0, ntiles, count, jnp.zeros((8, QB), I32)), axis=0, keepdims=True)
        return jnp.where(cnt >= topk, cand, ans)

    thr = lax.fori_loop(0, 32, bit_step, jnp.full((1, QB), INT_MIN, I32))
    thr = jnp.maximum(thr, INT_MIN + 1)

    def to_bias(j, c):
        rows = pl.ds(pl.multiple_of(j * KT, KT), KT)
        bias_ref[rows, :] = jnp.where(key_ref[rows, :] >= thr, 0.0, -jnp.inf).astype(F32)
        return c

    lax.fori_loop(0, ntiles, to_bias, 0)

    for h in range(Q_HEADS):
        g = h // (Q_HEADS // KV_HEADS)
        qh = qT_ref[h * HEAD_DIM:(h + 1) * HEAD_DIM, :]

        def kv_step(j, carry):
            m, l, acc = carry
            rows = pl.ds(pl.multiple_of(j * KTA, KTA), KTA)
            s = _nn(k_ref[rows, g * HEAD_DIM:(g + 1) * HEAD_DIM], qh) + bias_ref[rows, :]
            m_new = jnp.maximum(m, jnp.max(s, axis=0, keepdims=True))
            m_safe = jnp.where(m_new == -jnp.inf, 0.0, m_new)
            alpha = jnp.exp(m - m_safe)
            p = jnp.exp(s - m_safe)
            l = alpha * l + jnp.sum(p, axis=0, keepdims=True)
            pv = _nn(vT_ref[j, g * HEAD_DIM:(g + 1) * HEAD_DIM, :], p.astype(BF16))
            return m_new, l, alpha * acc + pv

        init = (jnp.full((1, QB), -jnp.inf, F32), jnp.zeros((1, QB), F32), jnp.zeros((HEAD_DIM, QB), F32))
        _, l, acc = lax.fori_loop(0, i + 1, kv_step, init)
        oh = (acc / l).T
        zh = z_ref[:, h * HEAD_DIM:(h + 1) * HEAD_DIM]
        o_ref[:, h * HEAD_DIM:(h + 1) * HEAD_DIM] = (oh * _silu(zh)).astype(o_ref.dtype)


def _attention(qT, qiT, wT, znat, zcol, k, vT, ke, ko, bsz, seq):
    nq = seq // QB
    topk = min(TOPK_MAX, seq // 4)
    m = bsz * seq
    step = lambda r: pl.BlockSpec((r, QB), lambda b, i: (0, b * nq + i))
    return pl.pallas_call(
        functools.partial(_attn_kernel, topk=topk),
        grid=(bsz, nq),
        in_specs=[step(1024), step(1024), step(IDX_HEADS),
                  pl.BlockSpec((QB, ATTN_WIDTH), lambda b, i: (b * nq + i, zcol)),
                  pl.BlockSpec((seq, 256), lambda b, i: (b, 0)),
                  pl.BlockSpec((seq // KTA, 256, KTA), lambda b, i: (b, 0, 0)),
                  pl.BlockSpec((seq, 128), lambda b, i: (b, 0)),
                  pl.BlockSpec((seq, 128), lambda b, i: (b, 0))],
        out_specs=pl.BlockSpec((QB, ATTN_WIDTH), lambda b, i: (b * nq + i, 0)),
        out_shape=jax.ShapeDtypeStruct((m, ATTN_WIDTH), BF16),
        scratch_shapes=[pltpu.VMEM((seq, QB), I32), pltpu.VMEM((seq, QB), F32)],
        compiler_params=pltpu.CompilerParams(dimension_semantics=("arbitrary", "arbitrary"),
                                             vmem_limit_bytes=VMEM_LIMIT),
        name="dsa_attention",
    )(qT, qiT, wT, znat, k, vT, ke, ko)


def _ssd_kernel(xbc_ref, z_ref, dt_ref, cw_ref, cb_ref, dtb_ref, a_ref, dsk_ref, nw_ref, o_ref,
                buf_ref, tail_ref, state_ref, *, tt):
    t = pl.program_id(1)

    @pl.when(t == 0)
    def _():
        tail_ref[...] = jnp.zeros_like(tail_ref)
        state_ref[...] = jnp.zeros_like(state_ref)

    xin = xbc_ref[...]
    buf_ref[0:8, :] = tail_ref[...]
    buf_ref[8:8 + tt, :] = xin
    tail_ref[...] = xin[tt - 8:tt, :]
    conv = cb_ref[...] + cw_ref[3:4, :] * xin
    for jtap in range(SSD_CONV - 1):
        conv += cw_ref[jtap:jtap + 1, :] * buf_ref[5 + jtap:5 + jtap + tt, :]
    xc = _silu(conv)

    dtv = dt_ref[...] + dtb_ref[...]
    dtv = jnp.maximum(dtv, 0.0) + jnp.log(1.0 + jnp.exp(-jnp.abs(dtv)))
    neg_a = -jnp.exp(a_ref[...])
    dta = dtv * neg_a
    r_i = lax.broadcasted_iota(I32, (CHUNK, CHUNK), 0)
    c_i = lax.broadcasted_iota(I32, (CHUNK, CHUNK), 1)
    tril = r_i >= c_i
    tril_b = tril.astype(BF16)

    for c in range(tt // CHUNK):
        r0 = c * CHUNK
        dta_c = dta[r0:r0 + CHUNK]
        hi = dta_c.astype(BF16)
        lo = (dta_c - hi.astype(F32)).astype(BF16)
        a_cum = _nn(tril_b, hi) + _nn(tril_b, lo)
        a_cum_t = a_cum.T
        a_last = a_cum[CHUNK - 1:CHUNK, :]
        dt_c = dtv[r0:r0 + CHUNK]
        for g in range(SSD_GROUPS):
            bg = xc[r0:r0 + CHUNK, SSD_WIDTH + g * SSD_N:SSD_WIDTH + (g + 1) * SSD_N]
            cg = xc[r0:r0 + CHUNK, SSD_WIDTH + SSD_GROUPS * SSD_N + g * SSD_N:
                    SSD_WIDTH + SSD_GROUPS * SSD_N + (g + 1) * SSD_N]
            bg16, cg16 = bg.astype(BF16), cg.astype(BF16)
            cb = _nt(cg16, bg16)
            prev = state_ref[g]
            y_off = _nn(cg16, prev.astype(BF16))
            ys, xdds, cdec = [], [], []
            for r in range(SSD_HEADS // SSD_GROUPS):
                h = g * (SSD_HEADS // SSD_GROUPS) + r
                xh = xc[r0:r0 + CHUNK, h * SSD_P:(h + 1) * SSD_P]
                ac_col = a_cum[:, h:h + 1]
                ac_row = a_cum_t[h:h + 1, :]
                lmat = jnp.exp(jnp.where(tril, ac_col - ac_row, -jnp.inf))
                xdt = xh * dt_c[:, h:h + 1]
                y = _nn((cb * lmat).astype(BF16), xdt.astype(BF16))
                y += y_off[:, r * SSD_P:(r + 1) * SSD_P] * jnp.exp(ac_col)
                y += dsk_ref[:, h:h + 1] * xh
                ys.append(y)
                xdds.append(xdt * jnp.exp(a_last[:, h:h + 1] - ac_col))
                cdec.append(jnp.broadcast_to(jnp.exp(a_last[:, h:h + 1]), (1, SSD_P)))
            xdd = jnp.concatenate(xdds, axis=1).astype(BF16)
            states = _nn(bg.T.astype(BF16), xdd)
            state_ref[g] = prev * jnp.concatenate(cdec, axis=1) + states
            yg = jnp.concatenate(ys, axis=1)
            gw = SSD_WIDTH // SSD_GROUPS
            gz = yg * _silu(z_ref[r0:r0 + CHUNK, g * gw:(g + 1) * gw])
            gz = gz * lax.rsqrt(jnp.mean(gz * gz, axis=1, keepdims=True) + RMS_EPS)
            o_ref[r0:r0 + CHUNK, g * gw:(g + 1) * gw] = (gz * nw_ref[:, g * gw:(g + 1) * gw]).astype(o_ref.dtype)


def _ssd(nat, conv_w, conv_b, dt_bias, a_log, d_skip, norm_w, bsz, seq, tt):
    m = bsz * seq
    nt = seq // tt
    row = lambda b, t: b * nt + t
    full = lambda a: pl.BlockSpec(a.shape, lambda b, t: (0, 0))
    return pl.pallas_call(
        functools.partial(_ssd_kernel, tt=tt),
        grid=(bsz, nt),
        in_specs=[pl.BlockSpec((tt, XBC_WIDTH), lambda b, t: (row(b, t), 0)),
                  pl.BlockSpec((tt, SSD_WIDTH), lambda b, t: (row(b, t), XBC_WIDTH // SSD_WIDTH)),
                  pl.BlockSpec((tt, 128), lambda b, t: (row(b, t), (XBC_WIDTH + SSD_WIDTH + ATTN_WIDTH) // 128)),
                  full(conv_w), full(conv_b), full(dt_bias), full(a_log), full(d_skip), full(norm_w)],
        out_specs=pl.BlockSpec((tt, SSD_WIDTH), lambda b, t: (row(b, t), 0)),
        out_shape=jax.ShapeDtypeStruct((m, SSD_WIDTH), BF16),
        scratch_shapes=[pltpu.VMEM((tt + 8, XBC_WIDTH), F32), pltpu.VMEM((8, XBC_WIDTH), F32),
                        pltpu.VMEM((SSD_GROUPS, SSD_N, SSD_WIDTH // SSD_GROUPS), F32)],
        compiler_params=pltpu.CompilerParams(dimension_semantics=("arbitrary", "arbitrary"),
                                             vmem_limit_bytes=VMEM_LIMIT),
        name="ssd_mixer",
    )(nat, nat, nat, conv_w, conv_b, dt_bias, a_log, d_skip, norm_w)


def _out_kernel(oa_ref, os_ref, x_ref, wa_ref, ws_ref, g_ref, b_ref, o_ref, *, alpha):
    sub = _nn(oa_ref[...], wa_ref[...]) + _nn(os_ref[...], ws_ref[...])
    y = alpha * x_ref[...] + sub
    mu = jnp.mean(y, axis=1, keepdims=True)
    yc = y - mu
    var = jnp.mean(yc * yc, axis=1, keepdims=True)
    o_ref[...] = yc * lax.rsqrt(var + LN_EPS) * g_ref[...] + b_ref[...]


def _out_proj(oa, os_, x2, wa, ws, g, b, alpha, tm):
    m = x2.shape[0]
    full = lambda a: pl.BlockSpec(a.shape, lambda i: (0, 0))
    rowblk = lambda w: pl.BlockSpec((tm, w), lambda i: (i, 0))
    return pl.pallas_call(
        functools.partial(_out_kernel, alpha=alpha),
        grid=(m // tm,),
        in_specs=[rowblk(ATTN_WIDTH), rowblk(SSD_WIDTH), rowblk(D_MODEL), full(wa), full(ws), full(g), full(b)],
        out_specs=rowblk(D_MODEL),
        out_shape=jax.ShapeDtypeStruct((m, D_MODEL), F32),
        compiler_params=pltpu.CompilerParams(dimension_semantics=("arbitrary",), vmem_limit_bytes=VMEM_LIMIT),
        name="out_proj_ln",
    )(oa, os_, x2, wa, ws, g, b)


def _rope_tables(seq, rot):
    half = rot // 2
    inv = ROPE_THETA ** (-jnp.arange(half, dtype=F32) * 2.0 / rot)
    ang = inv[:, None] * jnp.arange(seq, dtype=F32)[None, :]
    return jnp.cos(ang), jnp.sin(ang)


def _pad_lanes(v, width=128):
    v = v.reshape(1, -1).astype(F32)
    return jnp.pad(v, ((0, 0), (0, width - v.shape[1])))


def _layer(h, w_in, w_out, conv_w, conv_b, dt_bias, a_log, d_skip, norm_w, ln_g, ln_b, alpha):
    bsz, seq, _ = h.shape
    m = bsz * seq
    assert seq % QB == 0 and seq % 512 == 0
    x2 = h.reshape(m, D_MODEL)
    xb = x2.astype(BF16)

    o_q, o_k, o_v, o_za, o_qi, o_ki, o_wi, o_zs, o_xbc, o_dt = (
        0, 1024, 1280, 1536, 2560, 3584, 3648, 3664, 4688, 6736)
    wt = w_in.T.astype(BF16)
    wq = wt[o_q:o_k]
    wkv = wt[o_k:o_za]
    wqi = wt[o_qi:o_ki]
    wkw = jnp.pad(wt[o_ki:o_zs], ((0, 128 - (o_zs - o_ki)), (0, 0)))
    wn = jnp.concatenate([w_in[:, o_xbc:o_dt], w_in[:, o_zs:o_xbc], w_in[:, o_za:o_qi],
                          jnp.pad(w_in[:, o_dt:], ((0, 0), (0, 128 - SSD_HEADS)))],
                         axis=1).astype(BF16)

    ca, sa = _rope_tables(seq, HEAD_DIM // ROPE_DIV)
    ci, si = _rope_tables(seq, IDX_DIM // ROPE_DIV)

    qT, kT, vT, qiT, kwT = _proj_t(xb, wq, wkv, wqi, wkw, ca, sa, ci, si, seq, tm=512)
    nat = _proj_n(xb, wn, tm=512, tn=1408)

    k = kT.T
    kidx = kwT[:IDX_DIM].T.astype(BF16)
    zpad = jnp.zeros_like(kidx)
    ke = jnp.concatenate([kidx, zpad], axis=1)
    ko = jnp.concatenate([zpad, kidx], axis=1)
    wT = kwT[IDX_DIM:IDX_DIM + IDX_HEADS]

    o_attn = _attention(qT, qiT, wT, nat, (XBC_WIDTH + SSD_WIDTH) // ATTN_WIDTH, k, vT, ke, ko, bsz, seq)
    o_ssd = _ssd(nat, conv_w, conv_b.reshape(1, -1), _pad_lanes(dt_bias), _pad_lanes(a_log),
                 _pad_lanes(d_skip), norm_w.reshape(1, -1), bsz, seq, tt=256)

    wo = w_out.astype(BF16)
    out = _out_proj(o_attn, o_ssd, x2, wo[:ATTN_WIDTH], wo[ATTN_WIDTH:], ln_g.reshape(1, -1),
                    ln_b.reshape(1, -1), alpha, tm=512)
    return out.reshape(bsz, seq, D_MODEL)


def kernel(x, w_in, w_out, conv_w, conv_b, dt_bias, a_log, d_skip, ssd_norm_w, ln_g, ln_b):
    depth = w_in.shape[0]
    alpha = (2.0 * depth) ** 0.25
    h = x
    for layer in range(depth):
        h = _layer(h, w_in[layer], w_out[layer], conv_w[layer], conv_b[layer], dt_bias[layer],
                   a_log[layer], d_skip[layer], ssd_norm_w[layer], ln_g[layer], ln_b[layer], alpha)
    return h
```

```python
import functools
import math

import jax
import jax.numpy as jnp
from jax import lax
from jax.experimental import pallas as pl
from jax.experimental.pallas import tpu as pltpu

F32 = jnp.float32
BF16 = jnp.bfloat16
I32 = jnp.int32

D_MODEL = 2048
CHUNK = 64
ATTN_WIDTH = 1024
SSD_WIDTH = 1024
HEAD_DIM = 128
Q_HEADS = 8
KV_HEADS = 2
ROPE_THETA = 500000.0
ROPE_DIV = 4
IDX_HEADS = 16
IDX_DIM = 64
TOPK_MAX = 256
SSD_P = 64
SSD_HEADS = 16
SSD_GROUPS = 4
SSD_N = 128
SSD_CONV = 4
XBC_WIDTH = SSD_WIDTH + 2 * SSD_GROUPS * SSD_N
LN_EPS = 1e-5
RMS_EPS = 1e-5

VMEM_LIMIT = 56 * 1024 * 1024
INT_MIN = -(2 ** 31)

QB = 256
KT = 128
KTA = 256


def _nt(a, b):
    return lax.dot_general(a, b, (((1,), (1,)), ((), ())), preferred_element_type=F32)


def _tn(a, b):
    return lax.dot_general(a, b, (((0,), (0,)), ((), ())), preferred_element_type=F32)


def _nn(a, b):
    return jnp.dot(a, b, preferred_element_type=F32)


def _silu(v):
    return v * (1.0 / (1.0 + jnp.exp(-v)))


def _rope_rows(t, cos, sin, nheads, hd, half):
    pieces = []
    for h in range(nheads):
        b = h * hd
        x1 = t[b:b + half]
        x2 = t[b + half:b + 2 * half]
        pieces += [x1 * cos - x2 * sin, x2 * cos + x1 * sin, t[b + 2 * half:b + hd]]
    return jnp.concatenate(pieces, axis=0)


def _proj_t_kernel(x_ref, wq_ref, wkv_ref, wqi_ref, wkw_ref, ca_ref, sa_ref, ci_ref, si_ref,
                   q_ref, k_ref, v_ref, qi_ref, kw_ref):
    x = x_ref[...]
    ca, sa, ci, si = ca_ref[...], sa_ref[...], ci_ref[...], si_ref[...]
    qscale = HEAD_DIM ** -0.5
    tq = _nt(wq_ref[...], x)
    q_ref[...] = (_rope_rows(tq, ca, sa, Q_HEADS, HEAD_DIM, 16) * qscale).astype(BF16)
    tkv = _nt(wkv_ref[...], x)
    k_ref[...] = _rope_rows(tkv[:KV_HEADS * HEAD_DIM], ca, sa, KV_HEADS, HEAD_DIM, 16).astype(BF16)
    for jj in range(v_ref.shape[0]):
        v_ref[jj] = tkv[KV_HEADS * HEAD_DIM:, jj * KTA:(jj + 1) * KTA].astype(BF16)
    tqi = _nt(wqi_ref[...], x)
    qi_ref[...] = _rope_rows(tqi, ci, si, IDX_HEADS, IDX_DIM, 8).astype(BF16)
    tkw = _nt(wkw_ref[...], x)
    kidx = _rope_rows(tkw[:IDX_DIM], ci, si, 1, IDX_DIM, 8)
    wscale = IDX_HEADS ** -0.5 * IDX_DIM ** -0.5
    kw_ref[...] = jnp.concatenate([kidx, tkw[IDX_DIM:] * wscale], axis=0)


def _proj_t(xb, wq, wkv, wqi, wkw, ca, sa, ci, si, seq, tm):
    m, kdim = xb.shape
    nper = seq // tm
    full = lambda a: pl.BlockSpec(a.shape, lambda i: (0, 0))
    tab = lambda a: pl.BlockSpec((a.shape[0], tm), lambda i: (0, i % nper))
    outs = [(1024, BF16), (256, BF16), (1024, BF16), (128, F32)]
    flat = lambda r: pl.BlockSpec((r, tm), lambda i: (0, i))
    vspec = pl.BlockSpec((tm // KTA, 256, KTA), lambda i: (i, 0, 0))
    vshape = jax.ShapeDtypeStruct((m // KTA, 256, KTA), BF16)
    shp = [jax.ShapeDtypeStruct((r, m), d) for r, d in outs]
    return pl.pallas_call(
        _proj_t_kernel,
        grid=(m // tm,),
        in_specs=[pl.BlockSpec((tm, kdim), lambda i: (i, 0)), full(wq), full(wkv), full(wqi), full(wkw),
                  tab(ca), tab(sa), tab(ci), tab(si)],
        out_specs=[flat(1024), flat(256), vspec, flat(1024), flat(128)],
        out_shape=[shp[0], shp[1], vshape, shp[2], shp[3]],
        compiler_params=pltpu.CompilerParams(dimension_semantics=("arbitrary",), vmem_limit_bytes=VMEM_LIMIT),
        name="proj_t",
    )(xb, wq, wkv, wqi, wkw, ca, sa, ci, si)


def _proj_n_kernel(x_ref, w_ref, o_ref):
    o_ref[...] = _nn(x_ref[...], w_ref[...]).astype(o_ref.dtype)


def _proj_n(xb, w, tm, tn):
    m, kdim = xb.shape
    n = w.shape[1]
    return pl.pallas_call(
        _proj_n_kernel,
        grid=(n // tn, m // tm),
        in_specs=[pl.BlockSpec((tm, kdim), lambda j, i: (i, 0)), pl.BlockSpec((kdim, tn), lambda j, i: (0, j))],
        out_specs=pl.BlockSpec((tm, tn), lambda j, i: (i, j)),
        out_shape=jax.ShapeDtypeStruct((m, n), F32),
        compiler_params=pltpu.CompilerParams(dimension_semantics=("arbitrary", "arbitrary"),
                                             vmem_limit_bytes=VMEM_LIMIT),
        name="proj_n",
    )(xb, w)


def _ordered_key(score):
    bits = pltpu.bitcast(score, I32)
    return bits ^ ((bits >> 31) & 0x7FFFFFFF)


def _attn_kernel(qT_ref, qiT_ref, wT_ref, z_ref, k_ref, vT_ref, ke_ref, ko_ref, o_ref,
                 key_ref, bias_ref, m_ref, l_ref, acc_ref, *, topk):
    i = pl.program_id(1)
    nfull = 2 * i
    q_chunk = (i * QB + lax.broadcasted_iota(I32, (1, QB), 1)) // CHUNK

    def score_tile(j):
        rows = pl.ds(pl.multiple_of(j * KT, KT), KT)
        ke = ke_ref[rows, :]
        ko = ko_ref[rows, :]
        acc = jnp.zeros((KT, QB), F32)
        for p in range(IDX_HEADS // 2):
            qp = qiT_ref[p * 128:(p + 1) * 128, :]
            acc += jnp.maximum(_nn(ke, qp), 0.0) * wT_ref[2 * p:2 * p + 1, :]
            acc += jnp.maximum(_nn(ko, qp), 0.0) * wT_ref[2 * p + 1:2 * p + 2, :]
        return rows, _ordered_key(acc)

    def full_tile(j, c):
        rows, key = score_tile(j)
        key_ref[rows, :] = key
        return c

    lax.fori_loop(0, nfull, full_tile, 0)
    for jj in range(QB // KT):
        j = nfull + jj
        rows, key = score_tile(j)
        k_chunk = (j * KT + lax.broadcasted_iota(I32, (KT, 1), 0)) // CHUNK
        key_ref[rows, :] = jnp.where(k_chunk <= q_chunk, key, INT_MIN)
    ntiles = nfull + QB // KT

    def bit_step(it, ans):
        cand = ans + lax.shift_left(jnp.int32(1), 31 - it)

        def count(j, c):
            rows = pl.ds(pl.multiple_of(j * KT, KT), KT)
            ge = (key_ref[rows, :] >= cand).astype(I32)
            return c + jnp.sum(ge.reshape(KT // 8, 8, QB), axis=0)

        cnt = jnp.sum(lax.fori_loop(0, ntiles, count, jnp.zeros((8, QB), I32)), axis=0, keepdims=True)
        return jnp.where(cnt >= topk, cand, ans)

    thr = lax.fori_loop(0, 32, bit_step, jnp.full((1, QB), INT_MIN, I32))
    thr = jnp.maximum(thr, INT_MIN + 1)

    def to_bias(j, c):
        rows = pl.ds(pl.multiple_of(j * KT, KT), KT)
        bias_ref[rows, :] = jnp.where(key_ref[rows, :] >= thr, 0.0, -jnp.inf).astype(F32)
        return c

    lax.fori_loop(0, ntiles, to_bias, 0)

    m_ref[...] = jnp.full(m_ref.shape, -jnp.inf, F32)
    l_ref[...] = jnp.zeros(l_ref.shape, F32)
    acc_ref[...] = jnp.zeros(acc_ref.shape, F32)

    rep = Q_HEADS // KV_HEADS

    def kv_step(j, c):
        rows = pl.ds(pl.multiple_of(j * KTA, KTA), KTA)
        bias = bias_ref[rows, :]
        bias_g = jnp.concatenate([bias] * rep, axis=1)
        for g in range(KV_HEADS):
            qg = jnp.concatenate([qT_ref[(g * rep + r) * HEAD_DIM:(g * rep + r + 1) * HEAD_DIM, :]
                                  for r in range(rep)], axis=1)
            s = _nn(k_ref[rows, g * HEAD_DIM:(g + 1) * HEAD_DIM], qg) + bias_g
            m = m_ref[g:g + 1, :]
            m_new = jnp.maximum(m, jnp.max(s, axis=0, keepdims=True))
            m_safe = jnp.where(m_new == -jnp.inf, 0.0, m_new)
            alpha = jnp.exp(m - m_safe)
            p = jnp.exp(s - m_safe)
            l_ref[g:g + 1, :] = alpha * l_ref[g:g + 1, :] + jnp.sum(p, axis=0, keepdims=True)
            pv = _nn(vT_ref[j, g * HEAD_DIM:(g + 1) * HEAD_DIM, :], p.astype(BF16))
            acc_ref[g] = alpha * acc_ref[g] + pv
            m_ref[g:g + 1, :] = m_new
        return c

    lax.fori_loop(0, i + 1, kv_step, 0)
    for h in range(Q_HEADS):
        g, r = divmod(h, rep)
        oh = (acc_ref[g, :, r * QB:(r + 1) * QB] / l_ref[g:g + 1, r * QB:(r + 1) * QB]).T
        zh = z_ref[:, h * HEAD_DIM:(h + 1) * HEAD_DIM]
        o_ref[:, h * HEAD_DIM:(h + 1) * HEAD_DIM] = (oh * _silu(zh)).astype(o_ref.dtype)


def _attention(qT, qiT, wT, znat, zcol, k, vT, ke, ko, bsz, seq):
    nq = seq // QB
    topk = min(TOPK_MAX, seq // 4)
    m = bsz * seq
    step = lambda r: pl.BlockSpec((r, QB), lambda b, i: (0, b * nq + i))
    return pl.pallas_call(
        functools.partial(_attn_kernel, topk=topk),
        grid=(bsz, nq),
        in_specs=[step(1024), step(1024), step(IDX_HEADS),
                  pl.BlockSpec((QB, ATTN_WIDTH), lambda b, i: (b * nq + i, zcol)),
                  pl.BlockSpec((seq, 256), lambda b, i: (b, 0)),
                  pl.BlockSpec((seq // KTA, 256, KTA), lambda b, i: (b, 0, 0)),
                  pl.BlockSpec((seq, 128), lambda b, i: (b, 0)),
                  pl.BlockSpec((seq, 128), lambda b, i: (b, 0))],
        out_specs=pl.BlockSpec((QB, ATTN_WIDTH), lambda b, i: (b * nq + i, 0)),
        out_shape=jax.ShapeDtypeStruct((m, ATTN_WIDTH), BF16),
        scratch_shapes=[pltpu.VMEM((seq, QB), I32), pltpu.VMEM((seq, QB), F32),
                        pltpu.VMEM((KV_HEADS, Q_HEADS // KV_HEADS * QB), F32),
                        pltpu.VMEM((KV_HEADS, Q_HEADS // KV_HEADS * QB), F32),
                        pltpu.VMEM((KV_HEADS, HEAD_DIM, Q_HEADS // KV_HEADS * QB), F32)],
        compiler_params=pltpu.CompilerParams(dimension_semantics=("arbitrary", "arbitrary"),
                                             vmem_limit_bytes=VMEM_LIMIT),
        name="dsa_attention",
    )(qT, qiT, wT, znat, k, vT, ke, ko)


def _ssd_kernel(xbc_ref, z_ref, dt_ref, cw_ref, cb_ref, dtb_ref, a_ref, dsk_ref, nw_ref, o_ref,
                buf_ref, tail_ref, state_ref, *, tt):
    t = pl.program_id(1)

    @pl.when(t == 0)
    def _():
        tail_ref[...] = jnp.zeros_like(tail_ref)
        state_ref[...] = jnp.zeros_like(state_ref)

    xin = xbc_ref[...]
    buf_ref[0:8, :] = tail_ref[...]
    buf_ref[8:8 + tt, :] = xin
    tail_ref[...] = xin[tt - 8:tt, :]
    conv = cb_ref[...] + cw_ref[3:4, :] * xin
    for jtap in range(SSD_CONV - 1):
        conv += cw_ref[jtap:jtap + 1, :] * buf_ref[5 + jtap:5 + jtap + tt, :]
    xc = _silu(conv)

    dtv = dt_ref[...] + dtb_ref[...]
    dtv = jnp.maximum(dtv, 0.0) + jnp.log(1.0 + jnp.exp(-jnp.abs(dtv)))
    neg_a = -jnp.exp(a_ref[...])
    dta = dtv * neg_a
    r_i = lax.broadcasted_iota(I32, (CHUNK, CHUNK), 0)
    c_i = lax.broadcasted_iota(I32, (CHUNK, CHUNK), 1)
    tril = r_i >= c_i
    tril_b = tril.astype(BF16)

    for c in range(tt // CHUNK):
        r0 = c * CHUNK
        dta_c = dta[r0:r0 + CHUNK]
        hi = dta_c.astype(BF16)
        lo = (dta_c - hi.astype(F32)).astype(BF16)
        a_cum = _nn(tril_b, hi) + _nn(tril_b, lo)
        a_cum_t = a_cum.T
        a_last = a_cum[CHUNK - 1:CHUNK, :]
        dt_c = dtv[r0:r0 + CHUNK]
        for g in range(SSD_GROUPS):
            bg = xc[r0:r0 + CHUNK, SSD_WIDTH + g * SSD_N:SSD_WIDTH + (g + 1) * SSD_N]
            cg = xc[r0:r0 + CHUNK, SSD_WIDTH + SSD_GROUPS * SSD_N + g * SSD_N:
                    SSD_WIDTH + SSD_GROUPS * SSD_N + (g + 1) * SSD_N]
            bg16, cg16 = bg.astype(BF16), cg.astype(BF16)
            cb = _nt(cg16, bg16)
            prev = state_ref[g]
            y_off = _nn(cg16, prev.astype(BF16))
            ys, xdds, cdec = [], [], []
            for r in range(SSD_HEADS // SSD_GROUPS):
                h = g * (SSD_HEADS // SSD_GROUPS) + r
                xh = xc[r0:r0 + CHUNK, h * SSD_P:(h + 1) * SSD_P]
                ac_col = a_cum[:, h:h + 1]
                ac_row = a_cum_t[h:h + 1, :]
                lmat = jnp.exp(jnp.where(tril, ac_col - ac_row, -jnp.inf))
                xdt = xh * dt_c[:, h:h + 1]
                y = _nn((cb * lmat).astype(BF16), xdt.astype(BF16))
                y += y_off[:, r * SSD_P:(r + 1) * SSD_P] * jnp.exp(ac_col)
                y += dsk_ref[:, h:h + 1] * xh
                ys.append(y)
                xdds.append(xdt * jnp.exp(a_last[:, h:h + 1] - ac_col))
                cdec.append(jnp.broadcast_to(jnp.exp(a_last[:, h:h + 1]), (1, SSD_P)))
            xdd = jnp.concatenate(xdds, axis=1).astype(BF16)
            states = _nn(bg.T.astype(BF16), xdd)
            state_ref[g] = prev * jnp.concatenate(cdec, axis=1) + states
            yg = jnp.concatenate(ys, axis=1)
            gw = SSD_WIDTH // SSD_GROUPS
            gz = yg * _silu(z_ref[r0:r0 + CHUNK, g * gw:(g + 1) * gw])
            gz = gz * lax.rsqrt(jnp.mean(gz * gz, axis=1, keepdims=True) + RMS_EPS)
            o_ref[r0:r0 + CHUNK, g * gw:(g + 1) * gw] = (gz * nw_ref[:, g * gw:(g + 1) * gw]).astype(o_ref.dtype)


def _ssd(nat, conv_w, conv_b, dt_bias, a_log, d_skip, norm_w, bsz, seq, tt):
    m = bsz * seq
    nt = seq // tt
    row = lambda b, t: b * nt + t
    full = lambda a: pl.BlockSpec(a.shape, lambda b, t: (0, 0))
    return pl.pallas_call(
        functools.partial(_ssd_kernel, tt=tt),
        grid=(bsz, nt),
        in_specs=[pl.BlockSpec((tt, XBC_WIDTH), lambda b, t: (row(b, t), 0)),
                  pl.BlockSpec((tt, SSD_WIDTH), lambda b, t: (row(b, t), XBC_WIDTH // SSD_WIDTH)),
                  pl.BlockSpec((tt, 128), lambda b, t: (row(b, t), (XBC_WIDTH + SSD_WIDTH + ATTN_WIDTH) // 128)),
                  full(conv_w), full(conv_b), full(dt_bias), full(a_log), full(d_skip), full(norm_w)],
        out_specs=pl.BlockSpec((tt, SSD_WIDTH), lambda b, t: (row(b, t), 0)),
        out_shape=jax.ShapeDtypeStruct((m, SSD_WIDTH), BF16),
        scratch_shapes=[pltpu.VMEM((tt + 8, XBC_WIDTH), F32), pltpu.VMEM((8, XBC_WIDTH), F32),
                        pltpu.VMEM((SSD_GROUPS, SSD_N, SSD_WIDTH // SSD_GROUPS), F32)],
        compiler_params=pltpu.CompilerParams(dimension_semantics=("arbitrary", "arbitrary"),
                                             vmem_limit_bytes=VMEM_LIMIT),
        name="ssd_mixer",
    )(nat, nat, nat, conv_w, conv_b, dt_bias, a_log, d_skip, norm_w)


def _out_kernel(oa_ref, os_ref, x_ref, wa_ref, ws_ref, g_ref, b_ref, o_ref, *, alpha):
    sub = _nn(oa_ref[...], wa_ref[...]) + _nn(os_ref[...], ws_ref[...])
    y = alpha * x_ref[...] + sub
    mu = jnp.mean(y, axis=1, keepdims=True)
    yc = y - mu
    var = jnp.mean(yc * yc, axis=1, keepdims=True)
    o_ref[...] = yc * lax.rsqrt(var + LN_EPS) * g_ref[...] + b_ref[...]


def _out_proj(oa, os_, x2, wa, ws, g, b, alpha, tm):
    m = x2.shape[0]
    full = lambda a: pl.BlockSpec(a.shape, lambda i: (0, 0))
    rowblk = lambda w: pl.BlockSpec((tm, w), lambda i: (i, 0))
    return pl.pallas_call(
        functools.partial(_out_kernel, alpha=alpha),
        grid=(m // tm,),
        in_specs=[rowblk(ATTN_WIDTH), rowblk(SSD_WIDTH), rowblk(D_MODEL), full(wa), full(ws), full(g), full(b)],
        out_specs=rowblk(D_MODEL),
        out_shape=jax.ShapeDtypeStruct((m, D_MODEL), F32),
        compiler_params=pltpu.CompilerParams(dimension_semantics=("arbitrary",), vmem_limit_bytes=VMEM_LIMIT),
        name="out_proj_ln",
    )(oa, os_, x2, wa, ws, g, b)


def _rope_tables(seq, rot):
    half = rot // 2
    inv = ROPE_THETA ** (-jnp.arange(half, dtype=F32) * 2.0 / rot)
    ang = inv[:, None] * jnp.arange(seq, dtype=F32)[None, :]
    return jnp.cos(ang), jnp.sin(ang)


def _pad_lanes(v, width=128):
    v = v.reshape(1, -1).astype(F32)
    return jnp.pad(v, ((0, 0), (0, width - v.shape[1])))


def _layer(h, w_in, w_out, conv_w, conv_b, dt_bias, a_log, d_skip, norm_w, ln_g, ln_b, alpha):
    bsz, seq, _ = h.shape
    m = bsz * seq
    assert seq % QB == 0 and seq % 512 == 0
    x2 = h.reshape(m, D_MODEL)
    xb = x2.astype(BF16)

    o_q, o_k, o_v, o_za, o_qi, o_ki, o_wi, o_zs, o_xbc, o_dt = (
        0, 1024, 1280, 1536, 2560, 3584, 3648, 3664, 4688, 6736)
    wt = w_in.T.astype(BF16)
    wq = wt[o_q:o_k]
    wkv = wt[o_k:o_za]
    wqi = wt[o_qi:o_ki]
    wkw = jnp.pad(wt[o_ki:o_zs], ((0, 128 - (o_zs - o_ki)), (0, 0)))
    wn = jnp.concatenate([w_in[:, o_xbc:o_dt], w_in[:, o_zs:o_xbc], w_in[:, o_za:o_qi],
                          jnp.pad(w_in[:, o_dt:], ((0, 0), (0, 128 - SSD_HEADS)))],
                         axis=1).astype(BF16)

    ca, sa = _rope_tables(seq, HEAD_DIM // ROPE_DIV)
    ci, si = _rope_tables(seq, IDX_DIM // ROPE_DIV)

    qT, kT, vT, qiT, kwT = _proj_t(xb, wq, wkv, wqi, wkw, ca, sa, ci, si, seq, tm=512)
    nat = _proj_n(xb, wn, tm=512, tn=1408)

    k = kT.T
    kidx = kwT[:IDX_DIM].T.astype(BF16)
    zpad = jnp.zeros_like(kidx)
    ke = jnp.concatenate([kidx, zpad], axis=1)
    ko = jnp.concatenate([zpad, kidx], axis=1)
    wT = kwT[IDX_DIM:IDX_DIM + IDX_HEADS]

    o_attn = _attention(qT, qiT, wT, nat, (XBC_WIDTH + SSD_WIDTH) // ATTN_WIDTH, k, vT, ke, ko, bsz, seq)
    o_ssd = _ssd(nat, conv_w, conv_b.reshape(1, -1), _pad_lanes(dt_bias), _pad_lanes(a_log),
                 _pad_lanes(d_skip), norm_w.reshape(1, -1), bsz, seq, tt=256)

    wo = w_out.astype(BF16)
    out = _out_proj(o_attn, o_ssd, x2, wo[:ATTN_WIDTH], wo[ATTN_WIDTH:], ln_g.reshape(1, -1),
                    ln_b.reshape(1, -1), alpha, tm=512)
    return out.reshape(bsz, seq, D_MODEL)


def kernel(x, w_in, w_out, conv_w, conv_b, dt_bias, a_log, d_skip, ssd_norm_w, ln_g, ln_b):
    depth = w_in.shape[0]
    alpha = (2.0 * depth) ** 0.25
    h = x
    for layer in range(depth):
        h = _layer(h, w_in[layer], w_out[layer], conv_w[layer], conv_b[layer], dt_bias[layer],
                   a_log[layer], d_skip[layer], ssd_norm_w[layer], ln_g[layer], ln_b[layer], alpha)
    return h
```

```python
import functools
import math

import jax
import jax.numpy as jnp
from jax import lax
from jax.experimental import pallas as pl
from jax.experimental.pallas import tpu as pltpu

F32 = jnp.float32
BF16 = jnp.bfloat16
I32 = jnp.int32

D_MODEL = 2048
CHUNK = 64
ATTN_WIDTH = 1024
SSD_WIDTH = 1024
HEAD_DIM = 128
Q_HEADS = 8
KV_HEADS = 2
ROPE_THETA = 500000.0
ROPE_DIV = 4
IDX_HEADS = 16
IDX_DIM = 64
TOPK_MAX = 256
SSD_P = 64
SSD_HEADS = 16
SSD_GROUPS = 4
SSD_N = 128
SSD_CONV = 4
XBC_WIDTH = SSD_WIDTH + 2 * SSD_GROUPS * SSD_N
LN_EPS = 1e-5
RMS_EPS = 1e-5

VMEM_LIMIT = 56 * 1024 * 1024
INT_MIN = -(2 ** 31)

QB = 256
KT = 128
KTA = 256


def _nt(a, b):
    return lax.dot_general(a, b, (((1,), (1,)), ((), ())), preferred_element_type=F32)


def _tn(a, b):
    return lax.dot_general(a, b, (((0,), (0,)), ((), ())), preferred_element_type=F32)


def _nn(a, b):
    return jnp.dot(a, b, preferred_element_type=F32)


def _silu(v):
    return v * (1.0 / (1.0 + jnp.exp(-v)))


def _rope_rows(t, cos, sin, nheads, hd, half):
    pieces = []
    for h in range(nheads):
        b = h * hd
        x1 = t[b:b + half]
        x2 = t[b + half:b + 2 * half]
        pieces += [x1 * cos - x2 * sin, x2 * cos + x1 * sin, t[b + 2 * half:b + hd]]
    return jnp.concatenate(pieces, axis=0)


def _proj_t_kernel(x_ref, wq_ref, wkv_ref, wqi_ref, wkw_ref, ca_ref, sa_ref, ci_ref, si_ref,
                   q_ref, k_ref, v_ref, qi_ref, kw_ref):
    x = x_ref[...]
    ca, sa, ci, si = ca_ref[...], sa_ref[...], ci_ref[...], si_ref[...]
    qscale = HEAD_DIM ** -0.5 * math.log2(math.e)
    tq = _nt(wq_ref[...], x)
    q_ref[...] = (_rope_rows(tq, ca, sa, Q_HEADS, HEAD_DIM, 16) * qscale).astype(BF16)
    tkv = _nt(wkv_ref[...], x)
    k_ref[...] = _rope_rows(tkv[:KV_HEADS * HEAD_DIM], ca, sa, KV_HEADS, HEAD_DIM, 16).astype(BF16)
    for jj in range(v_ref.shape[0]):
        v_ref[jj] = tkv[KV_HEADS * HEAD_DIM:, jj * KTA:(jj + 1) * KTA].astype(BF16)
    tqi = _nt(wqi_ref[...], x)
    qi_ref[...] = _rope_rows(tqi, ci, si, IDX_HEADS, IDX_DIM, 8).astype(BF16)
    tkw = _nt(wkw_ref[...], x)
    kidx = _rope_rows(tkw[:IDX_DIM], ci, si, 1, IDX_DIM, 8)
    wscale = IDX_HEADS ** -0.5 * IDX_DIM ** -0.5
    kw_ref[...] = jnp.concatenate([kidx, tkw[IDX_DIM:] * wscale], axis=0)


def _proj_t(xb, wq, wkv, wqi, wkw, ca, sa, ci, si, seq, tm):
    m, kdim = xb.shape
    nper = seq // tm
    full = lambda a: pl.BlockSpec(a.shape, lambda i: (0, 0))
    tab = lambda a: pl.BlockSpec((a.shape[0], tm), lambda i: (0, i % nper))
    outs = [(1024, BF16), (256, BF16), (1024, BF16), (128, F32)]
    flat = lambda r: pl.BlockSpec((r, tm), lambda i: (0, i))
    vspec = pl.BlockSpec((tm // KTA, 256, KTA), lambda i: (i, 0, 0))
    vshape = jax.ShapeDtypeStruct((m // KTA, 256, KTA), BF16)
    shp = [jax.ShapeDtypeStruct((r, m), d) for r, d in outs]
    return pl.pallas_call(
        _proj_t_kernel,
        grid=(m // tm,),
        in_specs=[pl.BlockSpec((tm, kdim), lambda i: (i, 0)), full(wq), full(wkv), full(wqi), full(wkw),
                  tab(ca), tab(sa), tab(ci), tab(si)],
        out_specs=[flat(1024), flat(256), vspec, flat(1024), flat(128)],
        out_shape=[shp[0], shp[1], vshape, shp[2], shp[3]],
        compiler_params=pltpu.CompilerParams(dimension_semantics=("arbitrary",), vmem_limit_bytes=VMEM_LIMIT),
        name="proj_t",
    )(xb, wq, wkv, wqi, wkw, ca, sa, ci, si)


def _proj_n_kernel(x_ref, w_ref, o_ref):
    o_ref[...] = _nn(x_ref[...], w_ref[...]).astype(o_ref.dtype)


def _proj_n(xb, w, tm, tn):
    m, kdim = xb.shape
    n = w.shape[1]
    return pl.pallas_call(
        _proj_n_kernel,
        grid=(n // tn, m // tm),
        in_specs=[pl.BlockSpec((tm, kdim), lambda j, i: (i, 0)), pl.BlockSpec((kdim, tn), lambda j, i: (0, j))],
        out_specs=pl.BlockSpec((tm, tn), lambda j, i: (i, j)),
        out_shape=jax.ShapeDtypeStruct((m, n), F32),
        compiler_params=pltpu.CompilerParams(dimension_semantics=("arbitrary", "arbitrary"),
                                             vmem_limit_bytes=VMEM_LIMIT),
        name="proj_n",
    )(xb, w)


def _fold64(x, op2):
    parts = [x[a * 64:(a + 1) * 64] for a in range(x.shape[0] // 64)]
    while len(parts) > 1:
        parts = [op2(parts[2 * a], parts[2 * a + 1]) for a in range(len(parts) // 2)]
    return parts[0]


def _fold8(x, op):
    return op(x.reshape(x.shape[0] // 8, 8, x.shape[1]), axis=0)


def _attn_kernel(qT_ref, qiT_ref, wT_ref, z_ref, k_ref, vT_ref, kidx_ref, o_ref,
                 bias_ref, m_ref, l_ref, acc_ref, s_ref, *, topk):
    i = pl.program_id(1)
    nfull = 2 * i
    q_chunk = (i * QB + lax.broadcasted_iota(I32, (1, QB), 1)) // CHUNK

    def score_tile(j):
        rows = pl.ds(pl.multiple_of(j * KT, KT), KT)
        kt = kidx_ref[rows, :]
        acc = jnp.zeros((KT, QB), F32)
        for h in range(IDX_HEADS):
            qh = qiT_ref[h * IDX_DIM:(h + 1) * IDX_DIM, :]
            acc += jnp.maximum(_nn(kt, qh), 0.0) * wT_ref[h:h + 1, :]
        return rows, acc

    def full_tiles(j2, carry):
        mn, mx = carry
        for jj in range(2):
            rows, sc = score_tile(2 * j2 + jj)
            bias_ref[rows, :] = sc
            mn = jnp.minimum(mn, _fold8(sc, jnp.min))
            mx = jnp.maximum(mx, _fold8(sc, jnp.max))
        return mn, mx

    mn8, mx8 = lax.fori_loop(0, i, full_tiles,
                             (jnp.full((8, QB), jnp.inf, F32), jnp.full((8, QB), -jnp.inf, F32)))
    for jj in range(QB // KT):
        j = nfull + jj
        rows, sc = score_tile(j)
        k_chunk = (j * KT + lax.broadcasted_iota(I32, (KT, 1), 0)) // CHUNK
        adm = k_chunk <= q_chunk
        bias_ref[rows, :] = jnp.where(adm, sc, -jnp.inf)
        mn8 = jnp.minimum(mn8, _fold8(jnp.where(adm, sc, jnp.inf), jnp.min))
        mx8 = jnp.maximum(mx8, _fold8(jnp.where(adm, sc, -jnp.inf), jnp.max))
    lo0 = jnp.min(mn8, axis=0, keepdims=True)
    hi0 = jnp.max(mx8, axis=0, keepdims=True)

    def count_ge(t):
        def body(j, c):
            parts = []
            for a in range(KTA // 64):
                rows = pl.ds(pl.multiple_of(j * KTA + a * 64, 64), 64)
                parts.append(jnp.where(bias_ref[rows, :] >= t, 1.0, 0.0))
            return c + ((parts[0] + parts[1]) + (parts[2] + parts[3]))

        c64 = lax.fori_loop(0, i + 1, body, jnp.zeros((64, QB), F32))
        return jnp.sum(c64, axis=0, keepdims=True)

    kf = float(topk)
    n_adm = ((q_chunk + 1) * CHUNK).astype(F32)
    all_in = n_adm <= kf
    top_tie = count_ge(hi0) >= kf
    thr0 = jnp.where(all_in, lo0, hi0)
    done0 = jnp.where(all_in | top_tie, 1.0, 0.0)

    def search_cond(st):
        return jnp.min(st[3]) < 0.5

    def search_body(st):
        lo, hi, thr, done = st
        for _ in range(2):
            mid = 0.5 * lo + 0.5 * hi
            c = count_ge(mid)
            hit = c == kf
            stuck = (mid <= lo) | (mid >= hi)
            newly = (done < 0.5) & (hit | stuck)
            thr = jnp.where(newly, jnp.where(hit, mid, lo), thr)
            done = jnp.where(newly, 1.0, done)
            ge = c >= kf
            lo = jnp.where(ge, mid, lo)
            hi = jnp.where(ge, hi, mid)
        return lo, hi, thr, done

    _, _, thr, _ = lax.while_loop(search_cond, search_body, (lo0, hi0, thr0, done0))

    def to_bias(j, c):
        rows = pl.ds(pl.multiple_of(j * KTA, KTA), KTA)
        bias_ref[rows, :] = jnp.where(bias_ref[rows, :] >= thr, 0.0, -jnp.inf)
        return c

    lax.fori_loop(0, i + 1, to_bias, 0)

    m_ref[...] = jnp.full(m_ref.shape, -jnp.inf, F32)
    l_ref[...] = jnp.zeros(l_ref.shape, F32)
    acc_ref[...] = jnp.zeros(acc_ref.shape, F32)

    rep = Q_HEADS // KV_HEADS
    gq = rep * QB

    def masked_logits(j, slot):
        jc = jnp.minimum(j, i)
        rows = pl.ds(pl.multiple_of(jc * KTA, KTA), KTA)
        bias = jnp.where(j <= i, bias_ref[rows, :], -jnp.inf)
        bias_g = jnp.concatenate([bias] * rep, axis=1)
        for g in range(KV_HEADS):
            qg = jnp.concatenate([qT_ref[(g * rep + r) * HEAD_DIM:(g * rep + r + 1) * HEAD_DIM, :]
                                  for r in range(rep)], axis=1)
            s_ref[slot, :, g * gq:(g + 1) * gq] = _nn(k_ref[rows, g * HEAD_DIM:(g + 1) * HEAD_DIM], qg) + bias_g

    def softmax_pv(j, slot):
        jc = jnp.minimum(j, i)
        for g in range(KV_HEADS):
            s = s_ref[slot, :, g * gq:(g + 1) * gq]
            m = m_ref[g:g + 1, :]
            m_new = jnp.maximum(m, jnp.max(_fold64(s, jnp.maximum), axis=0, keepdims=True))
            m_safe = jnp.where(m_new == -jnp.inf, 0.0, m_new)
            alpha = jnp.exp2(m - m_safe)
            p = jnp.exp2(s - m_safe)
            l_ref[g:g + 1, :] = alpha * l_ref[g:g + 1, :] + jnp.sum(_fold64(p, jnp.add), axis=0, keepdims=True)
            pv = _nn(vT_ref[jc, g * HEAD_DIM:(g + 1) * HEAD_DIM, :], p.astype(BF16))
            acc_ref[g] = alpha * acc_ref[g] + pv
            m_ref[g:g + 1, :] = m_new

    masked_logits(0, 0)

    def kv_step(jj, c):
        a = 2 * jj
        masked_logits(a + 1, 1)
        softmax_pv(a, 0)
        masked_logits(a + 2, 0)
        softmax_pv(a + 1, 1)
        return c

    lax.fori_loop(0, (i + 2) // 2, kv_step, 0)
    for h in range(Q_HEADS):
        g, r = divmod(h, rep)
        oh = (acc_ref[g, :, r * QB:(r + 1) * QB] / l_ref[g:g + 1, r * QB:(r + 1) * QB]).T
        zh = z_ref[:, h * HEAD_DIM:(h + 1) * HEAD_DIM]
        o_ref[:, h * HEAD_DIM:(h + 1) * HEAD_DIM] = (oh * _silu(zh)).astype(o_ref.dtype)


def _attention(qT, qiT, wT, znat, zcol, k, vT, kidx, bsz, seq):
    nq = seq // QB
    topk = min(TOPK_MAX, seq // 4)
    m = bsz * seq
    step = lambda r: pl.BlockSpec((r, QB), lambda b, i: (0, b * nq + i))
    return pl.pallas_call(
        functools.partial(_attn_kernel, topk=topk),
        grid=(bsz, nq),
        in_specs=[step(1024), step(1024), step(IDX_HEADS),
                  pl.BlockSpec((QB, ATTN_WIDTH), lambda b, i: (b * nq + i, zcol)),
                  pl.BlockSpec((seq, 256), lambda b, i: (b, 0)),
                  pl.BlockSpec((seq // KTA, 256, KTA), lambda b, i: (b, 0, 0)),
                  pl.BlockSpec((seq, IDX_DIM), lambda b, i: (b, 0))],
        out_specs=pl.BlockSpec((QB, ATTN_WIDTH), lambda b, i: (b * nq + i, 0)),
        out_shape=jax.ShapeDtypeStruct((m, ATTN_WIDTH), BF16),
        scratch_shapes=[pltpu.VMEM((seq, QB), F32),
                        pltpu.VMEM((KV_HEADS, Q_HEADS // KV_HEADS * QB), F32),
                        pltpu.VMEM((KV_HEADS, Q_HEADS // KV_HEADS * QB), F32),
                        pltpu.VMEM((KV_HEADS, HEAD_DIM, Q_HEADS // KV_HEADS * QB), F32),
                        pltpu.VMEM((2, KTA, Q_HEADS * QB), F32)],
        compiler_params=pltpu.CompilerParams(dimension_semantics=("arbitrary", "arbitrary"),
                                             vmem_limit_bytes=VMEM_LIMIT),
        name="dsa_attention",
    )(qT, qiT, wT, znat, k, vT, kidx)


def _ssd_kernel(xbc_ref, z_ref, dt_ref, cw_ref, cb_ref, dtb_ref, a_ref, dsk_ref, nw_ref, e2_ref, o_ref,
                buf_ref, tail_ref, state_ref, xc_ref, xdt_ref, acum_ref, *, tt):
    t = pl.program_id(1)

    @pl.when(t == 0)
    def _():
        tail_ref[...] = jnp.zeros_like(tail_ref)
        state_ref[...] = jnp.zeros_like(state_ref)

    xin = xbc_ref[...]
    buf_ref[0:8, :] = tail_ref[...]
    buf_ref[8:8 + tt, :] = xin
    tail_ref[...] = xin[tt - 8:tt, :]
    conv = cb_ref[...] + cw_ref[3:4, :] * xin
    for jtap in range(SSD_CONV - 1):
        conv += cw_ref[jtap:jtap + 1, :] * buf_ref[5 + jtap:5 + jtap + tt, :]
    xc = _silu(conv)

    xc_ref[...] = xc
    xs = xc[:, :SSD_WIDTH]

    dtv = dt_ref[...] + dtb_ref[...]
    dtv = jnp.maximum(dtv, 0.0) + jnp.log(1.0 + jnp.exp(-jnp.abs(dtv)))
    dhi = dtv.astype(BF16).astype(F32)
    lane = lax.broadcasted_iota(I32, (1, 128), 1)
    dsplit = jnp.where(lane < SSD_HEADS, dhi, dtv - dhi).astype(BF16)
    dt_e = _nn(dsplit, e2_ref[...])
    xdt_ref[...] = xs * dt_e
    dta_e = dt_e * (-jnp.exp(a_ref[...]))
    r_i = lax.broadcasted_iota(I32, (tt, tt), 0)
    c_i = lax.broadcasted_iota(I32, (tt, tt), 1)
    tril_bd = ((r_i >= c_i) & ((r_i >> 6) == (c_i >> 6))).astype(BF16)
    ahi = dta_e.astype(BF16)
    alo = (dta_e - ahi.astype(F32)).astype(BF16)
    acum_ref[...] = _nn(tril_bd, ahi) + _nn(tril_bd, alo)

    gw = SSD_WIDTH // SSD_GROUPS
    s_row = lax.broadcasted_iota(I32, (CHUNK, gw), 0)
    s_lane = lax.broadcasted_iota(I32, (CHUNK, gw), 1) & (CHUNK - 1)
    diag = s_row == s_lane
    tril = s_row >= s_lane
    bd_r = lax.broadcasted_iota(I32, (2 * SSD_P, 2 * SSD_P), 0) >> 6
    bd_c = lax.broadcasted_iota(I32, (2 * SSD_P, 2 * SSD_P), 1) >> 6
    blockdiag = bd_r == bd_c

    def chunk_step(c, carry):
        rows = pl.ds(pl.multiple_of(c * CHUNK, CHUNK), CHUNK)
        for g in range(SSD_GROUPS):
            gl = slice(g * gw, (g + 1) * gw)
            bg = xc_ref[rows, SSD_WIDTH + g * SSD_N:SSD_WIDTH + (g + 1) * SSD_N]
            cg = xc_ref[rows, SSD_WIDTH + (SSD_GROUPS + g) * SSD_N:SSD_WIDTH + (SSD_GROUPS + g + 1) * SSD_N]
            bg16, cg16 = bg.astype(BF16), cg.astype(BF16)
            a_g = acum_ref[rows, gl]
            a_last = a_g[CHUNK - 1:CHUNK, :]
            xdt_g = xdt_ref[rows, gl]
            cb = _nt(cg16, jnp.concatenate([bg16] * (gw // CHUNK), axis=0))
            a_row = jnp.sum(jnp.where(diag, a_g, 0.0), axis=0, keepdims=True)
            lmat = jnp.exp(jnp.where(tril, a_g - a_row, -jnp.inf))
            mmat = (cb * lmat).astype(BF16)
            ys = []
            for q in range(gw // (2 * SSD_P)):
                ql = slice(q * 2 * SSD_P, (q + 1) * 2 * SSD_P)
                x2 = xdt_g[:, ql]
                wq = jnp.where(blockdiag, jnp.concatenate([x2, x2], axis=0), 0.0).astype(BF16)
                ys.append(_nn(mmat[:, ql], wq))
            y = jnp.concatenate(ys, axis=1)
            prev = state_ref[g]
            y += _nn(cg16, prev.astype(BF16)) * jnp.exp(a_g)
            y += dsk_ref[:, gl] * xc_ref[rows, gl]
            xdd = (xdt_g * jnp.exp(a_last - a_g)).astype(BF16)
            state_ref[g] = prev * jnp.exp(a_last) + _nn(bg.T.astype(BF16), xdd)
            gz = y * _silu(z_ref[rows, gl])
            gz = gz * lax.rsqrt(jnp.mean(gz * gz, axis=1, keepdims=True) + RMS_EPS)
            o_ref[rows, gl] = (gz * nw_ref[:, gl]).astype(o_ref.dtype)
        return carry

    lax.fori_loop(0, tt // CHUNK, chunk_step, 0)


def _ssd(nat, conv_w, conv_b, dt_bias, a_log, d_skip, norm_w, bsz, seq, tt):
    m = bsz * seq
    nt = seq // tt
    row = lambda b, t: b * nt + t
    full = lambda a: pl.BlockSpec(a.shape, lambda b, t: (0, 0))
    expand = lambda v: jnp.repeat(v.astype(F32), SSD_P).reshape(1, SSD_WIDTH)
    dtb2 = _pad_lanes(jnp.concatenate([dt_bias, dt_bias]))
    j_i = lax.broadcasted_iota(I32, (128, SSD_WIDTH), 0)
    h_i = lax.broadcasted_iota(I32, (128, SSD_WIDTH), 1) // SSD_P
    e2 = ((j_i == h_i) | (j_i == h_i + SSD_HEADS)).astype(BF16)
    return pl.pallas_call(
        functools.partial(_ssd_kernel, tt=tt),
        grid=(bsz, nt),
        in_specs=[pl.BlockSpec((tt, XBC_WIDTH), lambda b, t: (row(b, t), 0)),
                  pl.BlockSpec((tt, SSD_WIDTH), lambda b, t: (row(b, t), XBC_WIDTH // SSD_WIDTH)),
                  pl.BlockSpec((tt, 128), lambda b, t: (row(b, t), (XBC_WIDTH + SSD_WIDTH + ATTN_WIDTH) // 128)),
                  full(conv_w), full(conv_b), full(dtb2), pl.BlockSpec((1, SSD_WIDTH), lambda b, t: (0, 0)),
                  pl.BlockSpec((1, SSD_WIDTH), lambda b, t: (0, 0)), full(norm_w), full(e2)],
        out_specs=pl.BlockSpec((tt, SSD_WIDTH), lambda b, t: (row(b, t), 0)),
        out_shape=jax.ShapeDtypeStruct((m, SSD_WIDTH), BF16),
        scratch_shapes=[pltpu.VMEM((tt + 8, XBC_WIDTH), F32), pltpu.VMEM((8, XBC_WIDTH), F32),
                        pltpu.VMEM((SSD_GROUPS, SSD_N, SSD_WIDTH // SSD_GROUPS), F32),
                        pltpu.VMEM((tt, XBC_WIDTH), F32), pltpu.VMEM((tt, SSD_WIDTH), F32),
                        pltpu.VMEM((tt, SSD_WIDTH), F32)],
        compiler_params=pltpu.CompilerParams(dimension_semantics=("arbitrary", "arbitrary"),
                                             vmem_limit_bytes=VMEM_LIMIT),
        name="ssd_mixer",
    )(nat, nat, nat, conv_w, conv_b, dtb2, expand(a_log), expand(d_skip), norm_w, e2)


def _out_kernel(oa_ref, os_ref, x_ref, wa_ref, ws_ref, g_ref, b_ref, o_ref, *, alpha):
    sub = _nn(oa_ref[...], wa_ref[...]) + _nn(os_ref[...], ws_ref[...])
    y = alpha * x_ref[...] + sub
    mu = jnp.mean(y, axis=1, keepdims=True)
    yc = y - mu
    var = jnp.mean(yc * yc, axis=1, keepdims=True)
    o_ref[...] = yc * lax.rsqrt(var + LN_EPS) * g_ref[...] + b_ref[...]


def _out_proj(oa, os_, x2, wa, ws, g, b, alpha, tm):
    m = x2.shape[0]
    full = lambda a: pl.BlockSpec(a.shape, lambda i: (0, 0))
    rowblk = lambda w: pl.BlockSpec((tm, w), lambda i: (i, 0))
    return pl.pallas_call(
        functools.partial(_out_kernel, alpha=alpha),
        grid=(m // tm,),
        in_specs=[rowblk(ATTN_WIDTH), rowblk(SSD_WIDTH), rowblk(D_MODEL), full(wa), full(ws), full(g), full(b)],
        out_specs=rowblk(D_MODEL),
        out_shape=jax.ShapeDtypeStruct((m, D_MODEL), F32),
        compiler_params=pltpu.CompilerParams(dimension_semantics=("arbitrary",), vmem_limit_bytes=VMEM_LIMIT),
        name="out_proj_ln",
    )(oa, os_, x2, wa, ws, g, b)


def _rope_tables(seq, rot):
    half = rot // 2
    inv = ROPE_THETA ** (-jnp.arange(half, dtype=F32) * 2.0 / rot)
    ang = inv[:, None] * jnp.arange(seq, dtype=F32)[None, :]
    return jnp.cos(ang), jnp.sin(ang)


def _pad_lanes(v, width=128):
    v = v.reshape(1, -1).astype(F32)
    return jnp.pad(v, ((0, 0), (0, width - v.shape[1])))


def _layer(h, w_in, w_out, conv_w, conv_b, dt_bias, a_log, d_skip, norm_w, ln_g, ln_b, alpha):
    bsz, seq, _ = h.shape
    m = bsz * seq
    assert seq % QB == 0 and seq % 512 == 0
    x2 = h.reshape(m, D_MODEL)
    xb = x2.astype(BF16)

    o_q, o_k, o_v, o_za, o_qi, o_ki, o_wi, o_zs, o_xbc, o_dt = (
        0, 1024, 1280, 1536, 2560, 3584, 3648, 3664, 4688, 6736)
    wt = w_in.T.astype(BF16)
    wq = wt[o_q:o_k]
    wkv = wt[o_k:o_za]
    wqi = wt[o_qi:o_ki]
    wkw = jnp.pad(wt[o_ki:o_zs], ((0, 128 - (o_zs - o_ki)), (0, 0)))
    wn = jnp.concatenate([w_in[:, o_xbc:o_dt], w_in[:, o_zs:o_xbc], w_in[:, o_za:o_qi],
                          jnp.pad(jnp.concatenate([w_in[:, o_dt:], w_in[:, o_dt:]], axis=1),
                                  ((0, 0), (0, 128 - 2 * SSD_HEADS)))],
                         axis=1).astype(BF16)

    ca, sa = _rope_tables(seq, HEAD_DIM // ROPE_DIV)
    ci, si = _rope_tables(seq, IDX_DIM // ROPE_DIV)

    qT, kT, vT, qiT, kwT = _proj_t(xb, wq, wkv, wqi, wkw, ca, sa, ci, si, seq, tm=512)
    nat = _proj_n(xb, wn, tm=512, tn=1408)

    k = kT.T
    kidx = kwT[:IDX_DIM].T.astype(BF16)
    wT = kwT[IDX_DIM:IDX_DIM + IDX_HEADS]

    o_attn = _attention(qT, qiT, wT, nat, (XBC_WIDTH + SSD_WIDTH) // ATTN_WIDTH, k, vT, kidx, bsz, seq)
    o_ssd = _ssd(nat, conv_w, conv_b.reshape(1, -1), dt_bias, a_log, d_skip, norm_w.reshape(1, -1),
                 bsz, seq, tt=256)

    wo = w_out.astype(BF16)
    out = _out_proj(o_attn, o_ssd, x2, wo[:ATTN_WIDTH], wo[ATTN_WIDTH:], ln_g.reshape(1, -1),
                    ln_b.reshape(1, -1), alpha, tm=512)
    return out.reshape(bsz, seq, D_MODEL)


def kernel(x, w_in, w_out, conv_w, conv_b, dt_bias, a_log, d_skip, ssd_norm_w, ln_g, ln_b):
    depth = w_in.shape[0]
    alpha = (2.0 * depth) ** 0.25
    h = x
    for layer in range(depth):
        h = _layer(h, w_in[layer], w_out[layer], conv_w[layer], conv_b[layer], dt_bias[layer],
                   a_log[layer], d_skip[layer], ssd_norm_w[layer], ln_g[layer], ln_b[layer], alpha)
    return h
```

```python
import functools
import math

import jax
import jax.numpy as jnp
from jax import lax
from jax.experimental import pallas as pl
from jax.experimental.pallas import tpu as pltpu

F32 = jnp.float32
BF16 = jnp.bfloat16
I32 = jnp.int32

D_MODEL = 2048
CHUNK = 64
ATTN_WIDTH = 1024
SSD_WIDTH = 1024
HEAD_DIM = 128
Q_HEADS = 8
KV_HEADS = 2
ROPE_THETA = 500000.0
ROPE_DIV = 4
IDX_HEADS = 16
IDX_DIM = 64
TOPK_MAX = 256
SSD_P = 64
SSD_HEADS = 16
SSD_GROUPS = 4
SSD_N = 128
SSD_CONV = 4
XBC_WIDTH = SSD_WIDTH + 2 * SSD_GROUPS * SSD_N
LN_EPS = 1e-5
RMS_EPS = 1e-5

VMEM_LIMIT = 56 * 1024 * 1024
INT_MIN = -(2 ** 31)

QB = 256
KT = 128
KTA = 256


def _nt(a, b):
    return lax.dot_general(a, b, (((1,), (1,)), ((), ())), preferred_element_type=F32)


def _tn(a, b):
    return lax.dot_general(a, b, (((0,), (0,)), ((), ())), preferred_element_type=F32)


def _nn(a, b):
    return jnp.dot(a, b, preferred_element_type=F32)


def _silu(v):
    return v * (1.0 / (1.0 + jnp.exp(-v)))


def _rope_rows(t, cos, sin, nheads, hd, half):
    pieces = []
    for h in range(nheads):
        b = h * hd
        x1 = t[b:b + half]
        x2 = t[b + half:b + 2 * half]
        pieces += [x1 * cos - x2 * sin, x2 * cos + x1 * sin, t[b + 2 * half:b + hd]]
    return jnp.concatenate(pieces, axis=0)


def _proj_t_kernel(x_ref, wq_ref, wkv_ref, wqi_ref, wkw_ref, ca_ref, sa_ref, ci_ref, si_ref,
                   q_ref, k_ref, v_ref, qi_ref, kidx_ref, w_ref, xb_ref):
    x = x_ref[...].astype(BF16)
    xb_ref[...] = x
    ca, sa, ci, si = ca_ref[...], sa_ref[...], ci_ref[...], si_ref[...]
    qscale = HEAD_DIM ** -0.5 * math.log2(math.e)
    tq = _nt(wq_ref[...], x)
    q_ref[...] = (_rope_rows(tq, ca, sa, Q_HEADS, HEAD_DIM, 16) * qscale).astype(BF16)
    tkv = _nt(wkv_ref[...], x)
    k_ref[...] = _rope_rows(tkv[:KV_HEADS * HEAD_DIM], ca, sa, KV_HEADS, HEAD_DIM, 16).T.astype(BF16)
    for jj in range(v_ref.shape[0]):
        v_ref[jj] = tkv[KV_HEADS * HEAD_DIM:, jj * KTA:(jj + 1) * KTA].astype(BF16)
    tqi = _nt(wqi_ref[...], x)
    qi_ref[...] = _rope_rows(tqi, ci, si, IDX_HEADS, IDX_DIM, 8).astype(BF16)
    tkw = _nt(wkw_ref[...], x)
    kidx_ref[...] = _rope_rows(tkw[:IDX_DIM], ci, si, 1, IDX_DIM, 8).T.astype(BF16)
    w_ref[...] = tkw[IDX_DIM:IDX_DIM + IDX_HEADS] * (IDX_HEADS ** -0.5 * IDX_DIM ** -0.5)


def _proj_t(x2, wq, wkv, wqi, wkw, ca, sa, ci, si, seq, tm):
    m, kdim = x2.shape
    nper = seq // tm
    full = lambda a: pl.BlockSpec(a.shape, lambda i: (0, 0))
    tab = lambda a: pl.BlockSpec((a.shape[0], tm), lambda i: (0, i % nper))
    cols = lambda r: pl.BlockSpec((r, tm), lambda i: (0, i))
    rows = lambda c: pl.BlockSpec((tm, c), lambda i: (i, 0))
    vspec = pl.BlockSpec((tm // KTA, 256, KTA), lambda i: (i, 0, 0))
    sds = jax.ShapeDtypeStruct
    return pl.pallas_call(
        _proj_t_kernel,
        grid=(m // tm,),
        in_specs=[pl.BlockSpec((tm, kdim), lambda i: (i, 0)), full(wq), full(wkv), full(wqi), full(wkw),
                  tab(ca), tab(sa), tab(ci), tab(si)],
        out_specs=[cols(1024), rows(256), vspec, cols(1024), rows(IDX_DIM), cols(IDX_HEADS), rows(kdim)],
        out_shape=[sds((1024, m), BF16), sds((m, 256), BF16), sds((m // KTA, 256, KTA), BF16),
                   sds((1024, m), BF16), sds((m, IDX_DIM), BF16), sds((IDX_HEADS, m), F32), sds((m, kdim), BF16)],
        compiler_params=pltpu.CompilerParams(dimension_semantics=("arbitrary",), vmem_limit_bytes=VMEM_LIMIT),
        name="proj_t",
    )(x2, wq, wkv, wqi, wkw, ca, sa, ci, si)


def _proj_n_kernel(x_ref, w_ref, o_ref):
    o_ref[...] = _nn(x_ref[...], w_ref[...]).astype(o_ref.dtype)


def _proj_n(xb, w, tm, tn):
    m, kdim = xb.shape
    n = w.shape[1]
    return pl.pallas_call(
        _proj_n_kernel,
        grid=(n // tn, m // tm),
        in_specs=[pl.BlockSpec((tm, kdim), lambda j, i: (i, 0)), pl.BlockSpec((kdim, tn), lambda j, i: (0, j))],
        out_specs=pl.BlockSpec((tm, tn), lambda j, i: (i, j)),
        out_shape=jax.ShapeDtypeStruct((m, n), F32),
        compiler_params=pltpu.CompilerParams(dimension_semantics=("arbitrary", "arbitrary"),
                                             vmem_limit_bytes=VMEM_LIMIT),
        name="proj_n",
    )(xb, w)


def _fold64(x, op2):
    parts = [x[a * 64:(a + 1) * 64] for a in range(x.shape[0] // 64)]
    while len(parts) > 1:
        parts = [op2(parts[2 * a], parts[2 * a + 1]) for a in range(len(parts) // 2)]
    return parts[0]


def _fold8(x, op):
    return op(x.reshape(x.shape[0] // 8, 8, x.shape[1]), axis=0)


def _attn_kernel(qT_ref, qiT_ref, wT_ref, z_ref, k_ref, vT_ref, kidx_ref, o_ref,
                 bias_ref, m_ref, l_ref, acc_ref, s_ref, *, topk):
    i = pl.program_id(1)
    nfull = 2 * i
    q_chunk = (i * QB + lax.broadcasted_iota(I32, (1, QB), 1)) // CHUNK

    def score_tile(j):
        rows = pl.ds(pl.multiple_of(j * KT, KT), KT)
        kt = kidx_ref[rows, :]
        acc = jnp.zeros((KT, QB), F32)
        for h in range(IDX_HEADS):
            qh = qiT_ref[h * IDX_DIM:(h + 1) * IDX_DIM, :]
            acc += jnp.maximum(_nn(kt, qh), 0.0) * wT_ref[h:h + 1, :]
        return rows, acc

    def full_tiles(j2, carry):
        mn, mx = carry
        for jj in range(2):
            rows, sc = score_tile(2 * j2 + jj)
            bias_ref[rows, :] = sc
            mn = jnp.minimum(mn, _fold8(sc, jnp.min))
            mx = jnp.maximum(mx, _fold8(sc, jnp.max))
        return mn, mx

    mn8, mx8 = lax.fori_loop(0, i, full_tiles,
                             (jnp.full((8, QB), jnp.inf, F32), jnp.full((8, QB), -jnp.inf, F32)))
    for jj in range(QB // KT):
        j = nfull + jj
        rows, sc = score_tile(j)
        k_chunk = (j * KT + lax.broadcasted_iota(I32, (KT, 1), 0)) // CHUNK
        adm = k_chunk <= q_chunk
        bias_ref[rows, :] = jnp.where(adm, sc, -jnp.inf)
        mn8 = jnp.minimum(mn8, _fold8(jnp.where(adm, sc, jnp.inf), jnp.min))
        mx8 = jnp.maximum(mx8, _fold8(jnp.where(adm, sc, -jnp.inf), jnp.max))
    lo0 = jnp.min(mn8, axis=0, keepdims=True)
    hi0 = jnp.max(mx8, axis=0, keepdims=True)

    def count_ge(t):
        def body(j, c):
            parts = []
            for a in range(KTA // 64):
                rows = pl.ds(pl.multiple_of(j * KTA + a * 64, 64), 64)
                parts.append(jnp.where(bias_ref[rows, :] >= t, 1.0, 0.0))
            return c + ((parts[0] + parts[1]) + (parts[2] + parts[3]))

        c64 = lax.fori_loop(0, i + 1, body, jnp.zeros((64, QB), F32))
        return jnp.sum(c64, axis=0, keepdims=True)

    kf = float(topk)
    n_adm = ((q_chunk + 1) * CHUNK).astype(F32)
    all_in = n_adm <= kf
    top_tie = count_ge(hi0) >= kf
    thr0 = jnp.where(all_in, lo0, hi0)
    done0 = jnp.where(all_in | top_tie, 1.0, 0.0)

    def search_cond(st):
        return jnp.min(st[3]) < 0.5

    def search_body(st):
        lo, hi, thr, done = st
        for _ in range(2):
            mid = 0.5 * lo + 0.5 * hi
            c = count_ge(mid)
            hit = c == kf
            stuck = (mid <= lo) | (mid >= hi)
            newly = (done < 0.5) & (hit | stuck)
            thr = jnp.where(newly, jnp.where(hit, mid, lo), thr)
            done = jnp.where(newly, 1.0, done)
            ge = c >= kf
            lo = jnp.where(ge, mid, lo)
            hi = jnp.where(ge, hi, mid)
        return lo, hi, thr, done

    _, _, thr, _ = lax.while_loop(search_cond, search_body, (lo0, hi0, thr0, done0))

    def to_bias(j, c):
        rows = pl.ds(pl.multiple_of(j * KTA, KTA), KTA)
        bias_ref[rows, :] = jnp.where(bias_ref[rows, :] >= thr, 0.0, -jnp.inf)
        return c

    lax.fori_loop(0, i + 1, to_bias, 0)

    m_ref[...] = jnp.full(m_ref.shape, -jnp.inf, F32)
    l_ref[...] = jnp.zeros(l_ref.shape, F32)
    acc_ref[...] = jnp.zeros(acc_ref.shape, F32)

    rep = Q_HEADS // KV_HEADS
    gq = rep * QB
    ntile = bias_ref.shape[0] // KTA
    jpad = jnp.minimum(i + 1, ntile - 1)

    @pl.when(i + 1 < ntile)
    def _():
        bias_ref[pl.ds(pl.multiple_of((i + 1) * KTA, KTA), KTA), :] = jnp.full((KTA, QB), -jnp.inf, F32)

    def masked_logits(j, slot):
        jc = jnp.minimum(j, jpad)
        rows = pl.ds(pl.multiple_of(jc * KTA, KTA), KTA)
        for g in range(KV_HEADS):
            qg = jnp.concatenate([qT_ref[(g * rep + r) * HEAD_DIM:(g * rep + r + 1) * HEAD_DIM, :]
                                  for r in range(rep)], axis=1)
            sg = _nn(k_ref[rows, g * HEAD_DIM:(g + 1) * HEAD_DIM], qg)
            for r in range(rep):
                h = g * rep + r
                s_ref[slot, :, h * QB:(h + 1) * QB] = sg[:, r * QB:(r + 1) * QB] + bias_ref[rows, :]

    def softmax_pv(j, slot):
        jc = jnp.minimum(j, jpad)
        for h in range(Q_HEADS):
            g, r = divmod(h, rep)
            hl = slice(r * QB, (r + 1) * QB)
            s = s_ref[slot, :, h * QB:(h + 1) * QB]
            m = m_ref[g:g + 1, hl]
            m_new = jnp.maximum(m, jnp.max(_fold64(s, jnp.maximum), axis=0, keepdims=True))
            m_safe = jnp.where(m_new == -jnp.inf, 0.0, m_new)
            alpha = jnp.exp2(m - m_safe)
            p = jnp.exp2(s - m_safe)
            l_ref[g:g + 1, hl] = alpha * l_ref[g:g + 1, hl] + jnp.sum(_fold64(p, jnp.add), axis=0, keepdims=True)
            pv = _nn(vT_ref[jc, g * HEAD_DIM:(g + 1) * HEAD_DIM, :], p.astype(BF16))
            acc_ref[g, :, hl] = alpha * acc_ref[g, :, hl] + pv
            m_ref[g:g + 1, hl] = m_new

    masked_logits(0, 0)

    def kv_step(jj, c):
        a = 2 * jj
        masked_logits(a + 1, 1)
        softmax_pv(a, 0)
        masked_logits(a + 2, 0)
        softmax_pv(a + 1, 1)
        return c

    lax.fori_loop(0, (i + 2) // 2, kv_step, 0)
    for h in range(Q_HEADS):
        g, r = divmod(h, rep)
        oh = (acc_ref[g, :, r * QB:(r + 1) * QB] / l_ref[g:g + 1, r * QB:(r + 1) * QB]).T
        zh = z_ref[:, h * HEAD_DIM:(h + 1) * HEAD_DIM]
        o_ref[:, h * HEAD_DIM:(h + 1) * HEAD_DIM] = (oh * _silu(zh)).astype(o_ref.dtype)


def _attention(qT, qiT, wT, znat, zcol, k, vT, kidx, bsz, seq):
    nq = seq // QB
    topk = min(TOPK_MAX, seq // 4)
    m = bsz * seq
    step = lambda r: pl.BlockSpec((r, QB), lambda b, i: (0, b * nq + i))
    return pl.pallas_call(
        functools.partial(_attn_kernel, topk=topk),
        grid=(bsz, nq),
        in_specs=[step(1024), step(1024), step(IDX_HEADS),
                  pl.BlockSpec((QB, ATTN_WIDTH), lambda b, i: (b * nq + i, zcol)),
                  pl.BlockSpec((seq, 256), lambda b, i: (b, 0)),
                  pl.BlockSpec((seq // KTA, 256, KTA), lambda b, i: (b, 0, 0)),
                  pl.BlockSpec((seq, IDX_DIM), lambda b, i: (b, 0))],
        out_specs=pl.BlockSpec((QB, ATTN_WIDTH), lambda b, i: (b * nq + i, 0)),
        out_shape=jax.ShapeDtypeStruct((m, ATTN_WIDTH), BF16),
        scratch_shapes=[pltpu.VMEM((seq, QB), F32),
                        pltpu.VMEM((KV_HEADS, Q_HEADS // KV_HEADS * QB), F32),
                        pltpu.VMEM((KV_HEADS, Q_HEADS // KV_HEADS * QB), F32),
                        pltpu.VMEM((KV_HEADS, HEAD_DIM, Q_HEADS // KV_HEADS * QB), F32),
                        pltpu.VMEM((2, KTA, Q_HEADS * QB), F32)],
        compiler_params=pltpu.CompilerParams(dimension_semantics=("arbitrary", "arbitrary"),
                                             vmem_limit_bytes=VMEM_LIMIT),
        name="dsa_attention",
    )(qT, qiT, wT, znat, k, vT, kidx)


def _ssd_kernel(xbc_ref, z_ref, dt_ref, cw_ref, cb_ref, dtb_ref, a_ref, dsk_ref, nw_ref, e2_ref, tril_ref, o_ref,
                buf_ref, tail_ref, state_ref, xc_ref, xdt_ref, acum_ref, ahi_ref, alo_ref, *, tt):
    t = pl.program_id(1)

    @pl.when(t == 0)
    def _():
        tail_ref[...] = jnp.zeros_like(tail_ref)
        state_ref[...] = jnp.zeros_like(state_ref)

    buf_ref[0:8, :] = tail_ref[...]
    buf_ref[8:8 + tt, :] = xbc_ref[...]
    tail_ref[...] = xbc_ref[tt - 8:tt, :]

    dtv = dt_ref[...] + dtb_ref[...]
    dtv = jnp.maximum(dtv, 0.0) + jnp.log(1.0 + jnp.exp(-jnp.abs(dtv)))
    dhi = dtv.astype(BF16).astype(F32)
    lane = lax.broadcasted_iota(I32, (1, 128), 1)
    dsplit = jnp.where(lane < SSD_HEADS, dhi, dtv - dhi).astype(BF16)
    acum_ref[...] = _nn(dsplit, e2_ref[...])
    neg_a = -jnp.exp(a_ref[...])

    for c in range(tt // CHUNK):
        r0 = c * CHUNK
        conv = cb_ref[...] + cw_ref[3:4, :] * buf_ref[8 + r0:8 + r0 + CHUNK, :]
        for jtap in range(SSD_CONV - 1):
            conv += cw_ref[jtap:jtap + 1, :] * buf_ref[5 + jtap + r0:5 + jtap + r0 + CHUNK, :]
        xc = _silu(conv)
        xc_ref[r0:r0 + CHUNK, :] = xc
        dt_e = acum_ref[r0:r0 + CHUNK, :]
        xdt_ref[r0:r0 + CHUNK, :] = xc[:, :SSD_WIDTH] * dt_e
        dta_e = dt_e * neg_a
        ahi = dta_e.astype(BF16)
        ahi_ref[r0:r0 + CHUNK, :] = ahi
        alo_ref[r0:r0 + CHUNK, :] = (dta_e - ahi.astype(F32)).astype(BF16)
    acum_ref[...] = _nn(tril_ref[...], ahi_ref[...]) + _nn(tril_ref[...], alo_ref[...])

    gw = SSD_WIDTH // SSD_GROUPS
    s_row = lax.broadcasted_iota(I32, (CHUNK, gw), 0)
    s_lane = lax.broadcasted_iota(I32, (CHUNK, gw), 1) & (CHUNK - 1)
    diag = s_row == s_lane
    tril = s_row >= s_lane
    bd_r = lax.broadcasted_iota(I32, (2 * SSD_P, 2 * SSD_P), 0) >> 6
    bd_c = lax.broadcasted_iota(I32, (2 * SSD_P, 2 * SSD_P), 1) >> 6
    blockdiag = bd_r == bd_c

    def chunk_step(c, carry):
        rows = pl.ds(pl.multiple_of(c * CHUNK, CHUNK), CHUNK)
        for g in range(SSD_GROUPS):
            gl = slice(g * gw, (g + 1) * gw)
            bg = xc_ref[rows, SSD_WIDTH + g * SSD_N:SSD_WIDTH + (g + 1) * SSD_N]
            cg = xc_ref[rows, SSD_WIDTH + (SSD_GROUPS + g) * SSD_N:SSD_WIDTH + (SSD_GROUPS + g + 1) * SSD_N]
            bg16, cg16 = bg.astype(BF16), cg.astype(BF16)
            a_g = acum_ref[rows, gl]
            a_last = a_g[CHUNK - 1:CHUNK, :]
            xdt_g = xdt_ref[rows, gl]
            cb = _nt(cg16, jnp.concatenate([bg16] * (gw // CHUNK), axis=0))
            a_row = jnp.sum(jnp.where(diag, a_g, 0.0), axis=0, keepdims=True)
            lmat = jnp.exp(jnp.where(tril, a_g - a_row, -jnp.inf))
            mmat = (cb * lmat).astype(BF16)
            ys = []
            for q in range(gw // (2 * SSD_P)):
                ql = slice(q * 2 * SSD_P, (q + 1) * 2 * SSD_P)
                x2 = xdt_g[:, ql]
                wq = jnp.where(blockdiag, jnp.concatenate([x2, x2], axis=0), 0.0).astype(BF16)
                ys.append(_nn(mmat[:, ql], wq))
            y = jnp.concatenate(ys, axis=1)
            prev = state_ref[g]
            y += _nn(cg16, prev.astype(BF16)) * jnp.exp(a_g)
            y += dsk_ref[:, gl] * xc_ref[rows, gl]
            xdd = (xdt_g * jnp.exp(a_last - a_g)).astype(BF16)
            state_ref[g] = prev * jnp.exp(a_last) + _nn(bg.T.astype(BF16), xdd)
            gz = y * _silu(z_ref[rows, gl])
            gz = gz * lax.rsqrt(jnp.mean(gz * gz, axis=1, keepdims=True) + RMS_EPS)
            o_ref[rows, gl] = (gz * nw_ref[:, gl]).astype(o_ref.dtype)
        return carry

    lax.fori_loop(0, tt // CHUNK, chunk_step, 0)


def _ssd(nat, conv_w, conv_b, dt_bias, a_log, d_skip, norm_w, bsz, seq, tt):
    m = bsz * seq
    nt = seq // tt
    row = lambda b, t: b * nt + t
    full = lambda a: pl.BlockSpec(a.shape, lambda b, t: (0, 0))
    expand = lambda v: jnp.repeat(v.astype(F32), SSD_P).reshape(1, SSD_WIDTH)
    dtb2 = _pad_lanes(jnp.concatenate([dt_bias, dt_bias]))
    j_i = lax.broadcasted_iota(I32, (128, SSD_WIDTH), 0)
    h_i = lax.broadcasted_iota(I32, (128, SSD_WIDTH), 1) // SSD_P
    e2 = ((j_i == h_i) | (j_i == h_i + SSD_HEADS)).astype(BF16)
    r_i = lax.broadcasted_iota(I32, (tt, tt), 0)
    c_i = lax.broadcasted_iota(I32, (tt, tt), 1)
    tril_bd = ((r_i >= c_i) & (r_i // CHUNK == c_i // CHUNK)).astype(BF16)
    return pl.pallas_call(
        functools.partial(_ssd_kernel, tt=tt),
        grid=(bsz, nt),
        in_specs=[pl.BlockSpec((tt, XBC_WIDTH), lambda b, t: (row(b, t), 0)),
                  pl.BlockSpec((tt, SSD_WIDTH), lambda b, t: (row(b, t), XBC_WIDTH // SSD_WIDTH)),
                  pl.BlockSpec((tt, 128), lambda b, t: (row(b, t), (XBC_WIDTH + SSD_WIDTH + ATTN_WIDTH) // 128)),
                  full(conv_w), full(conv_b), full(dtb2), pl.BlockSpec((1, SSD_WIDTH), lambda b, t: (0, 0)),
                  pl.BlockSpec((1, SSD_WIDTH), lambda b, t: (0, 0)), full(norm_w), full(e2), full(tril_bd)],
        out_specs=pl.BlockSpec((tt, SSD_WIDTH), lambda b, t: (row(b, t), 0)),
        out_shape=jax.ShapeDtypeStruct((m, SSD_WIDTH), BF16),
        scratch_shapes=[pltpu.VMEM((tt + 8, XBC_WIDTH), F32), pltpu.VMEM((8, XBC_WIDTH), F32),
                        pltpu.VMEM((SSD_GROUPS, SSD_N, SSD_WIDTH // SSD_GROUPS), F32),
                        pltpu.VMEM((tt, XBC_WIDTH), F32), pltpu.VMEM((tt, SSD_WIDTH), F32),
                        pltpu.VMEM((tt, SSD_WIDTH), F32),
                        pltpu.VMEM((tt, SSD_WIDTH), BF16), pltpu.VMEM((tt, SSD_WIDTH), BF16)],
        compiler_params=pltpu.CompilerParams(dimension_semantics=("arbitrary", "arbitrary"),
                                             vmem_limit_bytes=VMEM_LIMIT),
        name="ssd_mixer",
    )(nat, nat, nat, conv_w, conv_b, dtb2, expand(a_log), expand(d_skip), norm_w, e2, tril_bd)


def _out_kernel(oa_ref, os_ref, x_ref, wa_ref, ws_ref, g_ref, b_ref, o_ref, *, alpha):
    sub = _nn(oa_ref[...], wa_ref[...]) + _nn(os_ref[...], ws_ref[...])
    y = alpha * x_ref[...] + sub
    mu = jnp.mean(y, axis=1, keepdims=True)
    yc = y - mu
    var = jnp.mean(yc * yc, axis=1, keepdims=True)
    o_ref[...] = yc * lax.rsqrt(var + LN_EPS) * g_ref[...] + b_ref[...]


def _out_proj(oa, os_, x2, wa, ws, g, b, alpha, tm):
    m = x2.shape[0]
    full = lambda a: pl.BlockSpec(a.shape, lambda i: (0, 0))
    rowblk = lambda w: pl.BlockSpec((tm, w), lambda i: (i, 0))
    return pl.pallas_call(
        functools.partial(_out_kernel, alpha=alpha),
        grid=(m // tm,),
        in_specs=[rowblk(ATTN_WIDTH), rowblk(SSD_WIDTH), rowblk(D_MODEL), full(wa), full(ws), full(g), full(b)],
        out_specs=rowblk(D_MODEL),
        out_shape=jax.ShapeDtypeStruct((m, D_MODEL), F32),
        compiler_params=pltpu.CompilerParams(dimension_semantics=("arbitrary",), vmem_limit_bytes=VMEM_LIMIT),
        name="out_proj_ln",
    )(oa, os_, x2, wa, ws, g, b)


def _rope_tables(seq, rot):
    half = rot // 2
    inv = ROPE_THETA ** (-jnp.arange(half, dtype=F32) * 2.0 / rot)
    ang = inv[:, None] * jnp.arange(seq, dtype=F32)[None, :]
    return jnp.cos(ang), jnp.sin(ang)


def _pad_lanes(v, width=128):
    v = v.reshape(1, -1).astype(F32)
    return jnp.pad(v, ((0, 0), (0, width - v.shape[1])))


def _layer(h, w_in, w_out, conv_w, conv_b, dt_bias, a_log, d_skip, norm_w, ln_g, ln_b, alpha):
    bsz, seq, _ = h.shape
    m = bsz * seq
    assert seq % QB == 0 and seq % 512 == 0
    x2 = h.reshape(m, D_MODEL)

    o_q, o_k, o_v, o_za, o_qi, o_ki, o_wi, o_zs, o_xbc, o_dt = (
        0, 1024, 1280, 1536, 2560, 3584, 3648, 3664, 4688, 6736)
    wt = w_in.T.astype(BF16)
    wq = wt[o_q:o_k]
    wkv = wt[o_k:o_za]
    wqi = wt[o_qi:o_ki]
    wkw = jnp.pad(wt[o_ki:o_zs], ((0, 128 - (o_zs - o_ki)), (0, 0)))
    wn = jnp.concatenate([w_in[:, o_xbc:o_dt], w_in[:, o_zs:o_xbc], w_in[:, o_za:o_qi],
                          jnp.pad(jnp.concatenate([w_in[:, o_dt:], w_in[:, o_dt:]], axis=1),
                                  ((0, 0), (0, 128 - 2 * SSD_HEADS)))],
                         axis=1).astype(BF16)

    ca, sa = _rope_tables(seq, HEAD_DIM // ROPE_DIV)
    ci, si = _rope_tables(seq, IDX_DIM // ROPE_DIV)

    qT, k, vT, qiT, kidx, wT, xb = _proj_t(x2, wq, wkv, wqi, wkw, ca, sa, ci, si, seq, tm=512)
    nat = _proj_n(xb, wn, tm=512, tn=1408)

    o_attn = _attention(qT, qiT, wT, nat, (XBC_WIDTH + SSD_WIDTH) // ATTN_WIDTH, k, vT, kidx, bsz, seq)
    o_ssd = _ssd(nat, conv_w, conv_b.reshape(1, -1), dt_bias, a_log, d_skip, norm_w.reshape(1, -1),
                 bsz, seq, tt=256)

    wo = w_out.astype(BF16)
    out = _out_proj(o_attn, o_ssd, x2, wo[:ATTN_WIDTH], wo[ATTN_WIDTH:], ln_g.reshape(1, -1),
                    ln_b.reshape(1, -1), alpha, tm=512)
    return out.reshape(bsz, seq, D_MODEL)


def kernel(x, w_in, w_out, conv_w, conv_b, dt_bias, a_log, d_skip, ssd_norm_w, ln_g, ln_b):
    depth = w_in.shape[0]
    alpha = (2.0 * depth) ** 0.25
    h = x
    for layer in range(depth):
        h = _layer(h, w_in[layer], w_out[layer], conv_w[layer], conv_b[layer], dt_bias[layer],
                   a_log[layer], d_skip[layer], ssd_norm_w[layer], ln_g[layer], ln_b[layer], alpha)
    return h
```

```python
import functools
import math

import jax
import jax.numpy as jnp
from jax import lax
from jax.experimental import pallas as pl
from jax.experimental.pallas import tpu as pltpu

F32 = jnp.float32
BF16 = jnp.bfloat16
I32 = jnp.int32

D_MODEL = 2048
CHUNK = 64
ATTN_WIDTH = 1024
SSD_WIDTH = 1024
HEAD_DIM = 128
Q_HEADS = 8
KV_HEADS = 2
ROPE_THETA = 500000.0
ROPE_DIV = 4
IDX_HEADS = 16
IDX_DIM = 64
TOPK_MAX = 256
SSD_P = 64
SSD_HEADS = 16
SSD_GROUPS = 4
SSD_N = 128
SSD_CONV = 4
XBC_WIDTH = SSD_WIDTH + 2 * SSD_GROUPS * SSD_N
LN_EPS = 1e-5
RMS_EPS = 1e-5

VMEM_LIMIT = 56 * 1024 * 1024
INT_MIN = -(2 ** 31)

QB = 256
KT = 128
KTA = 256


def _nt(a, b):
    return lax.dot_general(a, b, (((1,), (1,)), ((), ())), preferred_element_type=F32)


def _tn(a, b):
    return lax.dot_general(a, b, (((0,), (0,)), ((), ())), preferred_element_type=F32)


def _nn(a, b):
    return jnp.dot(a, b, preferred_element_type=F32)


def _silu(v):
    return v * (1.0 / (1.0 + jnp.exp(-v)))


def _rope_rows(t, cos, sin, nheads, hd, half):
    pieces = []
    for h in range(nheads):
        b = h * hd
        x1 = t[b:b + half]
        x2 = t[b + half:b + 2 * half]
        pieces += [x1 * cos - x2 * sin, x2 * cos + x1 * sin, t[b + 2 * half:b + hd]]
    return jnp.concatenate(pieces, axis=0)


def _proj_t_kernel(x_ref, wq_ref, wkv_ref, wqi_ref, wkw_ref, ca_ref, sa_ref, ci_ref, si_ref,
                   q_ref, k_ref, v_ref, qi_ref, kidx_ref, w_ref, xb_ref):
    x = x_ref[...].astype(BF16)
    xb_ref[...] = x
    ca, sa, ci, si = ca_ref[...], sa_ref[...], ci_ref[...], si_ref[...]
    qscale = HEAD_DIM ** -0.5 * math.log2(math.e)
    tq = _nt(wq_ref[...], x)
    q_ref[...] = (_rope_rows(tq, ca, sa, Q_HEADS, HEAD_DIM, 16) * qscale).astype(BF16)
    tkv = _nt(wkv_ref[...], x)
    k_ref[...] = _rope_rows(tkv[:KV_HEADS * HEAD_DIM], ca, sa, KV_HEADS, HEAD_DIM, 16).T.astype(BF16)
    for jj in range(v_ref.shape[0]):
        v_ref[jj] = tkv[KV_HEADS * HEAD_DIM:, jj * KTA:(jj + 1) * KTA].astype(BF16)
    tqi = _nt(wqi_ref[...], x)
    qi_ref[...] = _rope_rows(tqi, ci, si, IDX_HEADS, IDX_DIM, 8).astype(BF16)
    tkw = _nt(wkw_ref[...], x)
    kidx_ref[...] = _rope_rows(tkw[:IDX_DIM], ci, si, 1, IDX_DIM, 8).T.astype(BF16)
    w_ref[...] = tkw[IDX_DIM:IDX_DIM + IDX_HEADS] * (IDX_HEADS ** -0.5 * IDX_DIM ** -0.5)


def _proj_t(x2, wq, wkv, wqi, wkw, ca, sa, ci, si, seq, tm):
    m, kdim = x2.shape
    nper = seq // tm
    full = lambda a: pl.BlockSpec(a.shape, lambda i: (0, 0))
    tab = lambda a: pl.BlockSpec((a.shape[0], tm), lambda i: (0, i % nper))
    cols = lambda r: pl.BlockSpec((r, tm), lambda i: (0, i))
    rows = lambda c: pl.BlockSpec((tm, c), lambda i: (i, 0))
    vspec = pl.BlockSpec((tm // KTA, 256, KTA), lambda i: (i, 0, 0))
    sds = jax.ShapeDtypeStruct
    return pl.pallas_call(
        _proj_t_kernel,
        grid=(m // tm,),
        in_specs=[pl.BlockSpec((tm, kdim), lambda i: (i, 0)), full(wq), full(wkv), full(wqi), full(wkw),
                  tab(ca), tab(sa), tab(ci), tab(si)],
        out_specs=[cols(1024), rows(256), vspec, cols(1024), rows(IDX_DIM), cols(IDX_HEADS), rows(kdim)],
        out_shape=[sds((1024, m), BF16), sds((m, 256), BF16), sds((m // KTA, 256, KTA), BF16),
                   sds((1024, m), BF16), sds((m, IDX_DIM), BF16), sds((IDX_HEADS, m), F32), sds((m, kdim), BF16)],
        compiler_params=pltpu.CompilerParams(dimension_semantics=("arbitrary",), vmem_limit_bytes=VMEM_LIMIT),
        name="proj_t",
    )(x2, wq, wkv, wqi, wkw, ca, sa, ci, si)


def _proj_n_kernel(x_ref, w_ref, o_ref):
    o_ref[...] = _nn(x_ref[...], w_ref[...]).astype(o_ref.dtype)


def _proj_n(xb, w, tm, tn):
    m, kdim = xb.shape
    n = w.shape[1]
    return pl.pallas_call(
        _proj_n_kernel,
        grid=(n // tn, m // tm),
        in_specs=[pl.BlockSpec((tm, kdim), lambda j, i: (i, 0)), pl.BlockSpec((kdim, tn), lambda j, i: (0, j))],
        out_specs=pl.BlockSpec((tm, tn), lambda j, i: (i, j)),
        out_shape=jax.ShapeDtypeStruct((m, n), F32),
        compiler_params=pltpu.CompilerParams(dimension_semantics=("arbitrary", "arbitrary"),
                                             vmem_limit_bytes=VMEM_LIMIT),
        name="proj_n",
    )(xb, w)


def _fold64(x, op2):
    parts = [x[a * 64:(a + 1) * 64] for a in range(x.shape[0] // 64)]
    while len(parts) > 1:
        parts = [op2(parts[2 * a], parts[2 * a + 1]) for a in range(len(parts) // 2)]
    return parts[0]


def _fold8_tree(x):
    parts = [x[a * 8:(a + 1) * 8] for a in range(x.shape[0] // 8)]
    while len(parts) > 1:
        parts = [parts[2 * a] + parts[2 * a + 1] for a in range(len(parts) // 2)]
    return parts[0]


def _bit_transpose32(words):
    a = list(words)
    mask, j = 0x0000FFFF, 16
    while j:
        k = 0
        while k < 32:
            t = (a[k] ^ (a[k + j] >> j)) & mask
            a[k] = a[k] ^ t
            a[k + j] = a[k + j] ^ (t << j)
            k = (k + j + 1) & ~j
        j >>= 1
        mask = (mask ^ (mask << j)) & 0xFFFFFFFF
    return a


def _attn_kernel(qT_ref, qiT_ref, wT_ref, z_ref, k_ref, vT_ref, kidx_ref, o_ref,
                 key_ref, plane_ref, bias_ref, m_ref, l_ref, acc_ref, s_ref, *, topk):
    i = pl.program_id(1)
    nfull = 2 * i
    q_chunk = (i * QB + lax.broadcasted_iota(I32, (1, QB), 1)) // CHUNK

    def score_tile(j):
        rows = pl.ds(pl.multiple_of(j * KT, KT), KT)
        kt = kidx_ref[rows, :]
        acc = jnp.zeros((KT, QB), F32)
        for h in range(IDX_HEADS):
            qh = qiT_ref[h * IDX_DIM:(h + 1) * IDX_DIM, :]
            acc += jnp.maximum(_nn(kt, qh), 0.0) * wT_ref[h:h + 1, :]
        bits = pltpu.bitcast(acc, I32)
        return rows, bits ^ ((bits >> 31) & 0x7FFFFFFF)

    def store_planes(t256, keys):
        words = []
        for kk in keys:
            u = kk ^ INT_MIN
            words += [u[8 * a:8 * a + 8] for a in range(KT // 8)]
        planes = _bit_transpose32(words[::-1])
        prow = pl.ds(pl.multiple_of(t256 * 8, 8), 8)
        for b in range(32):
            plane_ref[b, prow, :] = planes[b]

    def full_tiles(j2, c):
        keys = []
        for jj in range(2):
            rows, key = score_tile(2 * j2 + jj)
            key_ref[rows, :] = key
            keys.append(key)
        store_planes(j2, keys)
        return c

    lax.fori_loop(0, i, full_tiles, 0)
    keys = []
    for jj in range(QB // KT):
        j = nfull + jj
        rows, key = score_tile(j)
        k_chunk = (j * KT + lax.broadcasted_iota(I32, (KT, 1), 0)) // CHUNK
        key = jnp.where(k_chunk <= q_chunk, key, INT_MIN)
        key_ref[rows, :] = key
        keys.append(key)
    store_planes(i, keys)

    nprow = plane_ref.shape[1]
    t_row = lax.broadcasted_iota(I32, (nprow, QB), 0) >> 3
    qc_local = lax.broadcasted_iota(I32, (1, QB), 1) // CHUNK
    diag_bits = lax.shift_right_logical(jnp.full((1, QB), -1, I32), 32 - 8 * (qc_local + 1))
    act = jnp.where(t_row < i, -1, jnp.where(t_row == i, diag_bits, 0))
    above = jnp.zeros((1, QB), I32)
    thr_u = jnp.zeros((1, QB), I32)
    for b in range(32):
        ones = act & plane_ref[b]
        c8 = _fold8_tree(lax.population_count(ones))
        c = above + jnp.sum(c8, axis=0, keepdims=True)
        take = c >= topk
        act = jnp.where(take, ones, act ^ ones)
        above = jnp.where(take, above, c)
        thr_u = thr_u | jnp.where(take, jnp.int32(-(2 ** 31) if b == 0 else 1 << (31 - b)), 0)
    thr = jnp.maximum(thr_u ^ INT_MIN, INT_MIN + 1)

    def to_bias(j, c):
        rows = pl.ds(pl.multiple_of(j * KTA, KTA), KTA)
        bias_ref[rows, :] = jnp.where(key_ref[rows, :] >= thr, 0.0, -jnp.inf)
        return c

    lax.fori_loop(0, i + 1, to_bias, 0)

    m_ref[...] = jnp.full(m_ref.shape, -jnp.inf, F32)
    l_ref[...] = jnp.zeros(l_ref.shape, F32)
    acc_ref[...] = jnp.zeros(acc_ref.shape, F32)

    rep = Q_HEADS // KV_HEADS
    gq = rep * QB
    ntile = bias_ref.shape[0] // KTA
    jpad = jnp.minimum(i + 1, ntile - 1)

    @pl.when(i + 1 < ntile)
    def _():
        bias_ref[pl.ds(pl.multiple_of((i + 1) * KTA, KTA), KTA), :] = jnp.full((KTA, QB), -jnp.inf, F32)

    def masked_logits(j, slot):
        jc = jnp.minimum(j, jpad)
        rows = pl.ds(pl.multiple_of(jc * KTA, KTA), KTA)
        for g in range(KV_HEADS):
            qg = jnp.concatenate([qT_ref[(g * rep + r) * HEAD_DIM:(g * rep + r + 1) * HEAD_DIM, :]
                                  for r in range(rep)], axis=1)
            sg = _nn(k_ref[rows, g * HEAD_DIM:(g + 1) * HEAD_DIM], qg)
            for r in range(rep):
                h = g * rep + r
                s_ref[slot, :, h * QB:(h + 1) * QB] = sg[:, r * QB:(r + 1) * QB] + bias_ref[rows, :]

    def softmax_pv(j, slot):
        jc = jnp.minimum(j, jpad)
        for h in range(Q_HEADS):
            g, r = divmod(h, rep)
            hl = slice(r * QB, (r + 1) * QB)
            s = s_ref[slot, :, h * QB:(h + 1) * QB]
            m = m_ref[g:g + 1, hl]
            m_new = jnp.maximum(m, jnp.max(_fold64(s, jnp.maximum), axis=0, keepdims=True))
            m_safe = jnp.where(m_new == -jnp.inf, 0.0, m_new)
            alpha = jnp.exp2(m - m_safe)
            p = jnp.exp2(s - m_safe)
            l_ref[g:g + 1, hl] = alpha * l_ref[g:g + 1, hl] + jnp.sum(_fold64(p, jnp.add), axis=0, keepdims=True)
            pv = _nn(vT_ref[jc, g * HEAD_DIM:(g + 1) * HEAD_DIM, :], p.astype(BF16))
            acc_ref[g, :, hl] = alpha * acc_ref[g, :, hl] + pv
            m_ref[g:g + 1, hl] = m_new

    masked_logits(0, 0)

    def kv_step(jj, c):
        a = 2 * jj
        masked_logits(a + 1, 1)
        softmax_pv(a, 0)
        masked_logits(a + 2, 0)
        softmax_pv(a + 1, 1)
        return c

    lax.fori_loop(0, (i + 2) // 2, kv_step, 0)
    for h in range(Q_HEADS):
        g, r = divmod(h, rep)
        oh = (acc_ref[g, :, r * QB:(r + 1) * QB] / l_ref[g:g + 1, r * QB:(r + 1) * QB]).T
        zh = z_ref[:, h * HEAD_DIM:(h + 1) * HEAD_DIM]
        o_ref[:, h * HEAD_DIM:(h + 1) * HEAD_DIM] = (oh * _silu(zh)).astype(o_ref.dtype)


def _attention(qT, qiT, wT, znat, zcol, k, vT, kidx, bsz, seq):
    nq = seq // QB
    topk = min(TOPK_MAX, seq // 4)
    m = bsz * seq
    step = lambda r: pl.BlockSpec((r, QB), lambda b, i: (0, b * nq + i))
    return pl.pallas_call(
        functools.partial(_attn_kernel, topk=topk),
        grid=(bsz, nq),
        in_specs=[step(1024), step(1024), step(IDX_HEADS),
                  pl.BlockSpec((QB, ATTN_WIDTH), lambda b, i: (b * nq + i, zcol)),
                  pl.BlockSpec((seq, 256), lambda b, i: (b, 0)),
                  pl.BlockSpec((seq // KTA, 256, KTA), lambda b, i: (b, 0, 0)),
                  pl.BlockSpec((seq, IDX_DIM), lambda b, i: (b, 0))],
        out_specs=pl.BlockSpec((QB, ATTN_WIDTH), lambda b, i: (b * nq + i, 0)),
        out_shape=jax.ShapeDtypeStruct((m, ATTN_WIDTH), BF16),
        scratch_shapes=[pltpu.VMEM((seq, QB), I32), pltpu.VMEM((32, seq // 32, QB), I32),
                        pltpu.VMEM((seq, QB), F32),
                        pltpu.VMEM((KV_HEADS, Q_HEADS // KV_HEADS * QB), F32),
                        pltpu.VMEM((KV_HEADS, Q_HEADS // KV_HEADS * QB), F32),
                        pltpu.VMEM((KV_HEADS, HEAD_DIM, Q_HEADS // KV_HEADS * QB), F32),
                        pltpu.VMEM((2, KTA, Q_HEADS * QB), F32)],
        compiler_params=pltpu.CompilerParams(dimension_semantics=("arbitrary", "arbitrary"),
                                             vmem_limit_bytes=VMEM_LIMIT),
        name="dsa_attention",
    )(qT, qiT, wT, znat, k, vT, kidx)


def _ssd_kernel(xbc_ref, z_ref, dt_ref, cw_ref, cb_ref, dtb_ref, a_ref, dsk_ref, nw_ref, e2_ref, tril_ref, o_ref,
                buf_ref, tail_ref, state_ref, xc_ref, xdt_ref, acum_ref, ahi_ref, alo_ref, *, tt):
    t = pl.program_id(1)

    @pl.when(t == 0)
    def _():
        tail_ref[...] = jnp.zeros_like(tail_ref)
        state_ref[...] = jnp.zeros_like(state_ref)

    buf_ref[0:8, :] = tail_ref[...]
    buf_ref[8:8 + tt, :] = xbc_ref[...]
    tail_ref[...] = xbc_ref[tt - 8:tt, :]

    dtv = dt_ref[...] + dtb_ref[...]
    dtv = jnp.maximum(dtv, 0.0) + jnp.log(1.0 + jnp.exp(-jnp.abs(dtv)))
    dhi = dtv.astype(BF16).astype(F32)
    lane = lax.broadcasted_iota(I32, (1, 128), 1)
    dsplit = jnp.where(lane < SSD_HEADS, dhi, dtv - dhi).astype(BF16)
    acum_ref[...] = _nn(dsplit, e2_ref[...])
    neg_a = -jnp.exp(a_ref[...])

    for c in range(tt // CHUNK):
        r0 = c * CHUNK
        conv = cb_ref[...] + cw_ref[3:4, :] * buf_ref[8 + r0:8 + r0 + CHUNK, :]
        for jtap in range(SSD_CONV - 1):
            conv += cw_ref[jtap:jtap + 1, :] * buf_ref[5 + jtap + r0:5 + jtap + r0 + CHUNK, :]
        xc = _silu(conv)
        xc_ref[r0:r0 + CHUNK, :] = xc
        dt_e = acum_ref[r0:r0 + CHUNK, :]
        xdt_ref[r0:r0 + CHUNK, :] = xc[:, :SSD_WIDTH] * dt_e
        dta_e = dt_e * neg_a
        ahi = dta_e.astype(BF16)
        ahi_ref[r0:r0 + CHUNK, :] = ahi
        alo_ref[r0:r0 + CHUNK, :] = (dta_e - ahi.astype(F32)).astype(BF16)
    acum_ref[...] = _nn(tril_ref[...], ahi_ref[...]) + _nn(tril_ref[...], alo_ref[...])

    gw = SSD_WIDTH // SSD_GROUPS
    s_row = lax.broadcasted_iota(I32, (CHUNK, gw), 0)
    s_lane = lax.broadcasted_iota(I32, (CHUNK, gw), 1) & (CHUNK - 1)
    diag = s_row == s_lane
    tril = s_row >= s_lane
    bd_r = lax.broadcasted_iota(I32, (2 * SSD_P, 2 * SSD_P), 0) >> 6
    bd_c = lax.broadcasted_iota(I32, (2 * SSD_P, 2 * SSD_P), 1) >> 6
    blockdiag = bd_r == bd_c

    def chunk_step(c, carry):
        rows = pl.ds(pl.multiple_of(c * CHUNK, CHUNK), CHUNK)
        for g in range(SSD_GROUPS):
            gl = slice(g * gw, (g + 1) * gw)
            bg = xc_ref[rows, SSD_WIDTH + g * SSD_N:SSD_WIDTH + (g + 1) * SSD_N]
            cg = xc_ref[rows, SSD_WIDTH + (SSD_GROUPS + g) * SSD_N:SSD_WIDTH + (SSD_GROUPS + g + 1) * SSD_N]
            bg16, cg16 = bg.astype(BF16), cg.astype(BF16)
            a_g = acum_ref[rows, gl]
            a_last = a_g[CHUNK - 1:CHUNK, :]
            xdt_g = xdt_ref[rows, gl]
            cb = _nt(cg16, jnp.concatenate([bg16] * (gw // CHUNK), axis=0))
            a_row = jnp.sum(jnp.where(diag, a_g, 0.0), axis=0, keepdims=True)
            lmat = jnp.exp(jnp.where(tril, a_g - a_row, -jnp.inf))
            mmat = (cb * lmat).astype(BF16)
            ys = []
            for q in range(gw // (2 * SSD_P)):
                ql = slice(q * 2 * SSD_P, (q + 1) * 2 * SSD_P)
                x2 = xdt_g[:, ql]
                wq = jnp.where(blockdiag, jnp.concatenate([x2, x2], axis=0), 0.0).astype(BF16)
                ys.append(_nn(mmat[:, ql], wq))
            y = jnp.concatenate(ys, axis=1)
            prev = state_ref[g]
            y += _nn(cg16, prev.astype(BF16)) * jnp.exp(a_g)
            y += dsk_ref[:, gl] * xc_ref[rows, gl]
            xdd = (xdt_g * jnp.exp(a_last - a_g)).astype(BF16)
            state_ref[g] = prev * jnp.exp(a_last) + _nn(bg.T.astype(BF16), xdd)
            gz = y * _silu(z_ref[rows, gl])
            gz = gz * lax.rsqrt(jnp.mean(gz * gz, axis=1, keepdims=True) + RMS_EPS)
            o_ref[rows, gl] = (gz * nw_ref[:, gl]).astype(o_ref.dtype)
        return carry

    lax.fori_loop(0, tt // CHUNK, chunk_step, 0)


def _ssd(nat, conv_w, conv_b, dt_bias, a_log, d_skip, norm_w, bsz, seq, tt):
    m = bsz * seq
    nt = seq // tt
    row = lambda b, t: b * nt + t
    full = lambda a: pl.BlockSpec(a.shape, lambda b, t: (0, 0))
    expand = lambda v: jnp.repeat(v.astype(F32), SSD_P).reshape(1, SSD_WIDTH)
    dtb2 = _pad_lanes(jnp.concatenate([dt_bias, dt_bias]))
    j_i = lax.broadcasted_iota(I32, (128, SSD_WIDTH), 0)
    h_i = lax.broadcasted_iota(I32, (128, SSD_WIDTH), 1) // SSD_P
    e2 = ((j_i == h_i) | (j_i == h_i + SSD_HEADS)).astype(BF16)
    r_i = lax.broadcasted_iota(I32, (tt, tt), 0)
    c_i = lax.broadcasted_iota(I32, (tt, tt), 1)
    tril_bd = ((r_i >= c_i) & (r_i // CHUNK == c_i // CHUNK)).astype(BF16)
    return pl.pallas_call(
        functools.partial(_ssd_kernel, tt=tt),
        grid=(bsz, nt),
        in_specs=[pl.BlockSpec((tt, XBC_WIDTH), lambda b, t: (row(b, t), 0)),
                  pl.BlockSpec((tt, SSD_WIDTH), lambda b, t: (row(b, t), XBC_WIDTH // SSD_WIDTH)),
                  pl.BlockSpec((tt, 128), lambda b, t: (row(b, t), (XBC_WIDTH + SSD_WIDTH + ATTN_WIDTH) // 128)),
                  full(conv_w), full(conv_b), full(dtb2), pl.BlockSpec((1, SSD_WIDTH), lambda b, t: (0, 0)),
                  pl.BlockSpec((1, SSD_WIDTH), lambda b, t: (0, 0)), full(norm_w), full(e2), full(tril_bd)],
        out_specs=pl.BlockSpec((tt, SSD_WIDTH), lambda b, t: (row(b, t), 0)),
        out_shape=jax.ShapeDtypeStruct((m, SSD_WIDTH), BF16),
        scratch_shapes=[pltpu.VMEM((tt + 8, XBC_WIDTH), F32), pltpu.VMEM((8, XBC_WIDTH), F32),
                        pltpu.VMEM((SSD_GROUPS, SSD_N, SSD_WIDTH // SSD_GROUPS), F32),
                        pltpu.VMEM((tt, XBC_WIDTH), F32), pltpu.VMEM((tt, SSD_WIDTH), F32),
                        pltpu.VMEM((tt, SSD_WIDTH), F32),
                        pltpu.VMEM((tt, SSD_WIDTH), BF16), pltpu.VMEM((tt, SSD_WIDTH), BF16)],
        compiler_params=pltpu.CompilerParams(dimension_semantics=("arbitrary", "arbitrary"),
                                             vmem_limit_bytes=VMEM_LIMIT),
        name="ssd_mixer",
    )(nat, nat, nat, conv_w, conv_b, dtb2, expand(a_log), expand(d_skip), norm_w, e2, tril_bd)


def _out_kernel(oa_ref, os_ref, x_ref, wa_ref, ws_ref, g_ref, b_ref, o_ref, *, alpha):
    sub = _nn(oa_ref[...], wa_ref[...]) + _nn(os_ref[...], ws_ref[...])
    y = alpha * x_ref[...] + sub
    mu = jnp.mean(y, axis=1, keepdims=True)
    yc = y - mu
    var = jnp.mean(yc * yc, axis=1, keepdims=True)
    o_ref[...] = yc * lax.rsqrt(var + LN_EPS) * g_ref[...] + b_ref[...]


def _out_proj(oa, os_, x2, wa, ws, g, b, alpha, tm):
    m = x2.shape[0]
    full = lambda a: pl.BlockSpec(a.shape, lambda i: (0, 0))
    rowblk = lambda w: pl.BlockSpec((tm, w), lambda i: (i, 0))
    return pl.pallas_call(
        functools.partial(_out_kernel, alpha=alpha),
        grid=(m // tm,),
        in_specs=[rowblk(ATTN_WIDTH), rowblk(SSD_WIDTH), rowblk(D_MODEL), full(wa), full(ws), full(g), full(b)],
        out_specs=rowblk(D_MODEL),
        out_shape=jax.ShapeDtypeStruct((m, D_MODEL), F32),
        compiler_params=pltpu.CompilerParams(dimension_semantics=("arbitrary",), vmem_limit_bytes=VMEM_LIMIT),
        name="out_proj_ln",
    )(oa, os_, x2, wa, ws, g, b)


def _rope_tables(seq, rot):
    half = rot // 2
    inv = ROPE_THETA ** (-jnp.arange(half, dtype=F32) * 2.0 / rot)
    ang = inv[:, None] * jnp.arange(seq, dtype=F32)[None, :]
    return jnp.cos(ang), jnp.sin(ang)


def _pad_lanes(v, width=128):
    v = v.reshape(1, -1).astype(F32)
    return jnp.pad(v, ((0, 0), (0, width - v.shape[1])))


def _layer(h, w_in, w_out, conv_w, conv_b, dt_bias, a_log, d_skip, norm_w, ln_g, ln_b, alpha):
    bsz, seq, _ = h.shape
    m = bsz * seq
    assert seq % QB == 0 and seq % 512 == 0
    x2 = h.reshape(m, D_MODEL)

    o_q, o_k, o_v, o_za, o_qi, o_ki, o_wi, o_zs, o_xbc, o_dt = (
        0, 1024, 1280, 1536, 2560, 3584, 3648, 3664, 4688, 6736)
    wt = w_in.T.astype(BF16)
    wq = wt[o_q:o_k]
    wkv = wt[o_k:o_za]
    wqi = wt[o_qi:o_ki]
    wkw = jnp.pad(wt[o_ki:o_zs], ((0, 128 - (o_zs - o_ki)), (0, 0)))
    wn = jnp.concatenate([w_in[:, o_xbc:o_dt], w_in[:, o_zs:o_xbc], w_in[:, o_za:o_qi],
                          jnp.pad(jnp.concatenate([w_in[:, o_dt:], w_in[:, o_dt:]], axis=1),
                                  ((0, 0), (0, 128 - 2 * SSD_HEADS)))],
                         axis=1).astype(BF16)

    ca, sa = _rope_tables(seq, HEAD_DIM // ROPE_DIV)
    ci, si = _rope_tables(seq, IDX_DIM // ROPE_DIV)

    qT, k, vT, qiT, kidx, wT, xb = _proj_t(x2, wq, wkv, wqi, wkw, ca, sa, ci, si, seq, tm=512)
    nat = _proj_n(xb, wn, tm=512, tn=1408)

    o_attn = _attention(qT, qiT, wT, nat, (XBC_WIDTH + SSD_WIDTH) // ATTN_WIDTH, k, vT, kidx, bsz, seq)
    o_ssd = _ssd(nat, conv_w, conv_b.reshape(1, -1), dt_bias, a_log, d_skip, norm_w.reshape(1, -1),
                 bsz, seq, tt=256)

    wo = w_out.astype(BF16)
    out = _out_proj(o_attn, o_ssd, x2, wo[:ATTN_WIDTH], wo[ATTN_WIDTH:], ln_g.reshape(1, -1),
                    ln_b.reshape(1, -1), alpha, tm=512)
    return out.reshape(bsz, seq, D_MODEL)


def kernel(x, w_in, w_out, conv_w, conv_b, dt_bias, a_log, d_skip, ssd_norm_w, ln_g, ln_b):
    depth = w_in.shape[0]
    alpha = (2.0 * depth) ** 0.25
    h = x
    for layer in range(depth):
        h = _layer(h, w_in[layer], w_out[layer], conv_w[layer], conv_b[layer], dt_bias[layer],
                   a_log[layer], d_skip[layer], ssd_norm_w[layer], ln_g[layer], ln_b[layer], alpha)
    return h
```

```python
import functools
import math

import jax
import jax.numpy as jnp
from jax import lax
from jax.experimental import pallas as pl
from jax.experimental.pallas import tpu as pltpu

F32 = jnp.float32
BF16 = jnp.bfloat16
I32 = jnp.int32

D_MODEL = 2048
CHUNK = 64
ATTN_WIDTH = 1024
SSD_WIDTH = 1024
HEAD_DIM = 128
Q_HEADS = 8
KV_HEADS = 2
ROPE_THETA = 500000.0
ROPE_DIV = 4
IDX_HEADS = 16
IDX_DIM = 64
TOPK_MAX = 256
SSD_P = 64
SSD_HEADS = 16
SSD_GROUPS = 4
SSD_N = 128
SSD_CONV = 4
XBC_WIDTH = SSD_WIDTH + 2 * SSD_GROUPS * SSD_N
LN_EPS = 1e-5
RMS_EPS = 1e-5

VMEM_LIMIT = 56 * 1024 * 1024
INT_MIN = -(2 ** 31)

QB = 256
KT = 128
KTA = 256


def _nt(a, b):
    return lax.dot_general(a, b, (((1,), (1,)), ((), ())), preferred_element_type=F32)


def _tn(a, b):
    return lax.dot_general(a, b, (((0,), (0,)), ((), ())), preferred_element_type=F32)


def _nn(a, b):
    return jnp.dot(a, b, preferred_element_type=F32)


def _silu(v):
    return v * (1.0 / (1.0 + jnp.exp(-v)))


def _rope_rows(t, cos, sin, nheads, hd, half):
    pieces = []
    for h in range(nheads):
        b = h * hd
        x1 = t[b:b + half]
        x2 = t[b + half:b + 2 * half]
        pieces += [x1 * cos - x2 * sin, x2 * cos + x1 * sin, t[b + 2 * half:b + hd]]
    return jnp.concatenate(pieces, axis=0)


def _proj_t_kernel(x_ref, wq_ref, wkv_ref, wqi_ref, wkw_ref, ca_ref, sa_ref, ci_ref, si_ref,
                   q_ref, k_ref, v_ref, qi_ref, kidx_ref, w_ref, xb_ref):
    x = x_ref[...].astype(BF16)
    xb_ref[...] = x
    ca, sa, ci, si = ca_ref[...], sa_ref[...], ci_ref[...], si_ref[...]
    qscale = HEAD_DIM ** -0.5 * math.log2(math.e)
    tq = _nt(wq_ref[...], x)
    q_ref[...] = (_rope_rows(tq, ca, sa, Q_HEADS, HEAD_DIM, 16) * qscale).astype(BF16)
    tkv = _nt(wkv_ref[...], x)
    k_ref[...] = _rope_rows(tkv[:KV_HEADS * HEAD_DIM], ca, sa, KV_HEADS, HEAD_DIM, 16).T.astype(BF16)
    for jj in range(v_ref.shape[0]):
        v_ref[jj] = tkv[KV_HEADS * HEAD_DIM:, jj * KTA:(jj + 1) * KTA].astype(BF16)
    tqi = _nt(wqi_ref[...], x)
    qi_ref[...] = _rope_rows(tqi, ci, si, IDX_HEADS, IDX_DIM, 8).astype(BF16)
    tkw = _nt(wkw_ref[...], x)
    kidx_ref[...] = _rope_rows(tkw[:IDX_DIM], ci, si, 1, IDX_DIM, 8).T.astype(BF16)
    w_ref[...] = tkw[IDX_DIM:IDX_DIM + IDX_HEADS] * (IDX_HEADS ** -0.5 * IDX_DIM ** -0.5)


def _proj_t(x2, wq, wkv, wqi, wkw, ca, sa, ci, si, seq, tm):
    m, kdim = x2.shape
    nper = seq // tm
    full = lambda a: pl.BlockSpec(a.shape, lambda i: (0, 0))
    tab = lambda a: pl.BlockSpec((a.shape[0], tm), lambda i: (0, i % nper))
    cols = lambda r: pl.BlockSpec((r, tm), lambda i: (0, i))
    rows = lambda c: pl.BlockSpec((tm, c), lambda i: (i, 0))
    vspec = pl.BlockSpec((tm // KTA, 256, KTA), lambda i: (i, 0, 0))
    sds = jax.ShapeDtypeStruct
    return pl.pallas_call(
        _proj_t_kernel,
        grid=(m // tm,),
        in_specs=[pl.BlockSpec((tm, kdim), lambda i: (i, 0)), full(wq), full(wkv), full(wqi), full(wkw),
                  tab(ca), tab(sa), tab(ci), tab(si)],
        out_specs=[cols(1024), rows(256), vspec, cols(1024), rows(IDX_DIM), cols(IDX_HEADS), rows(kdim)],
        out_shape=[sds((1024, m), BF16), sds((m, 256), BF16), sds((m // KTA, 256, KTA), BF16),
                   sds((1024, m), BF16), sds((m, IDX_DIM), BF16), sds((IDX_HEADS, m), F32), sds((m, kdim), BF16)],
        compiler_params=pltpu.CompilerParams(dimension_semantics=("arbitrary",), vmem_limit_bytes=VMEM_LIMIT),
        name="proj_t",
    )(x2, wq, wkv, wqi, wkw, ca, sa, ci, si)


def _proj_n_kernel(x_ref, w_ref, o_ref):
    o_ref[...] = _nn(x_ref[...], w_ref[...]).astype(o_ref.dtype)


def _proj_n(xb, w, tm, tn):
    m, kdim = xb.shape
    n = w.shape[1]
    return pl.pallas_call(
        _proj_n_kernel,
        grid=(n // tn, m // tm),
        in_specs=[pl.BlockSpec((tm, kdim), lambda j, i: (i, 0)), pl.BlockSpec((kdim, tn), lambda j, i: (0, j))],
        out_specs=pl.BlockSpec((tm, tn), lambda j, i: (i, j)),
        out_shape=jax.ShapeDtypeStruct((m, n), F32),
        compiler_params=pltpu.CompilerParams(dimension_semantics=("arbitrary", "arbitrary"),
                                             vmem_limit_bytes=VMEM_LIMIT),
        name="proj_n",
    )(xb, w)


def _fold64(x, op2):
    parts = [x[a * 64:(a + 1) * 64] for a in range(x.shape[0] // 64)]
    while len(parts) > 1:
        parts = [op2(parts[2 * a], parts[2 * a + 1]) for a in range(len(parts) // 2)]
    return parts[0]


def _fold8_tree(x):
    parts = [x[a * 8:(a + 1) * 8] for a in range(x.shape[0] // 8)]
    while len(parts) > 1:
        parts = [parts[2 * a] + parts[2 * a + 1] for a in range(len(parts) // 2)]
    return parts[0]


def _bit_transpose32(words):
    a = list(words)
    mask, j = 0x0000FFFF, 16
    while j:
        k = 0
        while k < 32:
            t = (a[k] ^ (a[k + j] >> j)) & mask
            a[k] = a[k] ^ t
            a[k + j] = a[k + j] ^ (t << j)
            k = (k + j + 1) & ~j
        j >>= 1
        mask = (mask ^ (mask << j)) & 0xFFFFFFFF
    return a


def _attn_kernel(qT_ref, qiT_ref, wT_ref, z_ref, k_ref, vT_ref, kidx_ref, o_ref,
                 key_ref, plane_ref, bias_ref, m_ref, l_ref, acc_ref, s_ref, mt_ref, *, topk):
    i = pl.program_id(1)
    nfull = 2 * i
    q_chunk = (i * QB + lax.broadcasted_iota(I32, (1, QB), 1)) // CHUNK

    def score_tile(j):
        rows = pl.ds(pl.multiple_of(j * KT, KT), KT)
        kt = kidx_ref[rows, :]
        acc = jnp.zeros((KT, QB), F32)
        for h in range(IDX_HEADS):
            qh = qiT_ref[h * IDX_DIM:(h + 1) * IDX_DIM, :]
            acc += jnp.maximum(_nn(kt, qh), 0.0) * wT_ref[h:h + 1, :]
        bits = pltpu.bitcast(acc, I32)
        return rows, bits ^ ((bits >> 31) & 0x7FFFFFFF)

    def store_planes(t256, keys):
        words = []
        for kk in keys:
            u = kk ^ INT_MIN
            words += [u[8 * a:8 * a + 8] for a in range(kk.shape[0] // 8)]
        planes = _bit_transpose32(words[::-1])
        prow = pl.ds(pl.multiple_of(t256 * 8, 8), 8)
        for b in range(32):
            plane_ref[b, prow, :] = planes[b]

    def planes_from_keys(t256):
        rows = pl.ds(pl.multiple_of(t256 * 2 * KT, 2 * KT), 2 * KT)
        store_planes(t256, [key_ref[rows, :]])

    @pl.when((pl.program_id(0) == 0) & (i == 0))
    def _():
        key_ref[...] = jnp.zeros(key_ref.shape, I32)

    def full_tiles(j2, c):
        planes_from_keys(jnp.maximum(j2 - 1, 0))
        for jj in range(2):
            rows, key = score_tile(2 * j2 + jj)
            key_ref[rows, :] = key
        return c

    lax.fori_loop(0, i, full_tiles, 0)
    planes_from_keys(jnp.maximum(i - 1, 0))
    keys = []
    for jj in range(QB // KT):
        j = nfull + jj
        rows, key = score_tile(j)
        k_chunk = (j * KT + lax.broadcasted_iota(I32, (KT, 1), 0)) // CHUNK
        key = jnp.where(k_chunk <= q_chunk, key, INT_MIN)
        key_ref[rows, :] = key
        keys.append(key)
    store_planes(i, keys)

    nprow = plane_ref.shape[1]
    t_row = lax.broadcasted_iota(I32, (nprow, QB), 0) >> 3
    qc_local = lax.broadcasted_iota(I32, (1, QB), 1) // CHUNK
    diag_bits = lax.shift_right_logical(jnp.full((1, QB), -1, I32), 32 - 8 * (qc_local + 1))
    act = jnp.where(t_row < i, -1, jnp.where(t_row == i, diag_bits, 0))
    above = jnp.zeros((1, QB), I32)
    thr_u = jnp.zeros((1, QB), I32)
    for b in range(32):
        ones = act & plane_ref[b]
        c8 = _fold8_tree(lax.population_count(ones))
        c = above + jnp.sum(c8, axis=0, keepdims=True)
        take = c >= topk
        act = jnp.where(take, ones, act ^ ones)
        above = jnp.where(take, above, c)
        thr_u = thr_u | jnp.where(take, jnp.int32(-(2 ** 31) if b == 0 else 1 << (31 - b)), 0)
    thr = jnp.maximum(thr_u ^ INT_MIN, INT_MIN + 1)

    def to_bias(j, c):
        rows = pl.ds(pl.multiple_of(j * KTA, KTA), KTA)
        bias_ref[rows, :] = jnp.where(key_ref[rows, :] >= thr, 0.0, -jnp.inf)
        return c

    lax.fori_loop(0, i + 1, to_bias, 0)

    m_ref[...] = jnp.full(m_ref.shape, -jnp.inf, F32)
    l_ref[...] = jnp.zeros(l_ref.shape, F32)
    acc_ref[...] = jnp.zeros(acc_ref.shape, F32)

    rep = Q_HEADS // KV_HEADS
    gq = rep * QB
    ntile = bias_ref.shape[0] // KTA
    jpad = jnp.minimum(i + 1, ntile - 1)

    @pl.when(i + 1 < ntile)
    def _():
        bias_ref[pl.ds(pl.multiple_of((i + 1) * KTA, KTA), KTA), :] = jnp.full((KTA, QB), -jnp.inf, F32)

    def masked_logits(j, slot):
        jc = jnp.minimum(j, jpad)
        rows = pl.ds(pl.multiple_of(jc * KTA, KTA), KTA)
        for g in range(KV_HEADS):
            qg = jnp.concatenate([qT_ref[(g * rep + r) * HEAD_DIM:(g * rep + r + 1) * HEAD_DIM, :]
                                  for r in range(rep)], axis=1)
            sg = _nn(k_ref[rows, g * HEAD_DIM:(g + 1) * HEAD_DIM], qg)
            for r in range(rep):
                h = g * rep + r
                sh = sg[:, r * QB:(r + 1) * QB] + bias_ref[rows, :]
                s_ref[slot, :, h * QB:(h + 1) * QB] = sh
                mt_ref[slot, h:h + 1, :] = jnp.max(_fold64(sh, jnp.maximum), axis=0, keepdims=True)

    def softmax_pv(j, slot):
        jc = jnp.minimum(j, jpad)
        for h in range(Q_HEADS):
            g, r = divmod(h, rep)
            hl = slice(r * QB, (r + 1) * QB)
            s = s_ref[slot, :, h * QB:(h + 1) * QB]
            m = m_ref[g:g + 1, hl]
            m_new = jnp.maximum(m, mt_ref[slot, h:h + 1, :])
            m_safe = jnp.where(m_new == -jnp.inf, 0.0, m_new)
            alpha = jnp.exp2(m - m_safe)
            p = jnp.exp2(s - m_safe)
            l_ref[g:g + 1, hl] = alpha * l_ref[g:g + 1, hl] + jnp.sum(_fold64(p, jnp.add), axis=0, keepdims=True)
            pv = _nn(vT_ref[jc, g * HEAD_DIM:(g + 1) * HEAD_DIM, :], p.astype(BF16))
            acc_ref[g, :, hl] = alpha * acc_ref[g, :, hl] + pv
            m_ref[g:g + 1, hl] = m_new

    masked_logits(0, 0)

    def kv_step(jj, c):
        a = 2 * jj
        masked_logits(a + 1, 1)
        softmax_pv(a, 0)
        masked_logits(a + 2, 0)
        softmax_pv(a + 1, 1)
        return c

    lax.fori_loop(0, (i + 2) // 2, kv_step, 0)
    for h in range(Q_HEADS):
        g, r = divmod(h, rep)
        oh = (acc_ref[g, :, r * QB:(r + 1) * QB] / l_ref[g:g + 1, r * QB:(r + 1) * QB]).T
        zh = z_ref[:, h * HEAD_DIM:(h + 1) * HEAD_DIM]
        o_ref[:, h * HEAD_DIM:(h + 1) * HEAD_DIM] = (oh * _silu(zh)).astype(o_ref.dtype)


def _attention(qT, qiT, wT, znat, zcol, k, vT, kidx, bsz, seq):
    nq = seq // QB
    topk = min(TOPK_MAX, seq // 4)
    m = bsz * seq
    step = lambda r: pl.BlockSpec((r, QB), lambda b, i: (0, b * nq + i))
    return pl.pallas_call(
        functools.partial(_attn_kernel, topk=topk),
        grid=(bsz, nq),
        in_specs=[step(1024), step(1024), step(IDX_HEADS),
                  pl.BlockSpec((QB, ATTN_WIDTH), lambda b, i: (b * nq + i, zcol)),
                  pl.BlockSpec((seq, 256), lambda b, i: (b, 0)),
                  pl.BlockSpec((seq // KTA, 256, KTA), lambda b, i: (b, 0, 0)),
                  pl.BlockSpec((seq, IDX_DIM), lambda b, i: (b, 0))],
        out_specs=pl.BlockSpec((QB, ATTN_WIDTH), lambda b, i: (b * nq + i, 0)),
        out_shape=jax.ShapeDtypeStruct((m, ATTN_WIDTH), BF16),
        scratch_shapes=[pltpu.VMEM((seq, QB), I32), pltpu.VMEM((32, seq // 32, QB), I32),
                        pltpu.VMEM((seq, QB), F32),
                        pltpu.VMEM((KV_HEADS, Q_HEADS // KV_HEADS * QB), F32),
                        pltpu.VMEM((KV_HEADS, Q_HEADS // KV_HEADS * QB), F32),
                        pltpu.VMEM((KV_HEADS, HEAD_DIM, Q_HEADS // KV_HEADS * QB), F32),
                        pltpu.VMEM((2, KTA, Q_HEADS * QB), F32), pltpu.VMEM((2, Q_HEADS, QB), F32)],
        compiler_params=pltpu.CompilerParams(dimension_semantics=("arbitrary", "arbitrary"),
                                             vmem_limit_bytes=VMEM_LIMIT),
        name="dsa_attention",
    )(qT, qiT, wT, znat, k, vT, kidx)


def _ssd_kernel(xbc_ref, z_ref, dt_ref, cw_ref, cb_ref, dtb_ref, a_ref, dsk_ref, nw_ref, e2_ref, tril_ref, o_ref,
                buf_ref, tail_ref, state_ref, xc_ref, xdt_ref, acum_ref, ahi_ref, alo_ref, *, tt):
    t = pl.program_id(1)

    @pl.when(t == 0)
    def _():
        tail_ref[...] = jnp.zeros_like(tail_ref)
        state_ref[...] = jnp.zeros_like(state_ref)

    buf_ref[0:8, :] = tail_ref[...]
    buf_ref[8:8 + tt, :] = xbc_ref[...]
    tail_ref[...] = xbc_ref[tt - 8:tt, :]

    dtv = dt_ref[...] + dtb_ref[...]
    dtv = jnp.maximum(dtv, 0.0) + jnp.log(1.0 + jnp.exp(-jnp.abs(dtv)))
    dhi = dtv.astype(BF16).astype(F32)
    lane = lax.broadcasted_iota(I32, (1, 128), 1)
    dsplit = jnp.where(lane < SSD_HEADS, dhi, dtv - dhi).astype(BF16)
    acum_ref[...] = _nn(dsplit, e2_ref[...])
    neg_a = -jnp.exp(a_ref[...])

    for c in range(tt // CHUNK):
        r0 = c * CHUNK
        conv = cb_ref[...] + cw_ref[3:4, :] * buf_ref[8 + r0:8 + r0 + CHUNK, :]
        for jtap in range(SSD_CONV - 1):
            conv += cw_ref[jtap:jtap + 1, :] * buf_ref[5 + jtap + r0:5 + jtap + r0 + CHUNK, :]
        xc = _silu(conv)
        xc_ref[r0:r0 + CHUNK, :] = xc
        dt_e = acum_ref[r0:r0 + CHUNK, :]
        xdt_ref[r0:r0 + CHUNK, :] = xc[:, :SSD_WIDTH] * dt_e
        dta_e = dt_e * neg_a
        ahi = dta_e.astype(BF16)
        ahi_ref[r0:r0 + CHUNK, :] = ahi
        alo_ref[r0:r0 + CHUNK, :] = (dta_e - ahi.astype(F32)).astype(BF16)
    acum_ref[...] = _nn(tril_ref[...], ahi_ref[...]) + _nn(tril_ref[...], alo_ref[...])

    gw = SSD_WIDTH // SSD_GROUPS
    s_row = lax.broadcasted_iota(I32, (CHUNK, gw), 0)
    s_lane = lax.broadcasted_iota(I32, (CHUNK, gw), 1) & (CHUNK - 1)
    diag = s_row == s_lane
    tril = s_row >= s_lane
    bd_r = lax.broadcasted_iota(I32, (2 * SSD_P, 2 * SSD_P), 0) >> 6
    bd_c = lax.broadcasted_iota(I32, (2 * SSD_P, 2 * SSD_P), 1) >> 6
    blockdiag = bd_r == bd_c

    for c in range(tt // CHUNK):
        rows = slice(c * CHUNK, (c + 1) * CHUNK)
        for g in range(SSD_GROUPS):
            gl = slice(g * gw, (g + 1) * gw)
            bg = xc_ref[rows, SSD_WIDTH + g * SSD_N:SSD_WIDTH + (g + 1) * SSD_N]
            cg = xc_ref[rows, SSD_WIDTH + (SSD_GROUPS + g) * SSD_N:SSD_WIDTH + (SSD_GROUPS + g + 1) * SSD_N]
            bg16, cg16 = bg.astype(BF16), cg.astype(BF16)
            a_g = acum_ref[rows, gl]
            a_last = a_g[CHUNK - 1:CHUNK, :]
            xdt_g = xdt_ref[rows, gl]
            cb = _nt(cg16, jnp.concatenate([bg16] * (gw // CHUNK), axis=0))
            a_row = jnp.sum(jnp.where(diag, a_g, 0.0), axis=0, keepdims=True)
            lmat = jnp.exp(jnp.where(tril, a_g - a_row, -jnp.inf))
            mmat = (cb * lmat).astype(BF16)
            ys = []
            for q in range(gw // (2 * SSD_P)):
                ql = slice(q * 2 * SSD_P, (q + 1) * 2 * SSD_P)
                x2 = xdt_g[:, ql]
                wq = jnp.where(blockdiag, jnp.concatenate([x2, x2], axis=0), 0.0).astype(BF16)
                ys.append(_nn(mmat[:, ql], wq))
            y = jnp.concatenate(ys, axis=1)
            prev = state_ref[g]
            y += _nn(cg16, prev.astype(BF16)) * jnp.exp(a_g)
            y += dsk_ref[:, gl] * xc_ref[rows, gl]
            xdd = (xdt_g * jnp.exp(a_last - a_g)).astype(BF16)
            state_ref[g] = prev * jnp.exp(a_last) + _nn(bg.T.astype(BF16), xdd)
            gz = y * _silu(z_ref[rows, gl])
            gz = gz * lax.rsqrt(jnp.mean(gz * gz, axis=1, keepdims=True) + RMS_EPS)
            o_ref[rows, gl] = (gz * nw_ref[:, gl]).astype(o_ref.dtype)


def _ssd(nat, conv_w, conv_b, dt_bias, a_log, d_skip, norm_w, bsz, seq, tt):
    m = bsz * seq
    nt = seq // tt
    row = lambda b, t: b * nt + t
    full = lambda a: pl.BlockSpec(a.shape, lambda b, t: (0, 0))
    expand = lambda v: jnp.repeat(v.astype(F32), SSD_P).reshape(1, SSD_WIDTH)
    dtb2 = _pad_lanes(jnp.concatenate([dt_bias, dt_bias]))
    j_i = lax.broadcasted_iota(I32, (128, SSD_WIDTH), 0)
    h_i = lax.broadcasted_iota(I32, (128, SSD_WIDTH), 1) // SSD_P
    e2 = ((j_i == h_i) | (j_i == h_i + SSD_HEADS)).astype(BF16)
    r_i = lax.broadcasted_iota(I32, (tt, tt), 0)
    c_i = lax.broadcasted_iota(I32, (tt, tt), 1)
    tril_bd = ((r_i >= c_i) & (r_i // CHUNK == c_i // CHUNK)).astype(BF16)
    return pl.pallas_call(
        functools.partial(_ssd_kernel, tt=tt),
        grid=(bsz, nt),
        in_specs=[pl.BlockSpec((tt, XBC_WIDTH), lambda b, t: (row(b, t), 0)),
                  pl.BlockSpec((tt, SSD_WIDTH), lambda b, t: (row(b, t), XBC_WIDTH // SSD_WIDTH)),
                  pl.BlockSpec((tt, 128), lambda b, t: (row(b, t), (XBC_WIDTH + SSD_WIDTH + ATTN_WIDTH) // 128)),
                  full(conv_w), full(conv_b), full(dtb2), pl.BlockSpec((1, SSD_WIDTH), lambda b, t: (0, 0)),
                  pl.BlockSpec((1, SSD_WIDTH), lambda b, t: (0, 0)), full(norm_w), full(e2), full(tril_bd)],
        out_specs=pl.BlockSpec((tt, SSD_WIDTH), lambda b, t: (row(b, t), 0)),
        out_shape=jax.ShapeDtypeStruct((m, SSD_WIDTH), BF16),
        scratch_shapes=[pltpu.VMEM((tt + 8, XBC_WIDTH), F32), pltpu.VMEM((8, XBC_WIDTH), F32),
                        pltpu.VMEM((SSD_GROUPS, SSD_N, SSD_WIDTH // SSD_GROUPS), F32),
                        pltpu.VMEM((tt, XBC_WIDTH), F32), pltpu.VMEM((tt, SSD_WIDTH), F32),
                        pltpu.VMEM((tt, SSD_WIDTH), F32),
                        pltpu.VMEM((tt, SSD_WIDTH), BF16), pltpu.VMEM((tt, SSD_WIDTH), BF16)],
        compiler_params=pltpu.CompilerParams(dimension_semantics=("arbitrary", "arbitrary"),
                                             vmem_limit_bytes=VMEM_LIMIT),
        name="ssd_mixer",
    )(nat, nat, nat, conv_w, conv_b, dtb2, expand(a_log), expand(d_skip), norm_w, e2, tril_bd)


def _out_kernel(oa_ref, os_ref, x_ref, wa_ref, ws_ref, g_ref, b_ref, o_ref, *, alpha):
    nsplit = 2
    rb = o_ref.shape[0] // nsplit
    for r in range(nsplit):
        rows = slice(r * rb, (r + 1) * rb)
        sub = _nn(oa_ref[rows, :], wa_ref[...]) + _nn(os_ref[rows, :], ws_ref[...])
        y = alpha * x_ref[rows, :] + sub
        mu = jnp.mean(y, axis=1, keepdims=True)
        yc = y - mu
        var = jnp.mean(yc * yc, axis=1, keepdims=True)
        o_ref[rows, :] = yc * lax.rsqrt(var + LN_EPS) * g_ref[...] + b_ref[...]


def _out_proj(oa, os_, x2, wa, ws, g, b, alpha, tm):
    m = x2.shape[0]
    full = lambda a: pl.BlockSpec(a.shape, lambda i: (0, 0))
    rowblk = lambda w: pl.BlockSpec((tm, w), lambda i: (i, 0))
    return pl.pallas_call(
        functools.partial(_out_kernel, alpha=alpha),
        grid=(m // tm,),
        in_specs=[rowblk(ATTN_WIDTH), rowblk(SSD_WIDTH), rowblk(D_MODEL), full(wa), full(ws), full(g), full(b)],
        out_specs=rowblk(D_MODEL),
        out_shape=jax.ShapeDtypeStruct((m, D_MODEL), F32),
        compiler_params=pltpu.CompilerParams(dimension_semantics=("arbitrary",), vmem_limit_bytes=VMEM_LIMIT),
        name="out_proj_ln",
    )(oa, os_, x2, wa, ws, g, b)


def _rope_tables(seq, rot):
    half = rot // 2
    inv = ROPE_THETA ** (-jnp.arange(half, dtype=F32) * 2.0 / rot)
    ang = inv[:, None] * jnp.arange(seq, dtype=F32)[None, :]
    return jnp.cos(ang), jnp.sin(ang)


def _pad_lanes(v, width=128):
    v = v.reshape(1, -1).astype(F32)
    return jnp.pad(v, ((0, 0), (0, width - v.shape[1])))


def _layer(h, w_in, w_out, conv_w, conv_b, dt_bias, a_log, d_skip, norm_w, ln_g, ln_b, alpha):
    bsz, seq, _ = h.shape
    m = bsz * seq
    assert seq % QB == 0 and seq % 512 == 0
    x2 = h.reshape(m, D_MODEL)

    o_q, o_k, o_v, o_za, o_qi, o_ki, o_wi, o_zs, o_xbc, o_dt = (
        0, 1024, 1280, 1536, 2560, 3584, 3648, 3664, 4688, 6736)
    wt = w_in.T.astype(BF16)
    wq = wt[o_q:o_k]
    wkv = wt[o_k:o_za]
    wqi = wt[o_qi:o_ki]
    wkw = jnp.pad(wt[o_ki:o_zs], ((0, 128 - (o_zs - o_ki)), (0, 0)))
    wn = jnp.concatenate([w_in[:, o_xbc:o_dt], w_in[:, o_zs:o_xbc], w_in[:, o_za:o_qi],
                          jnp.pad(jnp.concatenate([w_in[:, o_dt:], w_in[:, o_dt:]], axis=1),
                                  ((0, 0), (0, 128 - 2 * SSD_HEADS)))],
                         axis=1).astype(BF16)

    ca, sa = _rope_tables(seq, HEAD_DIM // ROPE_DIV)
    ci, si = _rope_tables(seq, IDX_DIM // ROPE_DIV)

    qT, k, vT, qiT, kidx, wT, xb = _proj_t(x2, wq, wkv, wqi, wkw, ca, sa, ci, si, seq, tm=512)
    nat = _proj_n(xb, wn, tm=1024, tn=1408)

    o_attn = _attention(qT, qiT, wT, nat, (XBC_WIDTH + SSD_WIDTH) // ATTN_WIDTH, k, vT, kidx, bsz, seq)
    o_ssd = _ssd(nat, conv_w, conv_b.reshape(1, -1), dt_bias, a_log, d_skip, norm_w.reshape(1, -1),
                 bsz, seq, tt=256)

    wo = w_out.astype(BF16)
    out = _out_proj(o_attn, o_ssd, x2, wo[:ATTN_WIDTH], wo[ATTN_WIDTH:], ln_g.reshape(1, -1),
                    ln_b.reshape(1, -1), alpha, tm=512)
    return out.reshape(bsz, seq, D_MODEL)


def kernel(x, w_in, w_out, conv_w, conv_b, dt_bias, a_log, d_skip, ssd_norm_w, ln_g, ln_b):
    depth = w_in.shape[0]
    alpha = (2.0 * depth) ** 0.25
    h = x
    for layer in range(depth):
        h = _layer(h, w_in[layer], w_out[layer], conv_w[layer], conv_b[layer], dt_bias[layer],
                   a_log[layer], d_skip[layer], ssd_norm_w[layer], ln_g[layer], ln_b[layer], alpha)
    return h
```

```python
import functools
import math

import jax
import jax.numpy as jnp
from jax import lax
from jax.experimental import pallas as pl
from jax.experimental.pallas import tpu as pltpu

F32 = jnp.float32
BF16 = jnp.bfloat16
I32 = jnp.int32

D_MODEL = 2048
CHUNK = 64
ATTN_WIDTH = 1024
SSD_WIDTH = 1024
HEAD_DIM = 128
Q_HEADS = 8
KV_HEADS = 2
ROPE_THETA = 500000.0
ROPE_DIV = 4
IDX_HEADS = 16
IDX_DIM = 64
TOPK_MAX = 256
SSD_P = 64
SSD_HEADS = 16
SSD_GROUPS = 4
SSD_N = 128
SSD_CONV = 4
XBC_WIDTH = SSD_WIDTH + 2 * SSD_GROUPS * SSD_N
LN_EPS = 1e-5
RMS_EPS = 1e-5

VMEM_LIMIT = 56 * 1024 * 1024
INT_MIN = -(2 ** 31)

QB = 256
KT = 128
KTA = 256


def _nt(a, b):
    return lax.dot_general(a, b, (((1,), (1,)), ((), ())), preferred_element_type=F32)


def _nn(a, b):
    return jnp.dot(a, b, preferred_element_type=F32)


def _silu(v):
    return v * (1.0 / (1.0 + jnp.exp(-v)))


def _rope_rows(t, cos, sin, nheads, hd, half):
    pieces = []
    for h in range(nheads):
        b = h * hd
        x1 = t[b:b + half]
        x2 = t[b + half:b + 2 * half]
        pieces += [x1 * cos - x2 * sin, x2 * cos + x1 * sin, t[b + 2 * half:b + hd]]
    return jnp.concatenate(pieces, axis=0)


def _proj_t_kernel(x_ref, wq_ref, wkv_ref, wqi_ref, wkw_ref, ca_ref, sa_ref, ci_ref, si_ref,
                   q_ref, k_ref, v_ref, qi_ref, kidx_ref, w_ref, xb_ref):
    x = x_ref[...].astype(BF16)
    xb_ref[...] = x
    ca, sa, ci, si = ca_ref[...], sa_ref[...], ci_ref[...], si_ref[...]
    qscale = HEAD_DIM ** -0.5 * math.log2(math.e)
    tq = _nt(wq_ref[...], x)
    q_ref[...] = (_rope_rows(tq, ca, sa, Q_HEADS, HEAD_DIM, 16) * qscale).astype(BF16)
    tkv = _nt(wkv_ref[...], x)
    k_ref[...] = _rope_rows(tkv[:KV_HEADS * HEAD_DIM], ca, sa, KV_HEADS, HEAD_DIM, 16).T.astype(BF16)
    for jj in range(v_ref.shape[0]):
        v_ref[jj] = tkv[KV_HEADS * HEAD_DIM:, jj * KTA:(jj + 1) * KTA].astype(BF16)
    tqi = _nt(wqi_ref[...], x)
    qi_ref[...] = _rope_rows(tqi, ci, si, IDX_HEADS, IDX_DIM, 8).astype(BF16)
    tkw = _nt(wkw_ref[...], x)
    kidx_ref[...] = _rope_rows(tkw[:IDX_DIM], ci, si, 1, IDX_DIM, 8).T.astype(BF16)
    w_ref[...] = tkw[IDX_DIM:IDX_DIM + IDX_HEADS] * (IDX_HEADS ** -0.5 * IDX_DIM ** -0.5)


def _proj_t(x2, wq, wkv, wqi, wkw, ca, sa, ci, si, seq, tm):
    m, kdim = x2.shape
    nper = seq // tm
    full = lambda a: pl.BlockSpec(a.shape, lambda i: (0, 0))
    tab = lambda a: pl.BlockSpec((a.shape[0], tm), lambda i: (0, i % nper))
    cols = lambda r: pl.BlockSpec((r, tm), lambda i: (0, i))
    rows = lambda c: pl.BlockSpec((tm, c), lambda i: (i, 0))
    vspec = pl.BlockSpec((tm // KTA, 256, KTA), lambda i: (i, 0, 0))
    sds = jax.ShapeDtypeStruct
    return pl.pallas_call(
        _proj_t_kernel,
        grid=(m // tm,),
        in_specs=[pl.BlockSpec((tm, kdim), lambda i: (i, 0)), full(wq), full(wkv), full(wqi), full(wkw),
                  tab(ca), tab(sa), tab(ci), tab(si)],
        out_specs=[cols(1024), rows(256), vspec, cols(1024), rows(IDX_DIM), cols(IDX_HEADS), rows(kdim)],
        out_shape=[sds((1024, m), BF16), sds((m, 256), BF16), sds((m // KTA, 256, KTA), BF16),
                   sds((1024, m), BF16), sds((m, IDX_DIM), BF16), sds((IDX_HEADS, m), F32), sds((m, kdim), BF16)],
        compiler_params=pltpu.CompilerParams(dimension_semantics=("arbitrary",), vmem_limit_bytes=VMEM_LIMIT),
        name="proj_t",
    )(x2, wq, wkv, wqi, wkw, ca, sa, ci, si)


def _proj_n_kernel(x_ref, w_ref, o_ref):
    o_ref[...] = _nn(x_ref[...], w_ref[...]).astype(o_ref.dtype)


def _proj_n(xb, w, tm, tn):
    m, kdim = xb.shape
    n = w.shape[1]
    return pl.pallas_call(
        _proj_n_kernel,
        grid=(n // tn, m // tm),
        in_specs=[pl.BlockSpec((tm, kdim), lambda j, i: (i, 0)), pl.BlockSpec((kdim, tn), lambda j, i: (0, j))],
        out_specs=pl.BlockSpec((tm, tn), lambda j, i: (i, j)),
        out_shape=jax.ShapeDtypeStruct((m, n), F32),
        compiler_params=pltpu.CompilerParams(dimension_semantics=("arbitrary", "arbitrary"),
                                             vmem_limit_bytes=VMEM_LIMIT),
        name="proj_n",
    )(xb, w)


def _fold64(x, op2):
    parts = [x[a * 64:(a + 1) * 64] for a in range(x.shape[0] // 64)]
    while len(parts) > 1:
        parts = [op2(parts[2 * a], parts[2 * a + 1]) for a in range(len(parts) // 2)]
    return parts[0]


def _fold8_tree(x):
    parts = [x[a * 8:(a + 1) * 8] for a in range(x.shape[0] // 8)]
    while len(parts) > 1:
        parts = [parts[2 * a] + parts[2 * a + 1] for a in range(len(parts) // 2)]
    return parts[0]


def _bit_transpose32(words):
    a = list(words)
    mask, j = 0x0000FFFF, 16
    while j:
        k = 0
        while k < 32:
            t = (a[k] ^ (a[k + j] >> j)) & mask
            a[k] = a[k] ^ t
            a[k + j] = a[k + j] ^ (t << j)
            k = (k + j + 1) & ~j
        j >>= 1
        mask = (mask ^ (mask << j)) & 0xFFFFFFFF
    return a


def _attn_kernel(qT_ref, qiT_ref, wT_ref, z_ref, k_ref, vT_ref, kidx_ref, o_ref,
                 key_ref, plane_ref, bias_ref, m_ref, l_ref, acc_ref, s_ref, mt_ref, *, topk):
    i = pl.program_id(1)
    nfull = 2 * i
    q_chunk = (i * QB + lax.broadcasted_iota(I32, (1, QB), 1)) // CHUNK

    def score_tile(j):
        rows = pl.ds(pl.multiple_of(j * KT, KT), KT)
        kt = kidx_ref[rows, :]
        acc = jnp.zeros((KT, QB), F32)
        for h in range(IDX_HEADS):
            qh = qiT_ref[h * IDX_DIM:(h + 1) * IDX_DIM, :]
            acc += jnp.maximum(_nn(kt, qh), 0.0) * wT_ref[h:h + 1, :]
        bits = pltpu.bitcast(acc, I32)
        return rows, bits ^ ((bits >> 31) & 0x7FFFFFFF)

    def store_planes(t256, keys):
        words = []
        for kk in keys:
            u = kk ^ INT_MIN
            words += [u[8 * a:8 * a + 8] for a in range(kk.shape[0] // 8)]
        planes = _bit_transpose32(words[::-1])
        prow = pl.ds(pl.multiple_of(t256 * 8, 8), 8)
        for b in range(32):
            plane_ref[b, prow, :] = planes[b]

    def planes_from_keys(t256):
        rows = pl.ds(pl.multiple_of(t256 * 2 * KT, 2 * KT), 2 * KT)
        store_planes(t256, [key_ref[rows, :]])

    @pl.when((pl.program_id(0) == 0) & (i == 0))
    def _():
        key_ref[...] = jnp.zeros(key_ref.shape, I32)

    def full_tiles(j2, c):
        planes_from_keys(jnp.maximum(j2 - 1, 0))
        for jj in range(2):
            rows, key = score_tile(2 * j2 + jj)
            key_ref[rows, :] = key
        return c

    lax.fori_loop(0, i, full_tiles, 0)
    planes_from_keys(jnp.maximum(i - 1, 0))
    keys = []
    for jj in range(QB // KT):
        j = nfull + jj
        rows, key = score_tile(j)
        k_chunk = (j * KT + lax.broadcasted_iota(I32, (KT, 1), 0)) // CHUNK
        key = jnp.where(k_chunk <= q_chunk, key, INT_MIN)
        key_ref[rows, :] = key
        keys.append(key)
    store_planes(i, keys)

    nprow = plane_ref.shape[1]
    t_row = lax.broadcasted_iota(I32, (nprow, QB), 0) >> 3
    qc_local = lax.broadcasted_iota(I32, (1, QB), 1) // CHUNK
    diag_bits = lax.shift_right_logical(jnp.full((1, QB), -1, I32), 32 - 8 * (qc_local + 1))
    act = jnp.where(t_row < i, -1, jnp.where(t_row == i, diag_bits, 0))
    above = jnp.zeros((1, QB), I32)
    thr_u = jnp.zeros((1, QB), I32)
    for b in range(32):
        ones = act & plane_ref[b]
        c8 = _fold8_tree(lax.population_count(ones))
        c = above + jnp.sum(c8, axis=0, keepdims=True)
        take = c >= topk
        act = jnp.where(take, ones, act ^ ones)
        above = jnp.where(take, above, c)
        thr_u = thr_u | jnp.where(take, jnp.int32(-(2 ** 31) if b == 0 else 1 << (31 - b)), 0)
    thr = jnp.maximum(thr_u ^ INT_MIN, INT_MIN + 1)
    need = topk - above
    n_tied = jnp.sum(_fold8_tree(lax.population_count(act)), axis=0, keepdims=True)
    any_tie = jnp.max(jnp.where(n_tied > need, 1, 0)) > 0

    @pl.when(jnp.logical_not(any_tie))
    def _():
        def to_bias(j, c):
            rows = pl.ds(pl.multiple_of(j * KTA, KTA), KTA)
            bias_ref[rows, :] = jnp.where(key_ref[rows, :] >= thr, 0.0, -jnp.inf)
            return c

        lax.fori_loop(0, i + 1, to_bias, 0)

    @pl.when(any_tie)
    def _():
        p_row = lax.broadcasted_iota(I32, (nprow, QB), 0)
        word_bit_clear = {7: 0x0000FFFF, 6: 0x00FF00FF, 5: 0x0F0F0F0F, 4: 0x33333333, 3: 0x55555555}
        cand, left, idx_thr = act, need, jnp.zeros((1, QB), I32)
        for b in range((bias_ref.shape[0] - 1).bit_length() - 1, -1, -1):
            if b >= 8:
                zmask = jnp.where(((p_row >> (3 + b - 8)) & 1) == 0, -1, 0)
            elif b >= 3:
                zmask = word_bit_clear[b]
            else:
                zmask = jnp.where(((p_row >> b) & 1) == 0, -1, 0)
            zeros = cand & zmask
            c0 = jnp.sum(_fold8_tree(lax.population_count(zeros)), axis=0, keepdims=True)
            low = c0 >= left
            cand = jnp.where(low, zeros, cand ^ zeros)
            left = jnp.where(low, left, left - c0)
            idx_thr = idx_thr | jnp.where(low, 0, 1 << b)

        def to_bias_tied(j, c):
            rows = pl.ds(pl.multiple_of(j * KTA, KTA), KTA)
            key = key_ref[rows, :]
            kidx_abs = j * KTA + lax.broadcasted_iota(I32, (KTA, 1), 0)
            keep = (key > thr) | ((key == thr) & (kidx_abs <= idx_thr))
            bias_ref[rows, :] = jnp.where(keep, 0.0, -jnp.inf)
            return c

        lax.fori_loop(0, i + 1, to_bias_tied, 0)

    m_ref[...] = jnp.full(m_ref.shape, -jnp.inf, F32)
    l_ref[...] = jnp.zeros(l_ref.shape, F32)
    acc_ref[...] = jnp.zeros(acc_ref.shape, F32)

    rep = Q_HEADS // KV_HEADS
    gq = rep * QB
    ntile = bias_ref.shape[0] // KTA
    jpad = jnp.minimum(i + 1, ntile - 1)

    @pl.when(i + 1 < ntile)
    def _():
        bias_ref[pl.ds(pl.multiple_of((i + 1) * KTA, KTA), KTA), :] = jnp.full((KTA, QB), -jnp.inf, F32)

    def masked_logits(j, slot):
        jc = jnp.minimum(j, jpad)
        rows = pl.ds(pl.multiple_of(jc * KTA, KTA), KTA)
        for g in range(KV_HEADS):
            qg = jnp.concatenate([qT_ref[(g * rep + r) * HEAD_DIM:(g * rep + r + 1) * HEAD_DIM, :]
                                  for r in range(rep)], axis=1)
            sg = _nn(k_ref[rows, g * HEAD_DIM:(g + 1) * HEAD_DIM], qg)
            for r in range(rep):
                h = g * rep + r
                sh = sg[:, r * QB:(r + 1) * QB] + bias_ref[rows, :]
                s_ref[slot, :, h * QB:(h + 1) * QB] = sh
                mt_ref[slot, h:h + 1, :] = jnp.max(_fold64(sh, jnp.maximum), axis=0, keepdims=True)

    def softmax_pv(j, slot):
        jc = jnp.minimum(j, jpad)
        for h in range(Q_HEADS):
            g, r = divmod(h, rep)
            hl = slice(r * QB, (r + 1) * QB)
            s = s_ref[slot, :, h * QB:(h + 1) * QB]
            m = m_ref[g:g + 1, hl]
            m_new = jnp.maximum(m, mt_ref[slot, h:h + 1, :])
            m_safe = jnp.where(m_new == -jnp.inf, 0.0, m_new)
            alpha = jnp.exp2(m - m_safe)
            p = jnp.exp2(s - m_safe)
            l_ref[g:g + 1, hl] = alpha * l_ref[g:g + 1, hl] + jnp.sum(_fold64(p, jnp.add), axis=0, keepdims=True)
            pv = _nn(vT_ref[jc, g * HEAD_DIM:(g + 1) * HEAD_DIM, :], p.astype(BF16))
            acc_ref[g, :, hl] = alpha * acc_ref[g, :, hl] + pv
            m_ref[g:g + 1, hl] = m_new

    masked_logits(0, 0)

    def kv_step(jj, c):
        a = 2 * jj
        masked_logits(a + 1, 1)
        softmax_pv(a, 0)
        masked_logits(a + 2, 0)
        softmax_pv(a + 1, 1)
        return c

    lax.fori_loop(0, (i + 2) // 2, kv_step, 0)
    for h in range(Q_HEADS):
        g, r = divmod(h, rep)
        oh = (acc_ref[g, :, r * QB:(r + 1) * QB] / l_ref[g:g + 1, r * QB:(r + 1) * QB]).T
        zh = z_ref[:, h * HEAD_DIM:(h + 1) * HEAD_DIM]
        o_ref[:, h * HEAD_DIM:(h + 1) * HEAD_DIM] = (oh * _silu(zh)).astype(o_ref.dtype)


def _attention(qT, qiT, wT, znat, zcol, k, vT, kidx, bsz, seq):
    nq = seq // QB
    topk = min(TOPK_MAX, seq // 4)
    m = bsz * seq
    step = lambda r: pl.BlockSpec((r, QB), lambda b, i: (0, b * nq + i))
    return pl.pallas_call(
        functools.partial(_attn_kernel, topk=topk),
        grid=(bsz, nq),
        in_specs=[step(1024), step(1024), step(IDX_HEADS),
                  pl.BlockSpec((QB, ATTN_WIDTH), lambda b, i: (b * nq + i, zcol)),
                  pl.BlockSpec((seq, 256), lambda b, i: (b, 0)),
                  pl.BlockSpec((seq // KTA, 256, KTA), lambda b, i: (b, 0, 0)),
                  pl.BlockSpec((seq, IDX_DIM), lambda b, i: (b, 0))],
        out_specs=pl.BlockSpec((QB, ATTN_WIDTH), lambda b, i: (b * nq + i, 0)),
        out_shape=jax.ShapeDtypeStruct((m, ATTN_WIDTH), BF16),
        scratch_shapes=[pltpu.VMEM((seq, QB), I32), pltpu.VMEM((32, seq // 32, QB), I32),
                        pltpu.VMEM((seq, QB), F32),
                        pltpu.VMEM((KV_HEADS, Q_HEADS // KV_HEADS * QB), F32),
                        pltpu.VMEM((KV_HEADS, Q_HEADS // KV_HEADS * QB), F32),
                        pltpu.VMEM((KV_HEADS, HEAD_DIM, Q_HEADS // KV_HEADS * QB), F32),
                        pltpu.VMEM((2, KTA, Q_HEADS * QB), F32), pltpu.VMEM((2, Q_HEADS, QB), F32)],
        compiler_params=pltpu.CompilerParams(dimension_semantics=("arbitrary", "arbitrary"),
                                             vmem_limit_bytes=VMEM_LIMIT),
        name="dsa_attention",
    )(qT, qiT, wT, znat, k, vT, kidx)


def _ssd_kernel(xbc_ref, z_ref, dt_ref, cw_ref, cb_ref, dtb_ref, a_ref, dsk_ref, nw_ref, e2_ref, tril_ref, o_ref,
                buf_ref, tail_ref, state_ref, xc_ref, xdt_ref, acum_ref, ahi_ref, alo_ref, *, tt):
    t = pl.program_id(1)

    @pl.when(t == 0)
    def _():
        tail_ref[...] = jnp.zeros_like(tail_ref)
        state_ref[...] = jnp.zeros_like(state_ref)

    buf_ref[0:8, :] = tail_ref[...]
    buf_ref[8:8 + tt, :] = xbc_ref[...]
    tail_ref[...] = xbc_ref[tt - 8:tt, :]

    dtv = dt_ref[...] + dtb_ref[...]
    dtv = jnp.maximum(dtv, 0.0) + jnp.log(1.0 + jnp.exp(-jnp.abs(dtv)))
    dhi = dtv.astype(BF16).astype(F32)
    lane = lax.broadcasted_iota(I32, (1, 128), 1)
    dsplit = jnp.where(lane < SSD_HEADS, dhi, dtv - dhi).astype(BF16)
    acum_ref[...] = _nn(dsplit, e2_ref[...])
    neg_a = -jnp.exp(a_ref[...])

    for c in range(tt // CHUNK):
        r0 = c * CHUNK
        conv = cb_ref[...] + cw_ref[3:4, :] * buf_ref[8 + r0:8 + r0 + CHUNK, :]
        for jtap in range(SSD_CONV - 1):
            conv += cw_ref[jtap:jtap + 1, :] * buf_ref[5 + jtap + r0:5 + jtap + r0 + CHUNK, :]
        xc = _silu(conv)
        xc_ref[r0:r0 + CHUNK, :] = xc
        dt_e = acum_ref[r0:r0 + CHUNK, :]
        xdt_ref[r0:r0 + CHUNK, :] = xc[:, :SSD_WIDTH] * dt_e
        dta_e = dt_e * neg_a
        ahi = dta_e.astype(BF16)
        ahi_ref[r0:r0 + CHUNK, :] = ahi
        alo_ref[r0:r0 + CHUNK, :] = (dta_e - ahi.astype(F32)).astype(BF16)
    acum_ref[...] = _nn(tril_ref[...], ahi_ref[...]) + _nn(tril_ref[...], alo_ref[...])

    gw = SSD_WIDTH // SSD_GROUPS
    s_row = lax.broadcasted_iota(I32, (CHUNK, gw), 0)
    s_lane = lax.broadcasted_iota(I32, (CHUNK, gw), 1) & (CHUNK - 1)
    diag = s_row == s_lane
    tril = s_row >= s_lane
    bd_r = lax.broadcasted_iota(I32, (2 * SSD_P, 2 * SSD_P), 0) >> 6
    bd_c = lax.broadcasted_iota(I32, (2 * SSD_P, 2 * SSD_P), 1) >> 6
    blockdiag = bd_r == bd_c

    for c in range(tt // CHUNK):
        rows = slice(c * CHUNK, (c + 1) * CHUNK)
        for g in range(SSD_GROUPS):
            gl = slice(g * gw, (g + 1) * gw)
            bg = xc_ref[rows, SSD_WIDTH + g * SSD_N:SSD_WIDTH + (g + 1) * SSD_N]
            cg = xc_ref[rows, SSD_WIDTH + (SSD_GROUPS + g) * SSD_N:SSD_WIDTH + (SSD_GROUPS + g + 1) * SSD_N]
            bg16, cg16 = bg.astype(BF16), cg.astype(BF16)
            a_g = acum_ref[rows, gl]
            a_last = a_g[CHUNK - 1:CHUNK, :]
            xdt_g = xdt_ref[rows, gl]
            cb = _nt(cg16, jnp.concatenate([bg16] * (gw // CHUNK), axis=0))
            a_row = jnp.sum(jnp.where(diag, a_g, 0.0), axis=0, keepdims=True)
            lmat = jnp.exp(jnp.where(tril, a_g - a_row, -jnp.inf))
            mmat = (cb * lmat).astype(BF16)
            ys = []
            for q in range(gw // (2 * SSD_P)):
                ql = slice(q * 2 * SSD_P, (q + 1) * 2 * SSD_P)
                x2 = xdt_g[:, ql]
                wq = jnp.where(blockdiag, jnp.concatenate([x2, x2], axis=0), 0.0).astype(BF16)
                ys.append(_nn(mmat[:, ql], wq))
            y = jnp.concatenate(ys, axis=1)
            prev = state_ref[g]
            y += _nn(cg16, prev.astype(BF16)) * jnp.exp(a_g)
            y += dsk_ref[:, gl] * xc_ref[rows, gl]
            xdd = (xdt_g * jnp.exp(a_last - a_g)).astype(BF16)
            state_ref[g] = prev * jnp.exp(a_last) + _nn(bg.T.astype(BF16), xdd)
            gz = y * _silu(z_ref[rows, gl])
            gz = gz * lax.rsqrt(jnp.mean(gz * gz, axis=1, keepdims=True) + RMS_EPS)
            o_ref[rows, gl] = (gz * nw_ref[:, gl]).astype(o_ref.dtype)


def _ssd(nat, conv_w, conv_b, dt_bias, a_log, d_skip, norm_w, bsz, seq, tt):
    m = bsz * seq
    nt = seq // tt
    row = lambda b, t: b * nt + t
    full = lambda a: pl.BlockSpec(a.shape, lambda b, t: (0, 0))
    expand = lambda v: jnp.repeat(v.astype(F32), SSD_P).reshape(1, SSD_WIDTH)
    dtb2 = _pad_lanes(jnp.concatenate([dt_bias, dt_bias]))
    j_i = lax.broadcasted_iota(I32, (128, SSD_WIDTH), 0)
    h_i = lax.broadcasted_iota(I32, (128, SSD_WIDTH), 1) // SSD_P
    e2 = ((j_i == h_i) | (j_i == h_i + SSD_HEADS)).astype(BF16)
    r_i = lax.broadcasted_iota(I32, (tt, tt), 0)
    c_i = lax.broadcasted_iota(I32, (tt, tt), 1)
    tril_bd = ((r_i >= c_i) & (r_i // CHUNK == c_i // CHUNK)).astype(BF16)
    return pl.pallas_call(
        functools.partial(_ssd_kernel, tt=tt),
        grid=(bsz, nt),
        in_specs=[pl.BlockSpec((tt, XBC_WIDTH), lambda b, t: (row(b, t), 0)),
                  pl.BlockSpec((tt, SSD_WIDTH), lambda b, t: (row(b, t), XBC_WIDTH // SSD_WIDTH)),
                  pl.BlockSpec((tt, 128), lambda b, t: (row(b, t), (XBC_WIDTH + SSD_WIDTH + ATTN_WIDTH) // 128)),
                  full(conv_w), full(conv_b), full(dtb2), pl.BlockSpec((1, SSD_WIDTH), lambda b, t: (0, 0)),
                  pl.BlockSpec((1, SSD_WIDTH), lambda b, t: (0, 0)), full(norm_w), full(e2), full(tril_bd)],
        out_specs=pl.BlockSpec((tt, SSD_WIDTH), lambda b, t: (row(b, t), 0)),
        out_shape=jax.ShapeDtypeStruct((m, SSD_WIDTH), BF16),
        scratch_shapes=[pltpu.VMEM((tt + 8, XBC_WIDTH), F32), pltpu.VMEM((8, XBC_WIDTH), F32),
                        pltpu.VMEM((SSD_GROUPS, SSD_N, SSD_WIDTH // SSD_GROUPS), F32),
                        pltpu.VMEM((tt, XBC_WIDTH), F32), pltpu.VMEM((tt, SSD_WIDTH), F32),
                        pltpu.VMEM((tt, SSD_WIDTH), F32),
                        pltpu.VMEM((tt, SSD_WIDTH), BF16), pltpu.VMEM((tt, SSD_WIDTH), BF16)],
        compiler_params=pltpu.CompilerParams(dimension_semantics=("arbitrary", "arbitrary"),
                                             vmem_limit_bytes=VMEM_LIMIT),
        name="ssd_mixer",
    )(nat, nat, nat, conv_w, conv_b, dtb2, expand(a_log), expand(d_skip), norm_w, e2, tril_bd)


def _out_kernel(oa_ref, os_ref, x_ref, wa_ref, ws_ref, g_ref, b_ref, o_ref, *, alpha):
    nsplit = 2
    rb = o_ref.shape[0] // nsplit
    for r in range(nsplit):
        rows = slice(r * rb, (r + 1) * rb)
        sub = _nn(oa_ref[rows, :], wa_ref[...]) + _nn(os_ref[rows, :], ws_ref[...])
        y = alpha * x_ref[rows, :] + sub
        mu = jnp.mean(y, axis=1, keepdims=True)
        yc = y - mu
        var = jnp.mean(yc * yc, axis=1, keepdims=True)
        o_ref[rows, :] = yc * lax.rsqrt(var + LN_EPS) * g_ref[...] + b_ref[...]


def _out_proj(oa, os_, x2, wa, ws, g, b, alpha, tm):
    m = x2.shape[0]
    full = lambda a: pl.BlockSpec(a.shape, lambda i: (0, 0))
    rowblk = lambda w: pl.BlockSpec((tm, w), lambda i: (i, 0))
    return pl.pallas_call(
        functools.partial(_out_kernel, alpha=alpha),
        grid=(m // tm,),
        in_specs=[rowblk(ATTN_WIDTH), rowblk(SSD_WIDTH), rowblk(D_MODEL), full(wa), full(ws), full(g), full(b)],
        out_specs=rowblk(D_MODEL),
        out_shape=jax.ShapeDtypeStruct((m, D_MODEL), F32),
        compiler_params=pltpu.CompilerParams(dimension_semantics=("arbitrary",), vmem_limit_bytes=VMEM_LIMIT),
        name="out_proj_ln",
    )(oa, os_, x2, wa, ws, g, b)


def _rope_tables(seq, rot):
    half = rot // 2
    inv = ROPE_THETA ** (-jnp.arange(half, dtype=F32) * 2.0 / rot)
    ang = inv[:, None] * jnp.arange(seq, dtype=F32)[None, :]
    return jnp.cos(ang), jnp.sin(ang)


def _pad_lanes(v, width=128):
    v = v.reshape(1, -1).astype(F32)
    return jnp.pad(v, ((0, 0), (0, width - v.shape[1])))


def _layer(h, w_in, w_out, conv_w, conv_b, dt_bias, a_log, d_skip, norm_w, ln_g, ln_b, alpha):
    bsz, seq, _ = h.shape
    m = bsz * seq
    assert seq % QB == 0 and seq % 512 == 0
    x2 = h.reshape(m, D_MODEL)

    o_q, o_k, o_v, o_za, o_qi, o_ki, o_wi, o_zs, o_xbc, o_dt = (
        0, 1024, 1280, 1536, 2560, 3584, 3648, 3664, 4688, 6736)
    wt = w_in.T.astype(BF16)
    wq = wt[o_q:o_k]
    wkv = wt[o_k:o_za]
    wqi = wt[o_qi:o_ki]
    wkw = jnp.pad(wt[o_ki:o_zs], ((0, 128 - (o_zs - o_ki)), (0, 0)))
    wn = jnp.concatenate([w_in[:, o_xbc:o_dt], w_in[:, o_zs:o_xbc], w_in[:, o_za:o_qi],
                          jnp.pad(jnp.concatenate([w_in[:, o_dt:], w_in[:, o_dt:]], axis=1),
                                  ((0, 0), (0, 128 - 2 * SSD_HEADS)))],
                         axis=1).astype(BF16)

    ca, sa = _rope_tables(seq, HEAD_DIM // ROPE_DIV)
    ci, si = _rope_tables(seq, IDX_DIM // ROPE_DIV)

    qT, k, vT, qiT, kidx, wT, xb = _proj_t(x2, wq, wkv, wqi, wkw, ca, sa, ci, si, seq, tm=512)
    nat = _proj_n(xb, wn, tm=min(1024, m), tn=1408)

    o_attn = _attention(qT, qiT, wT, nat, (XBC_WIDTH + SSD_WIDTH) // ATTN_WIDTH, k, vT, kidx, bsz, seq)
    o_ssd = _ssd(nat, conv_w, conv_b.reshape(1, -1), dt_bias, a_log, d_skip, norm_w.reshape(1, -1),
                 bsz, seq, tt=256)

    wo = w_out.astype(BF16)
    out = _out_proj(o_attn, o_ssd, x2, wo[:ATTN_WIDTH], wo[ATTN_WIDTH:], ln_g.reshape(1, -1),
                    ln_b.reshape(1, -1), alpha, tm=512)
    return out.reshape(bsz, seq, D_MODEL)


def kernel(x, w_in, w_out, conv_w, conv_b, dt_bias, a_log, d_skip, ssd_norm_w, ln_g, ln_b):
    depth = w_in.shape[0]
    alpha = (2.0 * depth) ** 0.25
    h = x
    for layer in range(depth):
        h = _layer(h, w_in[layer], w_out[layer], conv_w[layer], conv_b[layer], dt_bias[layer],
                   a_log[layer], d_skip[layer], ssd_norm_w[layer], ln_g[layer], ln_b[layer], alpha)
    return h
```

```python
import functools
import math

import jax
import jax.numpy as jnp
from jax import lax
from jax.experimental import pallas as pl
from jax.experimental.pallas import tpu as pltpu

F32 = jnp.float32
BF16 = jnp.bfloat16
I32 = jnp.int32

D_MODEL = 2048
CHUNK = 64
ATTN_WIDTH = 1024
SSD_WIDTH = 1024
HEAD_DIM = 128
Q_HEADS = 8
KV_HEADS = 2
ROPE_THETA = 500000.0
ROPE_DIV = 4
IDX_HEADS = 16
IDX_DIM = 64
TOPK_MAX = 256
SSD_P = 64
SSD_HEADS = 16
SSD_GROUPS = 4
SSD_N = 128
SSD_CONV = 4
XBC_WIDTH = SSD_WIDTH + 2 * SSD_GROUPS * SSD_N
LN_EPS = 1e-5
RMS_EPS = 1e-5

VMEM_LIMIT = 56 * 1024 * 1024
INT_MIN = -(2 ** 31)

QB = 256
KT = 128
KTA = 256


def _nt(a, b):
    return lax.dot_general(a, b, (((1,), (1,)), ((), ())), preferred_element_type=F32)


def _nn(a, b):
    return jnp.dot(a, b, preferred_element_type=F32)


def _silu(v):
    return v * (1.0 / (1.0 + jnp.exp(-v)))


def _rope_rows(t, cos, sin, nheads, hd, half):
    pieces = []
    for h in range(nheads):
        b = h * hd
        x1 = t[b:b + half]
        x2 = t[b + half:b + 2 * half]
        pieces += [x1 * cos - x2 * sin, x2 * cos + x1 * sin, t[b + 2 * half:b + hd]]
    return jnp.concatenate(pieces, axis=0)


def _proj_t_kernel(x_ref, wq_ref, wkv_ref, wqi_ref, wkw_ref, ca_ref, sa_ref, ci_ref, si_ref,
                   q_ref, k_ref, v_ref, qi_ref, kidx_ref, w_ref, xb_ref):
    x = x_ref[...].astype(BF16)
    xb_ref[...] = x
    ca, sa, ci, si = ca_ref[...], sa_ref[...], ci_ref[...], si_ref[...]
    qscale = HEAD_DIM ** -0.5 * math.log2(math.e)
    tq = _nt(wq_ref[...], x)
    q_ref[...] = (_rope_rows(tq, ca, sa, Q_HEADS, HEAD_DIM, 16) * qscale).astype(BF16)
    tkv = _nt(wkv_ref[...], x)
    k_ref[...] = _rope_rows(tkv[:KV_HEADS * HEAD_DIM], ca, sa, KV_HEADS, HEAD_DIM, 16).T.astype(BF16)
    for jj in range(v_ref.shape[0]):
        v_ref[jj] = tkv[KV_HEADS * HEAD_DIM:, jj * KTA:(jj + 1) * KTA].astype(BF16)
    tqi = _nt(wqi_ref[...], x)
    qi_ref[...] = _rope_rows(tqi, ci, si, IDX_HEADS, IDX_DIM, 8).astype(BF16)
    tkw = _nt(wkw_ref[...], x)
    kidx_ref[...] = _rope_rows(tkw[:IDX_DIM], ci, si, 1, IDX_DIM, 8).T.astype(BF16)
    w_ref[...] = tkw[IDX_DIM:IDX_DIM + IDX_HEADS] * (IDX_HEADS ** -0.5 * IDX_DIM ** -0.5)


def _proj_t(x2, wq, wkv, wqi, wkw, ca, sa, ci, si, seq, tm):
    m, kdim = x2.shape
    nper = seq // tm
    full = lambda a: pl.BlockSpec(a.shape, lambda i: (0, 0))
    tab = lambda a: pl.BlockSpec((a.shape[0], tm), lambda i: (0, i % nper))
    cols = lambda r: pl.BlockSpec((r, tm), lambda i: (0, i))
    rows = lambda c: pl.BlockSpec((tm, c), lambda i: (i, 0))
    vspec = pl.BlockSpec((tm // KTA, 256, KTA), lambda i: (i, 0, 0))
    sds = jax.ShapeDtypeStruct
    return pl.pallas_call(
        _proj_t_kernel,
        grid=(m // tm,),
        in_specs=[pl.BlockSpec((tm, kdim), lambda i: (i, 0)), full(wq), full(wkv), full(wqi), full(wkw),
                  tab(ca), tab(sa), tab(ci), tab(si)],
        out_specs=[cols(1024), rows(256), vspec, cols(1024), rows(IDX_DIM), cols(IDX_HEADS), rows(kdim)],
        out_shape=[sds((1024, m), BF16), sds((m, 256), BF16), sds((m // KTA, 256, KTA), BF16),
                   sds((1024, m), BF16), sds((m, IDX_DIM), BF16), sds((IDX_HEADS, m), F32), sds((m, kdim), BF16)],
        compiler_params=pltpu.CompilerParams(dimension_semantics=("arbitrary",), vmem_limit_bytes=VMEM_LIMIT),
        name="proj_t",
    )(x2, wq, wkv, wqi, wkw, ca, sa, ci, si)


def _proj_n_kernel(x_ref, w_ref, o_ref):
    o_ref[...] = _nn(x_ref[...], w_ref[...]).astype(o_ref.dtype)


def _proj_n(xb, w, tm, tn):
    m, kdim = xb.shape
    n = w.shape[1]
    return pl.pallas_call(
        _proj_n_kernel,
        grid=(n // tn, m // tm),
        in_specs=[pl.BlockSpec((tm, kdim), lambda j, i: (i, 0)), pl.BlockSpec((kdim, tn), lambda j, i: (0, j))],
        out_specs=pl.BlockSpec((tm, tn), lambda j, i: (i, j)),
        out_shape=jax.ShapeDtypeStruct((m, n), F32),
        compiler_params=pltpu.CompilerParams(dimension_semantics=("arbitrary", "arbitrary"),
                                             vmem_limit_bytes=VMEM_LIMIT),
        name="proj_n",
    )(xb, w)


def _fold64(x, op2):
    parts = [x[a * 64:(a + 1) * 64] for a in range(x.shape[0] // 64)]
    while len(parts) > 1:
        parts = [op2(parts[2 * a], parts[2 * a + 1]) for a in range(len(parts) // 2)]
    return parts[0]


def _fold8_tree(x):
    parts = [x[a * 8:(a + 1) * 8] for a in range(x.shape[0] // 8)]
    while len(parts) > 1:
        parts = [parts[2 * a] + parts[2 * a + 1] for a in range(len(parts) // 2)]
    return parts[0]


def _bit_transpose32(words):
    a = list(words)
    mask, j = 0x0000FFFF, 16
    while j:
        k = 0
        while k < 32:
            t = (a[k] ^ (a[k + j] >> j)) & mask
            a[k] = a[k] ^ t
            a[k + j] = a[k + j] ^ (t << j)
            k = (k + j + 1) & ~j
        j >>= 1
        mask = (mask ^ (mask << j)) & 0xFFFFFFFF
    return a


def _attn_kernel(qT_ref, qiT_ref, wT_ref, z_ref, k_ref, vT_ref, kidx_ref, o_ref,
                 key_ref, plane_ref, bias_ref, m_ref, l_ref, acc_ref, s_ref, mt_ref, *, topk):
    i = pl.program_id(1)
    nfull = 2 * i
    q_chunk = (i * QB + lax.broadcasted_iota(I32, (1, QB), 1)) // CHUNK

    def score_tile(j):
        rows = pl.ds(pl.multiple_of(j * KT, KT), KT)
        kt = kidx_ref[rows, :]
        acc = jnp.zeros((KT, QB), F32)
        for h in range(IDX_HEADS):
            qh = qiT_ref[h * IDX_DIM:(h + 1) * IDX_DIM, :]
            acc += jnp.maximum(_nn(kt, qh), 0.0) * wT_ref[h:h + 1, :]
        bits = pltpu.bitcast(acc, I32)
        return rows, bits ^ ((bits >> 31) & 0x7FFFFFFF)

    def store_planes(t256, keys):
        words = []
        for kk in keys:
            u = kk ^ INT_MIN
            words += [u[8 * a:8 * a + 8] for a in range(kk.shape[0] // 8)]
        planes = _bit_transpose32(words[::-1])
        prow = pl.ds(pl.multiple_of(t256 * 8, 8), 8)
        for b in range(32):
            plane_ref[b, prow, :] = planes[b]

    def planes_from_keys(t256):
        rows = pl.ds(pl.multiple_of(t256 * 2 * KT, 2 * KT), 2 * KT)
        store_planes(t256, [key_ref[rows, :]])

    @pl.when((pl.program_id(0) == 0) & (i == 0))
    def _():
        key_ref[...] = jnp.zeros(key_ref.shape, I32)

    def full_tiles(j2, c):
        planes_from_keys(jnp.maximum(j2 - 1, 0))
        for jj in range(2):
            rows, key = score_tile(2 * j2 + jj)
            key_ref[rows, :] = key
        return c

    lax.fori_loop(0, i, full_tiles, 0)
    planes_from_keys(jnp.maximum(i - 1, 0))
    keys = []
    for jj in range(QB // KT):
        j = nfull + jj
        rows, key = score_tile(j)
        k_chunk = (j * KT + lax.broadcasted_iota(I32, (KT, 1), 0)) // CHUNK
        key = jnp.where(k_chunk <= q_chunk, key, INT_MIN)
        key_ref[rows, :] = key
        keys.append(key)
    store_planes(i, keys)

    nprow = plane_ref.shape[1]
    t_row = lax.broadcasted_iota(I32, (nprow, QB), 0) >> 3
    qc_local = lax.broadcasted_iota(I32, (1, QB), 1) // CHUNK
    diag_bits = lax.shift_right_logical(jnp.full((1, QB), -1, I32), 32 - 8 * (qc_local + 1))
    act0 = jnp.where(t_row < i, -1, jnp.where(t_row == i, diag_bits, 0))

    def radix_select(nrow):
        act = act0[:nrow]
        above = jnp.zeros((1, QB), I32)
        thr_u = jnp.zeros((1, QB), I32)
        for b in range(32):
            ones = act & plane_ref[b, 0:nrow, :]
            c8 = _fold8_tree(lax.population_count(ones))
            c = above + jnp.sum(c8, axis=0, keepdims=True)
            take = c >= topk
            act = jnp.where(take, ones, act ^ ones)
            above = jnp.where(take, above, c)
            thr_u = thr_u | jnp.where(take, jnp.int32(-(2 ** 31) if b == 0 else 1 << (31 - b)), 0)
        if nrow < nprow:
            act = jnp.concatenate([act, jnp.zeros((nprow - nrow, QB), I32)], axis=0)
        return thr_u, above, act

    half = nprow // 2
    thr_u, above, act = lax.cond((i + 1) * 8 <= half, lambda: radix_select(half), lambda: radix_select(nprow))
    thr = jnp.maximum(thr_u ^ INT_MIN, INT_MIN + 1)
    need = topk - above
    n_tied = jnp.sum(_fold8_tree(lax.population_count(act)), axis=0, keepdims=True)
    any_tie = jnp.max(jnp.where(n_tied > need, 1, 0)) > 0

    @pl.when(jnp.logical_not(any_tie))
    def _():
        def to_bias(j, c):
            rows = pl.ds(pl.multiple_of(j * KTA, KTA), KTA)
            bias_ref[rows, :] = jnp.where(key_ref[rows, :] >= thr, 0.0, -jnp.inf)
            return c

        lax.fori_loop(0, i + 1, to_bias, 0)

    @pl.when(any_tie)
    def _():
        p_row = lax.broadcasted_iota(I32, (nprow, QB), 0)
        word_bit_clear = {7: 0x0000FFFF, 6: 0x00FF00FF, 5: 0x0F0F0F0F, 4: 0x33333333, 3: 0x55555555}
        cand, left, idx_thr = act, need, jnp.zeros((1, QB), I32)
        for b in range((bias_ref.shape[0] - 1).bit_length() - 1, -1, -1):
            if b >= 8:
                zmask = jnp.where(((p_row >> (3 + b - 8)) & 1) == 0, -1, 0)
            elif b >= 3:
                zmask = word_bit_clear[b]
            else:
                zmask = jnp.where(((p_row >> b) & 1) == 0, -1, 0)
            zeros = cand & zmask
            c0 = jnp.sum(_fold8_tree(lax.population_count(zeros)), axis=0, keepdims=True)
            low = c0 >= left
            cand = jnp.where(low, zeros, cand ^ zeros)
            left = jnp.where(low, left, left - c0)
            idx_thr = idx_thr | jnp.where(low, 0, 1 << b)

        def to_bias_tied(j, c):
            rows = pl.ds(pl.multiple_of(j * KTA, KTA), KTA)
            key = key_ref[rows, :]
            kidx_abs = j * KTA + lax.broadcasted_iota(I32, (KTA, 1), 0)
            keep = (key > thr) | ((key == thr) & (kidx_abs <= idx_thr))
            bias_ref[rows, :] = jnp.where(keep, 0.0, -jnp.inf)
            return c

        lax.fori_loop(0, i + 1, to_bias_tied, 0)

    m_ref[...] = jnp.full(m_ref.shape, -jnp.inf, F32)
    l_ref[...] = jnp.zeros(l_ref.shape, F32)
    acc_ref[...] = jnp.zeros(acc_ref.shape, F32)

    rep = Q_HEADS // KV_HEADS
    gq = rep * QB
    ntile = bias_ref.shape[0] // KTA
    jpad = jnp.minimum(i + 1, ntile - 1)

    @pl.when(i + 1 < ntile)
    def _():
        bias_ref[pl.ds(pl.multiple_of((i + 1) * KTA, KTA), KTA), :] = jnp.full((KTA, QB), -jnp.inf, F32)

    def masked_logits(j, slot):
        jc = jnp.minimum(j, jpad)
        rows = pl.ds(pl.multiple_of(jc * KTA, KTA), KTA)
        for g in range(KV_HEADS):
            qg = jnp.concatenate([qT_ref[(g * rep + r) * HEAD_DIM:(g * rep + r + 1) * HEAD_DIM, :]
                                  for r in range(rep)], axis=1)
            sg = _nn(k_ref[rows, g * HEAD_DIM:(g + 1) * HEAD_DIM], qg)
            for r in range(rep):
                h = g * rep + r
                sh = sg[:, r * QB:(r + 1) * QB] + bias_ref[rows, :]
                s_ref[slot, :, h * QB:(h + 1) * QB] = sh
                mt_ref[slot, h:h + 1, :] = jnp.max(_fold64(sh, jnp.maximum), axis=0, keepdims=True)

    def softmax_pv(j, slot):
        jc = jnp.minimum(j, jpad)
        for h in range(Q_HEADS):
            g, r = divmod(h, rep)
            hl = slice(r * QB, (r + 1) * QB)
            s = s_ref[slot, :, h * QB:(h + 1) * QB]
            m = m_ref[g:g + 1, hl]
            m_new = jnp.maximum(m, mt_ref[slot, h:h + 1, :])
            m_safe = jnp.where(m_new == -jnp.inf, 0.0, m_new)
            alpha = jnp.exp2(m - m_safe)
            p = jnp.exp2(s - m_safe)
            l_ref[g:g + 1, hl] = alpha * l_ref[g:g + 1, hl] + jnp.sum(_fold64(p, jnp.add), axis=0, keepdims=True)
            pv = _nn(vT_ref[jc, g * HEAD_DIM:(g + 1) * HEAD_DIM, :], p.astype(BF16))
            acc_ref[g, :, hl] = alpha * acc_ref[g, :, hl] + pv
            m_ref[g:g + 1, hl] = m_new

    masked_logits(0, 0)

    def kv_step(jj, c):
        a = 2 * jj
        masked_logits(a + 1, 1)
        softmax_pv(a, 0)
        masked_logits(a + 2, 0)
        softmax_pv(a + 1, 1)
        return c

    lax.fori_loop(0, (i + 2) // 2, kv_step, 0)
    for h in range(Q_HEADS):
        g, r = divmod(h, rep)
        inv_l = 1.0 / l_ref[g:g + 1, r * QB:(r + 1) * QB]
        oh = (acc_ref[g, :, r * QB:(r + 1) * QB] * inv_l).T
        zh = z_ref[:, h * HEAD_DIM:(h + 1) * HEAD_DIM]
        o_ref[:, h * HEAD_DIM:(h + 1) * HEAD_DIM] = (oh * _silu(zh)).astype(o_ref.dtype)


def _attention(qT, qiT, wT, znat, zcol, k, vT, kidx, bsz, seq):
    nq = seq // QB
    topk = min(TOPK_MAX, seq // 4)
    m = bsz * seq
    step = lambda r: pl.BlockSpec((r, QB), lambda b, i: (0, b * nq + i))
    return pl.pallas_call(
        functools.partial(_attn_kernel, topk=topk),
        grid=(bsz, nq),
        in_specs=[step(1024), step(1024), step(IDX_HEADS),
                  pl.BlockSpec((QB, ATTN_WIDTH), lambda b, i: (b * nq + i, zcol)),
                  pl.BlockSpec((seq, 256), lambda b, i: (b, 0)),
                  pl.BlockSpec((seq // KTA, 256, KTA), lambda b, i: (b, 0, 0)),
                  pl.BlockSpec((seq, IDX_DIM), lambda b, i: (b, 0))],
        out_specs=pl.BlockSpec((QB, ATTN_WIDTH), lambda b, i: (b * nq + i, 0)),
        out_shape=jax.ShapeDtypeStruct((m, ATTN_WIDTH), BF16),
        scratch_shapes=[pltpu.VMEM((seq, QB), I32), pltpu.VMEM((32, seq // 32, QB), I32),
                        pltpu.VMEM((seq, QB), F32),
                        pltpu.VMEM((KV_HEADS, Q_HEADS // KV_HEADS * QB), F32),
                        pltpu.VMEM((KV_HEADS, Q_HEADS // KV_HEADS * QB), F32),
                        pltpu.VMEM((KV_HEADS, HEAD_DIM, Q_HEADS // KV_HEADS * QB), F32),
                        pltpu.VMEM((2, KTA, Q_HEADS * QB), F32), pltpu.VMEM((2, Q_HEADS, QB), F32)],
        compiler_params=pltpu.CompilerParams(dimension_semantics=("arbitrary", "arbitrary"),
                                             vmem_limit_bytes=VMEM_LIMIT),
        name="dsa_attention",
    )(qT, qiT, wT, znat, k, vT, kidx)


def _ssd_kernel(xbc_ref, z_ref, dt_ref, cw_ref, cb_ref, dtb_ref, a_ref, dsk_ref, nw_ref, e2_ref, tril_ref, o_ref,
                buf_ref, tail_ref, state_ref, xc_ref, xdt_ref, acum_ref, ahi_ref, alo_ref, *, tt):
    t = pl.program_id(1)

    @pl.when(t == 0)
    def _():
        tail_ref[...] = jnp.zeros_like(tail_ref)
        state_ref[...] = jnp.zeros_like(state_ref)

    buf_ref[0:8, :] = tail_ref[...]
    buf_ref[8:8 + tt, :] = xbc_ref[...]
    tail_ref[...] = xbc_ref[tt - 8:tt, :]

    dtv = dt_ref[...] + dtb_ref[...]
    dtv = jnp.maximum(dtv, 0.0) + jnp.log(1.0 + jnp.exp(-jnp.abs(dtv)))
    dhi = dtv.astype(BF16).astype(F32)
    lane = lax.broadcasted_iota(I32, (1, 128), 1)
    dsplit = jnp.where(lane < SSD_HEADS, dhi, dtv - dhi).astype(BF16)
    acum_ref[...] = _nn(dsplit, e2_ref[...])
    neg_a = -jnp.exp(a_ref[...])

    for c in range(tt // CHUNK):
        r0 = c * CHUNK
        conv = cb_ref[...] + cw_ref[3:4, :] * buf_ref[8 + r0:8 + r0 + CHUNK, :]
        for jtap in range(SSD_CONV - 1):
            conv += cw_ref[jtap:jtap + 1, :] * buf_ref[5 + jtap + r0:5 + jtap + r0 + CHUNK, :]
        xc = _silu(conv)
        xc_ref[r0:r0 + CHUNK, :] = xc
        dt_e = acum_ref[r0:r0 + CHUNK, :]
        xdt_ref[r0:r0 + CHUNK, :] = xc[:, :SSD_WIDTH] * dt_e
        dta_e = dt_e * neg_a
        ahi = dta_e.astype(BF16)
        ahi_ref[r0:r0 + CHUNK, :] = ahi
        alo_ref[r0:r0 + CHUNK, :] = (dta_e - ahi.astype(F32)).astype(BF16)
    acum_ref[...] = _nn(tril_ref[...], ahi_ref[...]) + _nn(tril_ref[...], alo_ref[...])

    gw = SSD_WIDTH // SSD_GROUPS
    s_row = lax.broadcasted_iota(I32, (CHUNK, gw), 0)
    s_lane = lax.broadcasted_iota(I32, (CHUNK, gw), 1) & (CHUNK - 1)
    diag = s_row == s_lane
    tril = s_row >= s_lane
    bd_r = lax.broadcasted_iota(I32, (2 * SSD_P, 2 * SSD_P), 0) >> 6
    bd_c = lax.broadcasted_iota(I32, (2 * SSD_P, 2 * SSD_P), 1) >> 6
    blockdiag = bd_r == bd_c

    for c in range(tt // CHUNK):
        rows = slice(c * CHUNK, (c + 1) * CHUNK)
        for g in range(SSD_GROUPS):
            gl = slice(g * gw, (g + 1) * gw)
            bg = xc_ref[rows, SSD_WIDTH + g * SSD_N:SSD_WIDTH + (g + 1) * SSD_N]
            cg = xc_ref[rows, SSD_WIDTH + (SSD_GROUPS + g) * SSD_N:SSD_WIDTH + (SSD_GROUPS + g + 1) * SSD_N]
            bg16, cg16 = bg.astype(BF16), cg.astype(BF16)
            a_g = acum_ref[rows, gl]
            a_last = a_g[CHUNK - 1:CHUNK, :]
            xdt_g = xdt_ref[rows, gl]
            cb = _nt(cg16, jnp.concatenate([bg16] * (gw // CHUNK), axis=0))
            a_row = jnp.sum(jnp.where(diag, a_g, 0.0), axis=0, keepdims=True)
            lmat = jnp.exp(jnp.where(tril, a_g - a_row, -jnp.inf))
            mmat = (cb * lmat).astype(BF16)
            ys = []
            for q in range(gw // (2 * SSD_P)):
                ql = slice(q * 2 * SSD_P, (q + 1) * 2 * SSD_P)
                x2 = xdt_g[:, ql]
                wq = jnp.where(blockdiag, jnp.concatenate([x2, x2], axis=0), 0.0).astype(BF16)
                ys.append(_nn(mmat[:, ql], wq))
            y = jnp.concatenate(ys, axis=1)
            prev = state_ref[g]
            y += _nn(cg16, prev.astype(BF16)) * jnp.exp(a_g)
            y += dsk_ref[:, gl] * xc_ref[rows, gl]
            xdd = (xdt_g * jnp.exp(a_last - a_g)).astype(BF16)
            state_ref[g] = prev * jnp.exp(a_last) + _nn(bg.T.astype(BF16), xdd)
            gz = y * _silu(z_ref[rows, gl])
            gz = gz * lax.rsqrt(jnp.mean(gz * gz, axis=1, keepdims=True) + RMS_EPS)
            o_ref[rows, gl] = (gz * nw_ref[:, gl]).astype(o_ref.dtype)


def _ssd(nat, conv_w, conv_b, dt_bias, a_log, d_skip, norm_w, bsz, seq, tt):
    m = bsz * seq
    nt = seq // tt
    row = lambda b, t: b * nt + t
    full = lambda a: pl.BlockSpec(a.shape, lambda b, t: (0, 0))
    expand = lambda v: jnp.repeat(v.astype(F32), SSD_P).reshape(1, SSD_WIDTH)
    dtb2 = _pad_lanes(jnp.concatenate([dt_bias, dt_bias]))
    j_i = lax.broadcasted_iota(I32, (128, SSD_WIDTH), 0)
    h_i = lax.broadcasted_iota(I32, (128, SSD_WIDTH), 1) // SSD_P
    e2 = ((j_i == h_i) | (j_i == h_i + SSD_HEADS)).astype(BF16)
    r_i = lax.broadcasted_iota(I32, (tt, tt), 0)
    c_i = lax.broadcasted_iota(I32, (tt, tt), 1)
    tril_bd = ((r_i >= c_i) & (r_i // CHUNK == c_i // CHUNK)).astype(BF16)
    return pl.pallas_call(
        functools.partial(_ssd_kernel, tt=tt),
        grid=(bsz, nt),
        in_specs=[pl.BlockSpec((tt, XBC_WIDTH), lambda b, t: (row(b, t), 0)),
                  pl.BlockSpec((tt, SSD_WIDTH), lambda b, t: (row(b, t), XBC_WIDTH // SSD_WIDTH)),
                  pl.BlockSpec((tt, 128), lambda b, t: (row(b, t), (XBC_WIDTH + SSD_WIDTH + ATTN_WIDTH) // 128)),
                  full(conv_w), full(conv_b), full(dtb2), pl.BlockSpec((1, SSD_WIDTH), lambda b, t: (0, 0)),
                  pl.BlockSpec((1, SSD_WIDTH), lambda b, t: (0, 0)), full(norm_w), full(e2), full(tril_bd)],
        out_specs=pl.BlockSpec((tt, SSD_WIDTH), lambda b, t: (row(b, t), 0)),
        out_shape=jax.ShapeDtypeStruct((m, SSD_WIDTH), BF16),
        scratch_shapes=[pltpu.VMEM((tt + 8, XBC_WIDTH), F32), pltpu.VMEM((8, XBC_WIDTH), F32),
                        pltpu.VMEM((SSD_GROUPS, SSD_N, SSD_WIDTH // SSD_GROUPS), F32),
                        pltpu.VMEM((tt, XBC_WIDTH), F32), pltpu.VMEM((tt, SSD_WIDTH), F32),
                        pltpu.VMEM((tt, SSD_WIDTH), F32),
                        pltpu.VMEM((tt, SSD_WIDTH), BF16), pltpu.VMEM((tt, SSD_WIDTH), BF16)],
        compiler_params=pltpu.CompilerParams(dimension_semantics=("arbitrary", "arbitrary"),
                                             vmem_limit_bytes=VMEM_LIMIT),
        name="ssd_mixer",
    )(nat, nat, nat, conv_w, conv_b, dtb2, expand(a_log), expand(d_skip), norm_w, e2, tril_bd)


def _out_kernel(oa_ref, os_ref, x_ref, wa_ref, ws_ref, g_ref, b_ref, o_ref, *, alpha):
    nsplit = 4
    rb = o_ref.shape[0] // nsplit
    for r in range(nsplit):
        rows = slice(r * rb, (r + 1) * rb)
        sub = _nn(oa_ref[rows, :], wa_ref[...]) + _nn(os_ref[rows, :], ws_ref[...])
        y = alpha * x_ref[rows, :] + sub
        mu = jnp.mean(y, axis=1, keepdims=True)
        yc = y - mu
        var = jnp.mean(yc * yc, axis=1, keepdims=True)
        o_ref[rows, :] = yc * lax.rsqrt(var + LN_EPS) * g_ref[...] + b_ref[...]


def _out_proj(oa, os_, x2, wa, ws, g, b, alpha, tm):
    m = x2.shape[0]
    full = lambda a: pl.BlockSpec(a.shape, lambda i: (0, 0))
    rowblk = lambda w: pl.BlockSpec((tm, w), lambda i: (i, 0))
    return pl.pallas_call(
        functools.partial(_out_kernel, alpha=alpha),
        grid=(m // tm,),
        in_specs=[rowblk(ATTN_WIDTH), rowblk(SSD_WIDTH), rowblk(D_MODEL), full(wa), full(ws), full(g), full(b)],
        out_specs=rowblk(D_MODEL),
        out_shape=jax.ShapeDtypeStruct((m, D_MODEL), F32),
        compiler_params=pltpu.CompilerParams(dimension_semantics=("arbitrary",), vmem_limit_bytes=VMEM_LIMIT),
        name="out_proj_ln",
    )(oa, os_, x2, wa, ws, g, b)


def _rope_tables(seq, rot):
    half = rot // 2
    inv = ROPE_THETA ** (-jnp.arange(half, dtype=F32) * 2.0 / rot)
    ang = inv[:, None] * jnp.arange(seq, dtype=F32)[None, :]
    return jnp.cos(ang), jnp.sin(ang)


def _pad_lanes(v, width=128):
    v = v.reshape(1, -1).astype(F32)
    return jnp.pad(v, ((0, 0), (0, width - v.shape[1])))


def _layer(h, w_in, w_out, conv_w, conv_b, dt_bias, a_log, d_skip, norm_w, ln_g, ln_b, alpha):
    bsz, seq, _ = h.shape
    m = bsz * seq
    assert seq % QB == 0 and seq % 512 == 0
    x2 = h.reshape(m, D_MODEL)

    o_q, o_k, o_v, o_za, o_qi, o_ki, o_wi, o_zs, o_xbc, o_dt = (
        0, 1024, 1280, 1536, 2560, 3584, 3648, 3664, 4688, 6736)
    wqkv_t = w_in[:, o_q:o_za].T.astype(BF16)
    widx_t = w_in[:, o_qi:o_zs].T.astype(BF16)
    wq = wqkv_t[:o_k]
    wkv = wqkv_t[o_k:]
    wqi = widx_t[:o_ki - o_qi]
    wkw = jnp.pad(widx_t[o_ki - o_qi:], ((0, 128 - (o_zs - o_ki)), (0, 0)))
    wn = jnp.concatenate([w_in[:, o_xbc:o_dt], w_in[:, o_zs:o_xbc], w_in[:, o_za:o_qi],
                          jnp.pad(jnp.concatenate([w_in[:, o_dt:], w_in[:, o_dt:]], axis=1),
                                  ((0, 0), (0, 128 - 2 * SSD_HEADS)))],
                         axis=1).astype(BF16)

    ca, sa = _rope_tables(seq, HEAD_DIM // ROPE_DIV)
    ci, si = _rope_tables(seq, IDX_DIM // ROPE_DIV)

    qT, k, vT, qiT, kidx, wT, xb = _proj_t(x2, wq, wkv, wqi, wkw, ca, sa, ci, si, seq, tm=512)
    nat = _proj_n(xb, wn, tm=min(1024, m), tn=1408)

    o_attn = _attention(qT, qiT, wT, nat, (XBC_WIDTH + SSD_WIDTH) // ATTN_WIDTH, k, vT, kidx, bsz, seq)
    o_ssd = _ssd(nat, conv_w, conv_b.reshape(1, -1), dt_bias, a_log, d_skip, norm_w.reshape(1, -1),
                 bsz, seq, tt=256)

    wo = w_out.astype(BF16)
    out = _out_proj(o_attn, o_ssd, x2, wo[:ATTN_WIDTH], wo[ATTN_WIDTH:], ln_g.reshape(1, -1),
                    ln_b.reshape(1, -1), alpha, tm=512)
    return out.reshape(bsz, seq, D_MODEL)


def kernel(x, w_in, w_out, conv_w, conv_b, dt_bias, a_log, d_skip, ssd_norm_w, ln_g, ln_b):
    depth = w_in.shape[0]
    alpha = (2.0 * depth) ** 0.25
    h = x
    for layer in range(depth):
        h = _layer(h, w_in[layer], w_out[layer], conv_w[layer], conv_b[layer], dt_bias[layer],
                   a_log[layer], d_skip[layer], ssd_norm_w[layer], ln_g[layer], ln_b[layer], alpha)
    return h
```

```python
import functools
import math

import jax
import jax.numpy as jnp
from jax import lax
from jax.experimental import pallas as pl
from jax.experimental.pallas import tpu as pltpu

F32 = jnp.float32
BF16 = jnp.bfloat16
I32 = jnp.int32

D_MODEL = 2048
CHUNK = 64
ATTN_WIDTH = 1024
SSD_WIDTH = 1024
HEAD_DIM = 128
Q_HEADS = 8
KV_HEADS = 2
ROPE_THETA = 500000.0
ROPE_DIV = 4
IDX_HEADS = 16
IDX_DIM = 64
TOPK_MAX = 256
SSD_P = 64
SSD_HEADS = 16
SSD_GROUPS = 4
SSD_N = 128
SSD_CONV = 4
XBC_WIDTH = SSD_WIDTH + 2 * SSD_GROUPS * SSD_N
LN_EPS = 1e-5
RMS_EPS = 1e-5

VMEM_LIMIT = 56 * 1024 * 1024
INT_MIN = -(2 ** 31)

QB = 256
KT = 128
KTA = 256


def _nt(a, b):
    return lax.dot_general(a, b, (((1,), (1,)), ((), ())), preferred_element_type=F32)


def _nn(a, b):
    return jnp.dot(a, b, preferred_element_type=F32)


def _silu(v):
    return v * (1.0 / (1.0 + jnp.exp(-v)))


def _rope_rows(t, cos, sin, nheads, hd, half):
    pieces = []
    for h in range(nheads):
        b = h * hd
        x1 = t[b:b + half]
        x2 = t[b + half:b + 2 * half]
        pieces += [x1 * cos - x2 * sin, x2 * cos + x1 * sin, t[b + 2 * half:b + hd]]
    return jnp.concatenate(pieces, axis=0)


def _proj_t_kernel(x_ref, wq_ref, wkv_ref, wqi_ref, wkw_ref, ca_ref, sa_ref, ci_ref, si_ref,
                   q_ref, k_ref, v_ref, qi_ref, kidx_ref, w_ref, xb_ref):
    x = x_ref[...].astype(BF16)
    xb_ref[...] = x
    ca, sa, ci, si = ca_ref[...], sa_ref[...], ci_ref[...], si_ref[...]
    qscale = HEAD_DIM ** -0.5 * math.log2(math.e)
    tq = _nt(wq_ref[...], x)
    q_ref[...] = (_rope_rows(tq, ca, sa, Q_HEADS, HEAD_DIM, 16) * qscale).astype(BF16)
    tkv = _nt(wkv_ref[...], x)
    k_ref[...] = _rope_rows(tkv[:KV_HEADS * HEAD_DIM], ca, sa, KV_HEADS, HEAD_DIM, 16).T.astype(BF16)
    for jj in range(v_ref.shape[0]):
        v_ref[jj] = tkv[KV_HEADS * HEAD_DIM:, jj * KTA:(jj + 1) * KTA].astype(BF16)
    tqi = _nt(wqi_ref[...], x)
    qi_ref[...] = _rope_rows(tqi, ci, si, IDX_HEADS, IDX_DIM, 8).astype(BF16)
    tkw = _nt(wkw_ref[...], x)
    kidx_ref[...] = _rope_rows(tkw[:IDX_DIM], ci, si, 1, IDX_DIM, 8).T.astype(BF16)
    w_ref[...] = tkw[IDX_DIM:IDX_DIM + IDX_HEADS] * (IDX_HEADS ** -0.5 * IDX_DIM ** -0.5)


def _proj_t(x2, wq, wkv, wqi, wkw, ca, sa, ci, si, seq, tm):
    m, kdim = x2.shape
    nper = seq // tm
    full = lambda a: pl.BlockSpec(a.shape, lambda i: (0, 0))
    tab = lambda a: pl.BlockSpec((a.shape[0], tm), lambda i: (0, i % nper))
    cols = lambda r: pl.BlockSpec((r, tm), lambda i: (0, i))
    rows = lambda c: pl.BlockSpec((tm, c), lambda i: (i, 0))
    vspec = pl.BlockSpec((tm // KTA, 256, KTA), lambda i: (i, 0, 0))
    sds = jax.ShapeDtypeStruct
    return pl.pallas_call(
        _proj_t_kernel,
        grid=(m // tm,),
        in_specs=[pl.BlockSpec((tm, kdim), lambda i: (i, 0)), full(wq), full(wkv), full(wqi), full(wkw),
                  tab(ca), tab(sa), tab(ci), tab(si)],
        out_specs=[cols(1024), rows(256), vspec, cols(1024), rows(IDX_DIM), cols(IDX_HEADS), rows(kdim)],
        out_shape=[sds((1024, m), BF16), sds((m, 256), BF16), sds((m // KTA, 256, KTA), BF16),
                   sds((1024, m), BF16), sds((m, IDX_DIM), BF16), sds((IDX_HEADS, m), F32), sds((m, kdim), BF16)],
        compiler_params=pltpu.CompilerParams(dimension_semantics=("arbitrary",), vmem_limit_bytes=VMEM_LIMIT),
        name="proj_t",
    )(x2, wq, wkv, wqi, wkw, ca, sa, ci, si)


def _proj_n_kernel(x_ref, w_ref, o_ref):
    o_ref[...] = _nn(x_ref[...], w_ref[...]).astype(o_ref.dtype)


def _proj_n(xb, w, tm, tn):
    m, kdim = xb.shape
    n = w.shape[1]
    return pl.pallas_call(
        _proj_n_kernel,
        grid=(n // tn, m // tm),
        in_specs=[pl.BlockSpec((tm, kdim), lambda j, i: (i, 0)), pl.BlockSpec((kdim, tn), lambda j, i: (0, j))],
        out_specs=pl.BlockSpec((tm, tn), lambda j, i: (i, j)),
        out_shape=jax.ShapeDtypeStruct((m, n), F32),
        compiler_params=pltpu.CompilerParams(dimension_semantics=("arbitrary", "arbitrary"),
                                             vmem_limit_bytes=VMEM_LIMIT),
        name="proj_n",
    )(xb, w)


def _fold64(x, op2):
    parts = [x[a * 64:(a + 1) * 64] for a in range(x.shape[0] // 64)]
    while len(parts) > 1:
        parts = [op2(parts[2 * a], parts[2 * a + 1]) for a in range(len(parts) // 2)]
    return parts[0]


def _fold8_tree(x):
    parts = [x[a * 8:(a + 1) * 8] for a in range(x.shape[0] // 8)]
    while len(parts) > 1:
        parts = [parts[2 * a] + parts[2 * a + 1] for a in range(len(parts) // 2)]
    return parts[0]


def _bit_transpose32(words):
    a = list(words)
    mask, j = 0x0000FFFF, 16
    while j:
        k = 0
        while k < 32:
            t = (a[k] ^ (a[k + j] >> j)) & mask
            a[k] = a[k] ^ t
            a[k + j] = a[k + j] ^ (t << j)
            k = (k + j + 1) & ~j
        j >>= 1
        mask = (mask ^ (mask << j)) & 0xFFFFFFFF
    return a


def _attn_kernel(qT_ref, qiT_ref, wT_ref, z_ref, k_ref, vT_ref, kidx_ref, o_ref,
                 key_ref, plane_ref, bias_ref, m_ref, l_ref, acc_ref, s_ref, mt_ref, *, topk):
    i = pl.program_id(1)
    nfull = 2 * i
    q_chunk = (i * QB + lax.broadcasted_iota(I32, (1, QB), 1)) // CHUNK

    def score_tile(j):
        rows = pl.ds(pl.multiple_of(j * KT, KT), KT)
        kt = kidx_ref[rows, :]
        acc = jnp.zeros((KT, QB), F32)
        for h in range(IDX_HEADS):
            qh = qiT_ref[h * IDX_DIM:(h + 1) * IDX_DIM, :]
            acc += jnp.maximum(_nn(kt, qh), 0.0) * wT_ref[h:h + 1, :]
        bits = pltpu.bitcast(acc, I32)
        return rows, bits ^ ((bits >> 31) & 0x7FFFFFFF)

    def store_planes(t256, keys):
        words = []
        for kk in keys:
            u = kk ^ INT_MIN
            words += [u[8 * a:8 * a + 8] for a in range(kk.shape[0] // 8)]
        planes = _bit_transpose32(words[::-1])
        prow = pl.ds(pl.multiple_of(t256 * 8, 8), 8)
        for b in range(32):
            plane_ref[b, prow, :] = planes[b]

    def planes_from_keys(t256):
        rows = pl.ds(pl.multiple_of(t256 * 2 * KT, 2 * KT), 2 * KT)
        store_planes(t256, [key_ref[rows, :]])

    @pl.when((pl.program_id(0) == 0) & (i == 0))
    def _():
        key_ref[...] = jnp.zeros(key_ref.shape, I32)

    def full_tiles(j2, c):
        planes_from_keys(jnp.maximum(j2 - 1, 0))
        for jj in range(2):
            rows, key = score_tile(2 * j2 + jj)
            key_ref[rows, :] = key
        return c

    lax.fori_loop(0, i, full_tiles, 0)
    planes_from_keys(jnp.maximum(i - 1, 0))
    keys = []
    for jj in range(QB // KT):
        j = nfull + jj
        rows, key = score_tile(j)
        k_chunk = (j * KT + lax.broadcasted_iota(I32, (KT, 1), 0)) // CHUNK
        key = jnp.where(k_chunk <= q_chunk, key, INT_MIN)
        key_ref[rows, :] = key
        keys.append(key)
    store_planes(i, keys)

    nprow = plane_ref.shape[1]
    t_row = lax.broadcasted_iota(I32, (nprow, QB), 0) >> 3
    qc_local = lax.broadcasted_iota(I32, (1, QB), 1) // CHUNK
    diag_bits = lax.shift_right_logical(jnp.full((1, QB), -1, I32), 32 - 8 * (qc_local + 1))
    act0 = jnp.where(t_row < i, -1, jnp.where(t_row == i, diag_bits, 0))

    def radix_select(nrow):
        act = act0[:nrow]
        above = jnp.zeros((1, QB), I32)
        thr_u = jnp.zeros((1, QB), I32)
        for b in range(32):
            ones = act & plane_ref[b, 0:nrow, :]
            c8 = _fold8_tree(lax.population_count(ones))
            c = above + jnp.sum(c8, axis=0, keepdims=True)
            take = c >= topk
            act = jnp.where(take, ones, act ^ ones)
            above = jnp.where(take, above, c)
            thr_u = thr_u | jnp.where(take, jnp.int32(-(2 ** 31) if b == 0 else 1 << (31 - b)), 0)
        if nrow < nprow:
            act = jnp.concatenate([act, jnp.zeros((nprow - nrow, QB), I32)], axis=0)
        return thr_u, above, act

    half = nprow // 2
    thr_u, above, act = lax.cond((i + 1) * 8 <= half, lambda: radix_select(half), lambda: radix_select(nprow))
    thr = jnp.maximum(thr_u ^ INT_MIN, INT_MIN + 1)
    need = topk - above
    n_tied = jnp.sum(_fold8_tree(lax.population_count(act)), axis=0, keepdims=True)
    any_tie = jnp.max(jnp.where(n_tied > need, 1, 0)) > 0

    @pl.when(jnp.logical_not(any_tie))
    def _():
        def to_bias(j, c):
            rows = pl.ds(pl.multiple_of(j * KTA, KTA), KTA)
            bias_ref[rows, :] = jnp.where(key_ref[rows, :] >= thr, 0.0, -jnp.inf)
            return c

        lax.fori_loop(0, i + 1, to_bias, 0)

    @pl.when(any_tie)
    def _():
        p_row = lax.broadcasted_iota(I32, (nprow, QB), 0)
        word_bit_clear = {7: 0x0000FFFF, 6: 0x00FF00FF, 5: 0x0F0F0F0F, 4: 0x33333333, 3: 0x55555555}
        cand, left, idx_thr = act, need, jnp.zeros((1, QB), I32)
        for b in range((bias_ref.shape[0] - 1).bit_length() - 1, -1, -1):
            if b >= 8:
                zmask = jnp.where(((p_row >> (3 + b - 8)) & 1) == 0, -1, 0)
            elif b >= 3:
                zmask = word_bit_clear[b]
            else:
                zmask = jnp.where(((p_row >> b) & 1) == 0, -1, 0)
            zeros = cand & zmask
            c0 = jnp.sum(_fold8_tree(lax.population_count(zeros)), axis=0, keepdims=True)
            low = c0 >= left
            cand = jnp.where(low, zeros, cand ^ zeros)
            left = jnp.where(low, left, left - c0)
            idx_thr = idx_thr | jnp.where(low, 0, 1 << b)

        def to_bias_tied(j, c):
            rows = pl.ds(pl.multiple_of(j * KTA, KTA), KTA)
            key = key_ref[rows, :]
            kidx_abs = j * KTA + lax.broadcasted_iota(I32, (KTA, 1), 0)
            keep = (key > thr) | ((key == thr) & (kidx_abs <= idx_thr))
            bias_ref[rows, :] = jnp.where(keep, 0.0, -jnp.inf)
            return c

        lax.fori_loop(0, i + 1, to_bias_tied, 0)

    m_ref[...] = jnp.full(m_ref.shape, -jnp.inf, F32)
    l_ref[...] = jnp.zeros(l_ref.shape, F32)
    acc_ref[...] = jnp.zeros(acc_ref.shape, F32)

    rep = Q_HEADS // KV_HEADS
    gq = rep * QB
    ntile = bias_ref.shape[0] // KTA
    jpad = jnp.minimum(i + 1, ntile - 1)

    @pl.when(i + 1 < ntile)
    def _():
        bias_ref[pl.ds(pl.multiple_of((i + 1) * KTA, KTA), KTA), :] = jnp.full((KTA, QB), -jnp.inf, F32)

    def masked_logits(j, slot):
        jc = jnp.minimum(j, jpad)
        rows = pl.ds(pl.multiple_of(jc * KTA, KTA), KTA)
        for g in range(KV_HEADS):
            qg = jnp.concatenate([qT_ref[(g * rep + r) * HEAD_DIM:(g * rep + r + 1) * HEAD_DIM, :]
                                  for r in range(rep)], axis=1)
            sg = _nn(k_ref[rows, g * HEAD_DIM:(g + 1) * HEAD_DIM], qg)
            for r in range(rep):
                h = g * rep + r
                sh = sg[:, r * QB:(r + 1) * QB] + bias_ref[rows, :]
                s_ref[slot, :, h * QB:(h + 1) * QB] = sh
                mt_ref[slot, h:h + 1, :] = jnp.max(_fold64(sh, jnp.maximum), axis=0, keepdims=True)

    def softmax_pv(j, slot):
        jc = jnp.minimum(j, jpad)
        for h in range(Q_HEADS):
            g, r = divmod(h, rep)
            hl = slice(r * QB, (r + 1) * QB)
            s = s_ref[slot, :, h * QB:(h + 1) * QB]
            m = m_ref[g:g + 1, hl]
            m_new = jnp.maximum(m, mt_ref[slot, h:h + 1, :])
            m_safe = jnp.where(m_new == -jnp.inf, 0.0, m_new)
            alpha = jnp.exp2(m - m_safe)
            p = jnp.exp2(s - m_safe)
            l_ref[g:g + 1, hl] = alpha * l_ref[g:g + 1, hl] + jnp.sum(_fold64(p, jnp.add), axis=0, keepdims=True)
            pv = _nn(vT_ref[jc, g * HEAD_DIM:(g + 1) * HEAD_DIM, :], p.astype(BF16))
            acc_ref[g, :, hl] = alpha * acc_ref[g, :, hl] + pv
            m_ref[g:g + 1, hl] = m_new

    masked_logits(0, 0)

    def kv_step(jj, c):
        a = 2 * jj
        masked_logits(a + 1, 1)
        softmax_pv(a, 0)
        masked_logits(a + 2, 0)
        softmax_pv(a + 1, 1)
        return c

    lax.fori_loop(0, (i + 2) // 2, kv_step, 0)
    for h in range(Q_HEADS):
        g, r = divmod(h, rep)
        inv_l = 1.0 / l_ref[g:g + 1, r * QB:(r + 1) * QB]
        oh = (acc_ref[g, :, r * QB:(r + 1) * QB] * inv_l).T
        zh = z_ref[:, h * HEAD_DIM:(h + 1) * HEAD_DIM]
        o_ref[:, h * HEAD_DIM:(h + 1) * HEAD_DIM] = (oh * _silu(zh)).astype(o_ref.dtype)


def _attention(qT, qiT, wT, znat, zcol, k, vT, kidx, bsz, seq):
    nq = seq // QB
    topk = min(TOPK_MAX, seq // 4)
    m = bsz * seq
    step = lambda r: pl.BlockSpec((r, QB), lambda b, i: (0, b * nq + i))
    return pl.pallas_call(
        functools.partial(_attn_kernel, topk=topk),
        grid=(bsz, nq),
        in_specs=[step(1024), step(1024), step(IDX_HEADS),
                  pl.BlockSpec((QB, ATTN_WIDTH), lambda b, i: (b * nq + i, zcol)),
                  pl.BlockSpec((seq, 256), lambda b, i: (b, 0)),
                  pl.BlockSpec((seq // KTA, 256, KTA), lambda b, i: (b, 0, 0)),
                  pl.BlockSpec((seq, IDX_DIM), lambda b, i: (b, 0))],
        out_specs=pl.BlockSpec((QB, ATTN_WIDTH), lambda b, i: (b * nq + i, 0)),
        out_shape=jax.ShapeDtypeStruct((m, ATTN_WIDTH), BF16),
        scratch_shapes=[pltpu.VMEM((seq, QB), I32), pltpu.VMEM((32, seq // 32, QB), I32),
                        pltpu.VMEM((seq, QB), F32),
                        pltpu.VMEM((KV_HEADS, Q_HEADS // KV_HEADS * QB), F32),
                        pltpu.VMEM((KV_HEADS, Q_HEADS // KV_HEADS * QB), F32),
                        pltpu.VMEM((KV_HEADS, HEAD_DIM, Q_HEADS // KV_HEADS * QB), F32),
                        pltpu.VMEM((2, KTA, Q_HEADS * QB), F32), pltpu.VMEM((2, Q_HEADS, QB), F32)],
        compiler_params=pltpu.CompilerParams(dimension_semantics=("arbitrary", "arbitrary"),
                                             vmem_limit_bytes=VMEM_LIMIT),
        name="dsa_attention",
    )(qT, qiT, wT, znat, k, vT, kidx)


def _ssd_kernel(xbc_ref, z_ref, dt_ref, cw_ref, cb_ref, dtb_ref, a_ref, dsk_ref, nw_ref, e2_ref, tril_ref, o_ref,
                buf_ref, tail_ref, state_ref, xc_ref, xdt_ref, acum_ref, ahi_ref, alo_ref, *, tt):
    t = pl.program_id(1)

    @pl.when(t == 0)
    def _():
        tail_ref[...] = jnp.zeros_like(tail_ref)
        state_ref[...] = jnp.zeros_like(state_ref)

    buf_ref[0:8, :] = tail_ref[...]
    buf_ref[8:8 + tt, :] = xbc_ref[...]
    tail_ref[...] = xbc_ref[tt - 8:tt, :]

    dtv = dt_ref[...] + dtb_ref[...]
    dtv = jnp.maximum(dtv, 0.0) + jnp.log(1.0 + jnp.exp(-jnp.abs(dtv)))
    dhi = dtv.astype(BF16).astype(F32)
    lane = lax.broadcasted_iota(I32, (1, 128), 1)
    dsplit = jnp.where(lane < SSD_HEADS, dhi, dtv - dhi).astype(BF16)
    acum_ref[...] = _nn(dsplit, e2_ref[...])
    neg_a = -jnp.exp(a_ref[...])

    for c in range(tt // CHUNK):
        r0 = c * CHUNK
        conv = cb_ref[...] + cw_ref[3:4, :] * buf_ref[8 + r0:8 + r0 + CHUNK, :]
        for jtap in range(SSD_CONV - 1):
            conv += cw_ref[jtap:jtap + 1, :] * buf_ref[5 + jtap + r0:5 + jtap + r0 + CHUNK, :]
        xc = _silu(conv)
        xc_ref[r0:r0 + CHUNK, :] = xc
        dt_e = acum_ref[r0:r0 + CHUNK, :]
        xdt_ref[r0:r0 + CHUNK, :] = xc[:, :SSD_WIDTH] * dt_e
        dta_e = dt_e * neg_a
        ahi = dta_e.astype(BF16)
        ahi_ref[r0:r0 + CHUNK, :] = ahi
        alo_ref[r0:r0 + CHUNK, :] = (dta_e - ahi.astype(F32)).astype(BF16)
    acum_ref[...] = _nn(tril_ref[...], ahi_ref[...]) + _nn(tril_ref[...], alo_ref[...])

    gw = SSD_WIDTH // SSD_GROUPS
    s_row = lax.broadcasted_iota(I32, (CHUNK, gw), 0)
    s_lane = lax.broadcasted_iota(I32, (CHUNK, gw), 1) & (CHUNK - 1)
    diag = s_row == s_lane
    tril = s_row >= s_lane
    bd_r = lax.broadcasted_iota(I32, (2 * SSD_P, 2 * SSD_P), 0) >> 6
    bd_c = lax.broadcasted_iota(I32, (2 * SSD_P, 2 * SSD_P), 1) >> 6
    blockdiag = bd_r == bd_c

    for c in range(tt // CHUNK):
        rows = slice(c * CHUNK, (c + 1) * CHUNK)
        for g in range(SSD_GROUPS):
            gl = slice(g * gw, (g + 1) * gw)
            bg = xc_ref[rows, SSD_WIDTH + g * SSD_N:SSD_WIDTH + (g + 1) * SSD_N]
            cg = xc_ref[rows, SSD_WIDTH + (SSD_GROUPS + g) * SSD_N:SSD_WIDTH + (SSD_GROUPS + g + 1) * SSD_N]
            bg16, cg16 = bg.astype(BF16), cg.astype(BF16)
            a_g = acum_ref[rows, gl]
            a_last = a_g[CHUNK - 1:CHUNK, :]
            xdt_g = xdt_ref[rows, gl]
            cb = _nt(cg16, jnp.concatenate([bg16] * (gw // CHUNK), axis=0))
            a_row = jnp.sum(jnp.where(diag, a_g, 0.0), axis=0, keepdims=True)
            lmat = jnp.exp(jnp.where(tril, a_g - a_row, -jnp.inf))
            mmat = (cb * lmat).astype(BF16)
            ys = []
            for q in range(gw // (2 * SSD_P)):
                ql = slice(q * 2 * SSD_P, (q + 1) * 2 * SSD_P)
                x2 = xdt_g[:, ql]
                wq = jnp.where(blockdiag, jnp.concatenate([x2, x2], axis=0), 0.0).astype(BF16)
                ys.append(_nn(mmat[:, ql], wq))
            y = jnp.concatenate(ys, axis=1)
            prev = state_ref[g]
            y += _nn(cg16, prev.astype(BF16)) * jnp.exp(a_g)
            y += dsk_ref[:, gl] * xc_ref[rows, gl]
            xdd = (xdt_g * jnp.exp(a_last - a_g)).astype(BF16)
            state_ref[g] = prev * jnp.exp(a_last) + _nn(bg.T.astype(BF16), xdd)
            gz = y * _silu(z_ref[rows, gl])
            gz = gz * lax.rsqrt(jnp.mean(gz * gz, axis=1, keepdims=True) + RMS_EPS)
            o_ref[rows, gl] = (gz * nw_ref[:, gl]).astype(o_ref.dtype)


def _ssd(nat, conv_w, conv_b, dt_bias, a_log, d_skip, norm_w, bsz, seq, tt):
    m = bsz * seq
    nt = seq // tt
    row = lambda b, t: b * nt + t
    full = lambda a: pl.BlockSpec(a.shape, lambda b, t: (0, 0))
    expand = lambda v: jnp.repeat(v.astype(F32), SSD_P).reshape(1, SSD_WIDTH)
    dtb2 = _pad_lanes(jnp.concatenate([dt_bias, dt_bias]))
    j_i = lax.broadcasted_iota(I32, (128, SSD_WIDTH), 0)
    h_i = lax.broadcasted_iota(I32, (128, SSD_WIDTH), 1) // SSD_P
    e2 = ((j_i == h_i) | (j_i == h_i + SSD_HEADS)).astype(BF16)
    r_i = lax.broadcasted_iota(I32, (tt, tt), 0)
    c_i = lax.broadcasted_iota(I32, (tt, tt), 1)
    tril_bd = ((r_i >= c_i) & (r_i // CHUNK == c_i // CHUNK)).astype(BF16)
    return pl.pallas_call(
        functools.partial(_ssd_kernel, tt=tt),
        grid=(bsz, nt),
        in_specs=[pl.BlockSpec((tt, XBC_WIDTH), lambda b, t: (row(b, t), 0)),
                  pl.BlockSpec((tt, SSD_WIDTH), lambda b, t: (row(b, t), XBC_WIDTH // SSD_WIDTH)),
                  pl.BlockSpec((tt, 128), lambda b, t: (row(b, t), (XBC_WIDTH + SSD_WIDTH + ATTN_WIDTH) // 128)),
                  full(conv_w), full(conv_b), full(dtb2), pl.BlockSpec((1, SSD_WIDTH), lambda b, t: (0, 0)),
                  pl.BlockSpec((1, SSD_WIDTH), lambda b, t: (0, 0)), full(norm_w), full(e2), full(tril_bd)],
        out_specs=pl.BlockSpec((tt, SSD_WIDTH), lambda b, t: (row(b, t), 0)),
        out_shape=jax.ShapeDtypeStruct((m, SSD_WIDTH), BF16),
        scratch_shapes=[pltpu.VMEM((tt + 8, XBC_WIDTH), F32), pltpu.VMEM((8, XBC_WIDTH), F32),
                        pltpu.VMEM((SSD_GROUPS, SSD_N, SSD_WIDTH // SSD_GROUPS), F32),
                        pltpu.VMEM((tt, XBC_WIDTH), F32), pltpu.VMEM((tt, SSD_WIDTH), F32),
                        pltpu.VMEM((tt, SSD_WIDTH), F32),
                        pltpu.VMEM((tt, SSD_WIDTH), BF16), pltpu.VMEM((tt, SSD_WIDTH), BF16)],
        compiler_params=pltpu.CompilerParams(dimension_semantics=("arbitrary", "arbitrary"),
                                             vmem_limit_bytes=VMEM_LIMIT),
        name="ssd_mixer",
    )(nat, nat, nat, conv_w, conv_b, dtb2, expand(a_log), expand(d_skip), norm_w, e2, tril_bd)


def _out_kernel(oa_ref, os_ref, x_ref, wa_ref, ws_ref, g_ref, b_ref, o_ref, *, alpha):
    nsplit = 4
    rb = o_ref.shape[0] // nsplit
    for r in range(nsplit):
        rows = slice(r * rb, (r + 1) * rb)
        sub = _nn(oa_ref[rows, :], wa_ref[...]) + _nn(os_ref[rows, :], ws_ref[...])
        y = alpha * x_ref[rows, :] + sub
        mu = jnp.mean(y, axis=1, keepdims=True)
        yc = y - mu
        var = jnp.mean(yc * yc, axis=1, keepdims=True)
        o_ref[rows, :] = yc * lax.rsqrt(var + LN_EPS) * g_ref[...] + b_ref[...]


def _out_proj(oa, os_, x2, wa, ws, g, b, alpha, tm):
    m = x2.shape[0]
    full = lambda a: pl.BlockSpec(a.shape, lambda i: (0, 0))
    rowblk = lambda w: pl.BlockSpec((tm, w), lambda i: (i, 0))
    return pl.pallas_call(
        functools.partial(_out_kernel, alpha=alpha),
        grid=(m // tm,),
        in_specs=[rowblk(ATTN_WIDTH), rowblk(SSD_WIDTH), rowblk(D_MODEL), full(wa), full(ws), full(g), full(b)],
        out_specs=rowblk(D_MODEL),
        out_shape=jax.ShapeDtypeStruct((m, D_MODEL), F32),
        compiler_params=pltpu.CompilerParams(dimension_semantics=("arbitrary",), vmem_limit_bytes=VMEM_LIMIT),
        name="out_proj_ln",
    )(oa, os_, x2, wa, ws, g, b)


def _rope_tables(seq, rot):
    half = rot // 2
    inv = ROPE_THETA ** (-jnp.arange(half, dtype=F32) * 2.0 / rot)
    ang = inv[:, None] * jnp.arange(seq, dtype=F32)[None, :]
    return jnp.cos(ang), jnp.sin(ang)


def _pad_lanes(v, width=128):
    v = v.reshape(1, -1).astype(F32)
    return jnp.pad(v, ((0, 0), (0, width - v.shape[1])))


def _layer(h, w_in, w_out, conv_w, conv_b, dt_bias, a_log, d_skip, norm_w, ln_g, ln_b, alpha):
    bsz, seq, _ = h.shape
    m = bsz * seq
    assert seq % QB == 0 and seq % 512 == 0
    x2 = h.reshape(m, D_MODEL)

    o_q, o_k, o_v, o_za, o_qi, o_ki, o_wi, o_zs, o_xbc, o_dt = (
        0, 1024, 1280, 1536, 2560, 3584, 3648, 3664, 4688, 6736)
    w_all, layer = w_in
    wcol = lambda a, b: w_all[layer, :, a:b]
    wqkv_t = wcol(o_q, o_za).T.astype(BF16)
    widx_t = wcol(o_qi, o_zs).T.astype(BF16)
    wq = wqkv_t[:o_k]
    wkv = wqkv_t[o_k:]
    wqi = widx_t[:o_ki - o_qi]
    wkw = jnp.pad(widx_t[o_ki - o_qi:], ((0, 128 - (o_zs - o_ki)), (0, 0)))
    w_dt = wcol(o_dt, o_dt + SSD_HEADS)
    wn = jnp.concatenate([wcol(o_xbc, o_dt), wcol(o_zs, o_xbc), wcol(o_za, o_qi),
                          jnp.pad(jnp.concatenate([w_dt, w_dt], axis=1), ((0, 0), (0, 128 - 2 * SSD_HEADS)))],
                         axis=1).astype(BF16)

    ca, sa = _rope_tables(seq, HEAD_DIM // ROPE_DIV)
    ci, si = _rope_tables(seq, IDX_DIM // ROPE_DIV)

    qT, k, vT, qiT, kidx, wT, xb = _proj_t(x2, wq, wkv, wqi, wkw, ca, sa, ci, si, seq, tm=512)
    nat = _proj_n(xb, wn, tm=min(1024, m), tn=1408)

    o_attn = _attention(qT, qiT, wT, nat, (XBC_WIDTH + SSD_WIDTH) // ATTN_WIDTH, k, vT, kidx, bsz, seq)
    o_ssd = _ssd(nat, conv_w, conv_b.reshape(1, -1), dt_bias, a_log, d_skip, norm_w.reshape(1, -1),
                 bsz, seq, tt=256)

    wo_all, _ = w_out
    out = _out_proj(o_attn, o_ssd, x2, wo_all[layer, :ATTN_WIDTH].astype(BF16), wo_all[layer, ATTN_WIDTH:].astype(BF16),
                    ln_g.reshape(1, -1), ln_b.reshape(1, -1), alpha, tm=512)
    return out.reshape(bsz, seq, D_MODEL)


def kernel(x, w_in, w_out, conv_w, conv_b, dt_bias, a_log, d_skip, ssd_norm_w, ln_g, ln_b):
    depth = w_in.shape[0]
    alpha = (2.0 * depth) ** 0.25
    h = x
    for layer in range(depth):
        h = _layer(h, (w_in, layer), (w_out, layer), conv_w[layer], conv_b[layer], dt_bias[layer],
                   a_log[layer], d_skip[layer], ssd_norm_w[layer], ln_g[layer], ln_b[layer], alpha)
    return h
```

```python
import functools
import math

import jax
import jax.numpy as jnp
from jax import lax
from jax.experimental import pallas as pl
from jax.experimental.pallas import tpu as pltpu

F32 = jnp.float32
BF16 = jnp.bfloat16
I32 = jnp.int32

D_MODEL = 2048
CHUNK = 64
ATTN_WIDTH = 1024
SSD_WIDTH = 1024
HEAD_DIM = 128
Q_HEADS = 8
KV_HEADS = 2
ROPE_THETA = 500000.0
ROPE_DIV = 4
IDX_HEADS = 16
IDX_DIM = 64
TOPK_MAX = 256
SSD_P = 64
SSD_HEADS = 16
SSD_GROUPS = 4
SSD_N = 128
SSD_CONV = 4
XBC_WIDTH = SSD_WIDTH + 2 * SSD_GROUPS * SSD_N
LN_EPS = 1e-5
RMS_EPS = 1e-5

VMEM_LIMIT = 56 * 1024 * 1024
INT_MIN = -(2 ** 31)

QB = 256
KT = 128
KTA = 256
VPAD = 16
VROWS = HEAD_DIM + VPAD


def _nt(a, b):
    return lax.dot_general(a, b, (((1,), (1,)), ((), ())), preferred_element_type=F32)


def _nn(a, b):
    return jnp.dot(a, b, preferred_element_type=F32)


def _silu(v):
    return v * (1.0 / (1.0 + jnp.exp(-v)))


def _rope_rows(t, cos, sin, nheads, hd, half):
    pieces = []
    for h in range(nheads):
        b = h * hd
        x1 = t[b:b + half]
        x2 = t[b + half:b + 2 * half]
        pieces += [x1 * cos - x2 * sin, x2 * cos + x1 * sin, t[b + 2 * half:b + hd]]
    return jnp.concatenate(pieces, axis=0)


def _proj_t_kernel(x_ref, wq_ref, wkv_ref, wqi_ref, wkw_ref, ca_ref, sa_ref, ci_ref, si_ref,
                   q_ref, k_ref, v_ref, qi_ref, kidx_ref, w_ref, xb_ref):
    x = x_ref[...].astype(BF16)
    xb_ref[...] = x
    ca, sa, ci, si = ca_ref[...], sa_ref[...], ci_ref[...], si_ref[...]
    qscale = HEAD_DIM ** -0.5 * math.log2(math.e)
    tq = _nt(wq_ref[...], x)
    q_ref[...] = (_rope_rows(tq, ca, sa, Q_HEADS, HEAD_DIM, 16) * qscale).astype(BF16)
    tkv = _nt(wkv_ref[...], x)
    k_ref[...] = _rope_rows(tkv[:KV_HEADS * HEAD_DIM], ca, sa, KV_HEADS, HEAD_DIM, 16).T.astype(BF16)
    ones = jnp.ones((VPAD, KTA), BF16)
    for jj in range(v_ref.shape[0]):
        for g in range(KV_HEADS):
            vg = tkv[(KV_HEADS + g) * HEAD_DIM:(KV_HEADS + g + 1) * HEAD_DIM, jj * KTA:(jj + 1) * KTA]
            v_ref[jj, g * VROWS:(g + 1) * VROWS, :] = jnp.concatenate([vg.astype(BF16), ones], axis=0)
    tqi = _nt(wqi_ref[...], x)
    qi_ref[...] = _rope_rows(tqi, ci, si, IDX_HEADS, IDX_DIM, 8).astype(BF16)
    tkw = _nt(wkw_ref[...], x)
    kidx_ref[...] = _rope_rows(tkw[:IDX_DIM], ci, si, 1, IDX_DIM, 8).T.astype(BF16)
    w_ref[...] = tkw[IDX_DIM:IDX_DIM + IDX_HEADS] * (IDX_HEADS ** -0.5 * IDX_DIM ** -0.5)


def _proj_t(x2, wq, wkv, wqi, wkw, ca, sa, ci, si, seq, tm):
    m, kdim = x2.shape
    nper = seq // tm
    full = lambda a: pl.BlockSpec(a.shape, lambda i: (0, 0))
    tab = lambda a: pl.BlockSpec((a.shape[0], tm), lambda i: (0, i % nper))
    cols = lambda r: pl.BlockSpec((r, tm), lambda i: (0, i))
    rows = lambda c: pl.BlockSpec((tm, c), lambda i: (i, 0))
    vspec = pl.BlockSpec((tm // KTA, KV_HEADS * VROWS, KTA), lambda i: (i, 0, 0))
    sds = jax.ShapeDtypeStruct
    return pl.pallas_call(
        _proj_t_kernel,
        grid=(m // tm,),
        in_specs=[pl.BlockSpec((tm, kdim), lambda i: (i, 0)), full(wq), full(wkv), full(wqi), full(wkw),
                  tab(ca), tab(sa), tab(ci), tab(si)],
        out_specs=[cols(1024), rows(256), vspec, cols(1024), rows(IDX_DIM), cols(IDX_HEADS), rows(kdim)],
        out_shape=[sds((1024, m), BF16), sds((m, 256), BF16), sds((m // KTA, KV_HEADS * VROWS, KTA), BF16),
                   sds((1024, m), BF16), sds((m, IDX_DIM), BF16), sds((IDX_HEADS, m), F32), sds((m, kdim), BF16)],
        compiler_params=pltpu.CompilerParams(dimension_semantics=("arbitrary",), vmem_limit_bytes=VMEM_LIMIT),
        name="proj_t",
    )(x2, wq, wkv, wqi, wkw, ca, sa, ci, si)


def _proj_n_kernel(x_ref, w_ref, o_ref):
    o_ref[...] = _nn(x_ref[...], w_ref[...]).astype(o_ref.dtype)


def _proj_n(xb, w, tm, tn):
    m, kdim = xb.shape
    n = w.shape[1]
    return pl.pallas_call(
        _proj_n_kernel,
        grid=(n // tn, m // tm),
        in_specs=[pl.BlockSpec((tm, kdim), lambda j, i: (i, 0)), pl.BlockSpec((kdim, tn), lambda j, i: (0, j))],
        out_specs=pl.BlockSpec((tm, tn), lambda j, i: (i, j)),
        out_shape=jax.ShapeDtypeStruct((m, n), F32),
        compiler_params=pltpu.CompilerParams(dimension_semantics=("arbitrary", "arbitrary"),
                                             vmem_limit_bytes=VMEM_LIMIT),
        name="proj_n",
    )(xb, w)


def _fold64(x, op2):
    parts = [x[a * 64:(a + 1) * 64] for a in range(x.shape[0] // 64)]
    while len(parts) > 1:
        parts = [op2(parts[2 * a], parts[2 * a + 1]) for a in range(len(parts) // 2)]
    return parts[0]


def _fold8_tree(x):
    parts = [x[a * 8:(a + 1) * 8] for a in range(x.shape[0] // 8)]
    while len(parts) > 1:
        parts = [parts[2 * a] + parts[2 * a + 1] for a in range(len(parts) // 2)]
    return parts[0]


def _bit_transpose32(words):
    a = list(words)
    mask, j = 0x0000FFFF, 16
    while j:
        k = 0
        while k < 32:
            t = (a[k] ^ (a[k + j] >> j)) & mask
            a[k] = a[k] ^ t
            a[k + j] = a[k + j] ^ (t << j)
            k = (k + j + 1) & ~j
        j >>= 1
        mask = (mask ^ (mask << j)) & 0xFFFFFFFF
    return a


def _attn_kernel(qT_ref, qiT_ref, wT_ref, z_ref, k_ref, vT_ref, kidx_ref, o_ref,
                 key_ref, plane_ref, bias_ref, m_ref, l_ref, acc_ref, s_ref, mt_ref, *, topk):
    i = pl.program_id(1)
    nfull = 2 * i
    q_chunk = (i * QB + lax.broadcasted_iota(I32, (1, QB), 1)) // CHUNK

    def score_tile(j):
        rows = pl.ds(pl.multiple_of(j * KT, KT), KT)
        kt = kidx_ref[rows, :]
        acc = jnp.zeros((KT, QB), F32)
        for h in range(IDX_HEADS):
            qh = qiT_ref[h * IDX_DIM:(h + 1) * IDX_DIM, :]
            acc += jnp.maximum(_nn(kt, qh), 0.0) * wT_ref[h:h + 1, :]
        bits = pltpu.bitcast(acc, I32)
        return rows, bits ^ ((bits >> 31) & 0x7FFFFFFF)

    def store_planes(t256, keys):
        words = []
        for kk in keys:
            u = kk ^ INT_MIN
            words += [u[8 * a:8 * a + 8] for a in range(kk.shape[0] // 8)]
        planes = _bit_transpose32(words[::-1])
        prow = pl.ds(pl.multiple_of(t256 * 8, 8), 8)
        for b in range(32):
            plane_ref[b, prow, :] = planes[b]

    def planes_from_keys(t256):
        rows = pl.ds(pl.multiple_of(t256 * 2 * KT, 2 * KT), 2 * KT)
        store_planes(t256, [key_ref[rows, :]])

    @pl.when((pl.program_id(0) == 0) & (i == 0))
    def _():
        key_ref[...] = jnp.zeros(key_ref.shape, I32)

    def full_tiles(j2, c):
        planes_from_keys(jnp.maximum(j2 - 1, 0))
        for jj in range(2):
            rows, key = score_tile(2 * j2 + jj)
            key_ref[rows, :] = key
        return c

    lax.fori_loop(0, i, full_tiles, 0)
    planes_from_keys(jnp.maximum(i - 1, 0))
    keys = []
    for jj in range(QB // KT):
        j = nfull + jj
        rows, key = score_tile(j)
        k_chunk = (j * KT + lax.broadcasted_iota(I32, (KT, 1), 0)) // CHUNK
        key = jnp.where(k_chunk <= q_chunk, key, INT_MIN)
        key_ref[rows, :] = key
        keys.append(key)
    store_planes(i, keys)

    nprow = plane_ref.shape[1]
    t_row = lax.broadcasted_iota(I32, (nprow, QB), 0) >> 3
    qc_local = lax.broadcasted_iota(I32, (1, QB), 1) // CHUNK
    diag_bits = lax.shift_right_logical(jnp.full((1, QB), -1, I32), 32 - 8 * (qc_local + 1))
    act0 = jnp.where(t_row < i, -1, jnp.where(t_row == i, diag_bits, 0))

    def radix_select(nrow):
        act = act0[:nrow]
        above = jnp.zeros((1, QB), I32)
        thr_u = jnp.zeros((1, QB), I32)
        for b in range(32):
            ones = act & plane_ref[b, 0:nrow, :]
            c8 = _fold8_tree(lax.population_count(ones))
            c = above + jnp.sum(c8, axis=0, keepdims=True)
            take = c >= topk
            act = jnp.where(take, ones, act ^ ones)
            above = jnp.where(take, above, c)
            thr_u = thr_u | jnp.where(take, jnp.int32(-(2 ** 31) if b == 0 else 1 << (31 - b)), 0)
        if nrow < nprow:
            act = jnp.concatenate([act, jnp.zeros((nprow - nrow, QB), I32)], axis=0)
        return thr_u, above, act

    half = nprow // 2
    thr_u, above, act = lax.cond((i + 1) * 8 <= half, lambda: radix_select(half), lambda: radix_select(nprow))
    thr = jnp.maximum(thr_u ^ INT_MIN, INT_MIN + 1)
    need = topk - above
    n_tied = jnp.sum(_fold8_tree(lax.population_count(act)), axis=0, keepdims=True)
    any_tie = jnp.max(jnp.where(n_tied > need, 1, 0)) > 0

    @pl.when(jnp.logical_not(any_tie))
    def _():
        def to_bias(j, c):
            rows = pl.ds(pl.multiple_of(j * KTA, KTA), KTA)
            bias_ref[rows, :] = jnp.where(key_ref[rows, :] >= thr, 0.0, -jnp.inf)
            return c

        lax.fori_loop(0, i + 1, to_bias, 0)

    @pl.when(any_tie)
    def _():
        p_row = lax.broadcasted_iota(I32, (nprow, QB), 0)
        word_bit_clear = {7: 0x0000FFFF, 6: 0x00FF00FF, 5: 0x0F0F0F0F, 4: 0x33333333, 3: 0x55555555}
        cand, left, idx_thr = act, need, jnp.zeros((1, QB), I32)
        for b in range((bias_ref.shape[0] - 1).bit_length() - 1, -1, -1):
            if b >= 8:
                zmask = jnp.where(((p_row >> (3 + b - 8)) & 1) == 0, -1, 0)
            elif b >= 3:
                zmask = word_bit_clear[b]
            else:
                zmask = jnp.where(((p_row >> b) & 1) == 0, -1, 0)
            zeros = cand & zmask
            c0 = jnp.sum(_fold8_tree(lax.population_count(zeros)), axis=0, keepdims=True)
            low = c0 >= left
            cand = jnp.where(low, zeros, cand ^ zeros)
            left = jnp.where(low, left, left - c0)
            idx_thr = idx_thr | jnp.where(low, 0, 1 << b)

        def to_bias_tied(j, c):
            rows = pl.ds(pl.multiple_of(j * KTA, KTA), KTA)
            key = key_ref[rows, :]
            kidx_abs = j * KTA + lax.broadcasted_iota(I32, (KTA, 1), 0)
            keep = (key > thr) | ((key == thr) & (kidx_abs <= idx_thr))
            bias_ref[rows, :] = jnp.where(keep, 0.0, -jnp.inf)
            return c

        lax.fori_loop(0, i + 1, to_bias_tied, 0)

    m_ref[...] = jnp.full(m_ref.shape, -jnp.inf, F32)
    l_ref[...] = jnp.zeros(l_ref.shape, F32)
    acc_ref[...] = jnp.zeros(acc_ref.shape, F32)

    rep = Q_HEADS // KV_HEADS
    gq = rep * QB
    ntile = bias_ref.shape[0] // KTA
    jpad = jnp.minimum(i + 1, ntile - 1)

    @pl.when(i + 1 < ntile)
    def _():
        bias_ref[pl.ds(pl.multiple_of((i + 1) * KTA, KTA), KTA), :] = jnp.full((KTA, QB), -jnp.inf, F32)

    def masked_logits(j, slot):
        jc = jnp.minimum(j, jpad)
        rows = pl.ds(pl.multiple_of(jc * KTA, KTA), KTA)
        for g in range(KV_HEADS):
            qg = jnp.concatenate([qT_ref[(g * rep + r) * HEAD_DIM:(g * rep + r + 1) * HEAD_DIM, :]
                                  for r in range(rep)], axis=1)
            sg = _nn(k_ref[rows, g * HEAD_DIM:(g + 1) * HEAD_DIM], qg)
            for r in range(rep):
                h = g * rep + r
                sh = sg[:, r * QB:(r + 1) * QB] + bias_ref[rows, :]
                s_ref[slot, :, h * QB:(h + 1) * QB] = sh
                mt_ref[slot, h:h + 1, :] = jnp.max(_fold64(sh, jnp.maximum), axis=0, keepdims=True)

    def softmax_pv(j, slot):
        jc = jnp.minimum(j, jpad)
        for h in range(Q_HEADS):
            g, r = divmod(h, rep)
            hl = slice(r * QB, (r + 1) * QB)
            s = s_ref[slot, :, h * QB:(h + 1) * QB]
            m = m_ref[g:g + 1, hl]
            m_new = jnp.maximum(m, mt_ref[slot, h:h + 1, :])
            m_safe = jnp.where(m_new == -jnp.inf, 0.0, m_new)
            alpha = jnp.exp2(m - m_safe)
            p = jnp.exp2(s - m_safe)
            pv = _nn(vT_ref[jc, g * VROWS:(g + 1) * VROWS, :], p.astype(BF16))
            l_ref[g:g + 1, hl] = alpha * l_ref[g:g + 1, hl] + pv[HEAD_DIM:HEAD_DIM + 1, :]
            acc_ref[g, :, hl] = alpha * acc_ref[g, :, hl] + pv[:HEAD_DIM]
            m_ref[g:g + 1, hl] = m_new

    masked_logits(0, 0)

    def kv_step(jj, c):
        a = 2 * jj
        masked_logits(a + 1, 1)
        softmax_pv(a, 0)
        masked_logits(a + 2, 0)
        softmax_pv(a + 1, 1)
        return c

    lax.fori_loop(0, (i + 2) // 2, kv_step, 0)
    for h in range(Q_HEADS):
        g, r = divmod(h, rep)
        inv_l = 1.0 / l_ref[g:g + 1, r * QB:(r + 1) * QB]
        oh = (acc_ref[g, :, r * QB:(r + 1) * QB] * inv_l).T
        zh = z_ref[:, h * HEAD_DIM:(h + 1) * HEAD_DIM]
        o_ref[:, h * HEAD_DIM:(h + 1) * HEAD_DIM] = (oh * _silu(zh)).astype(o_ref.dtype)


def _attention(qT, qiT, wT, znat, zcol, k, vT, kidx, bsz, seq):
    nq = seq // QB
    topk = min(TOPK_MAX, seq // 4)
    m = bsz * seq
    step = lambda r: pl.BlockSpec((r, QB), lambda b, i: (0, b * nq + i))
    return pl.pallas_call(
        functools.partial(_attn_kernel, topk=topk),
        grid=(bsz, nq),
        in_specs=[step(1024), step(1024), step(IDX_HEADS),
                  pl.BlockSpec((QB, ATTN_WIDTH), lambda b, i: (b * nq + i, zcol)),
                  pl.BlockSpec((seq, 256), lambda b, i: (b, 0)),
                  pl.BlockSpec((seq // KTA, KV_HEADS * VROWS, KTA), lambda b, i: (b, 0, 0)),
                  pl.BlockSpec((seq, IDX_DIM), lambda b, i: (b, 0))],
        out_specs=pl.BlockSpec((QB, ATTN_WIDTH), lambda b, i: (b * nq + i, 0)),
        out_shape=jax.ShapeDtypeStruct((m, ATTN_WIDTH), BF16),
        scratch_shapes=[pltpu.VMEM((seq, QB), I32), pltpu.VMEM((32, seq // 32, QB), I32),
                        pltpu.VMEM((seq, QB), F32),
                        pltpu.VMEM((KV_HEADS, Q_HEADS // KV_HEADS * QB), F32),
                        pltpu.VMEM((KV_HEADS, Q_HEADS // KV_HEADS * QB), F32),
                        pltpu.VMEM((KV_HEADS, HEAD_DIM, Q_HEADS // KV_HEADS * QB), F32),
                        pltpu.VMEM((2, KTA, Q_HEADS * QB), F32), pltpu.VMEM((2, Q_HEADS, QB), F32)],
        compiler_params=pltpu.CompilerParams(dimension_semantics=("arbitrary", "arbitrary"),
                                             vmem_limit_bytes=VMEM_LIMIT),
        name="dsa_attention",
    )(qT, qiT, wT, znat, k, vT, kidx)


def _ssd_kernel(xbc_ref, z_ref, dt_ref, cw_ref, cb_ref, dtb_ref, a_ref, dsk_ref, nw_ref, e2_ref, tril_ref, o_ref,
                buf_ref, tail_ref, state_ref, xc_ref, xdt_ref, acum_ref, ahi_ref, alo_ref, *, tt):
    t = pl.program_id(1)

    @pl.when(t == 0)
    def _():
        tail_ref[...] = jnp.zeros_like(tail_ref)
        state_ref[...] = jnp.zeros_like(state_ref)

    buf_ref[0:8, :] = tail_ref[...]
    buf_ref[8:8 + tt, :] = xbc_ref[...]
    tail_ref[...] = xbc_ref[tt - 8:tt, :]

    dtv = dt_ref[...] + dtb_ref[...]
    dtv = jnp.maximum(dtv, 0.0) + jnp.log(1.0 + jnp.exp(-jnp.abs(dtv)))
    dhi = dtv.astype(BF16).astype(F32)
    lane = lax.broadcasted_iota(I32, (1, 128), 1)
    dsplit = jnp.where(lane < SSD_HEADS, dhi, dtv - dhi).astype(BF16)
    acum_ref[...] = _nn(dsplit, e2_ref[...])
    neg_a = -jnp.exp(a_ref[...])

    for c in range(tt // CHUNK):
        r0 = c * CHUNK
        conv = cb_ref[...] + cw_ref[3:4, :] * buf_ref[8 + r0:8 + r0 + CHUNK, :]
        for jtap in range(SSD_CONV - 1):
            conv += cw_ref[jtap:jtap + 1, :] * buf_ref[5 + jtap + r0:5 + jtap + r0 + CHUNK, :]
        xc = _silu(conv)
        xc_ref[r0:r0 + CHUNK, :] = xc
        dt_e = acum_ref[r0:r0 + CHUNK, :]
        xdt_ref[r0:r0 + CHUNK, :] = xc[:, :SSD_WIDTH] * dt_e
        dta_e = dt_e * neg_a
        ahi = dta_e.astype(BF16)
        ahi_ref[r0:r0 + CHUNK, :] = ahi
        alo_ref[r0:r0 + CHUNK, :] = (dta_e - ahi.astype(F32)).astype(BF16)
    acum_ref[...] = _nn(tril_ref[...], ahi_ref[...]) + _nn(tril_ref[...], alo_ref[...])

    gw = SSD_WIDTH // SSD_GROUPS
    s_row = lax.broadcasted_iota(I32, (CHUNK, gw), 0)
    s_lane = lax.broadcasted_iota(I32, (CHUNK, gw), 1) & (CHUNK - 1)
    diag = s_row == s_lane
    tril = s_row >= s_lane
    bd_r = lax.broadcasted_iota(I32, (2 * SSD_P, 2 * SSD_P), 0) >> 6
    bd_c = lax.broadcasted_iota(I32, (2 * SSD_P, 2 * SSD_P), 1) >> 6
    blockdiag = bd_r == bd_c

    for c in range(tt // CHUNK):
        rows = slice(c * CHUNK, (c + 1) * CHUNK)
        for g in range(SSD_GROUPS):
            gl = slice(g * gw, (g + 1) * gw)
            bg = xc_ref[rows, SSD_WIDTH + g * SSD_N:SSD_WIDTH + (g + 1) * SSD_N]
            cg = xc_ref[rows, SSD_WIDTH + (SSD_GROUPS + g) * SSD_N:SSD_WIDTH + (SSD_GROUPS + g + 1) * SSD_N]
            bg16, cg16 = bg.astype(BF16), cg.astype(BF16)
            a_g = acum_ref[rows, gl]
            a_last = a_g[CHUNK - 1:CHUNK, :]
            xdt_g = xdt_ref[rows, gl]
            cb = _nt(cg16, jnp.concatenate([bg16] * (gw // CHUNK), axis=0))
            a_row = jnp.sum(jnp.where(diag, a_g, 0.0), axis=0, keepdims=True)
            lmat = jnp.exp(jnp.where(tril, a_g - a_row, -jnp.inf))
            mmat = (cb * lmat).astype(BF16)
            ys = []
            for q in range(gw // (2 * SSD_P)):
                ql = slice(q * 2 * SSD_P, (q + 1) * 2 * SSD_P)
                x2 = xdt_g[:, ql]
                wq = jnp.where(blockdiag, jnp.concatenate([x2, x2], axis=0), 0.0).astype(BF16)
                ys.append(_nn(mmat[:, ql], wq))
            y = jnp.concatenate(ys, axis=1)
            prev = state_ref[g]
            y += _nn(cg16, prev.astype(BF16)) * jnp.exp(a_g)
            y += dsk_ref[:, gl] * xc_ref[rows, gl]
            xdd = (xdt_g * jnp.exp(a_last - a_g)).astype(BF16)
            state_ref[g] = prev * jnp.exp(a_last) + _nn(bg.T.astype(BF16), xdd)
            gz = y * _silu(z_ref[rows, gl])
            gz = gz * lax.rsqrt(jnp.mean(gz * gz, axis=1, keepdims=True) + RMS_EPS)
            o_ref[rows, gl] = (gz * nw_ref[:, gl]).astype(o_ref.dtype)


def _ssd(nat, conv_w, conv_b, dt_bias, a_log, d_skip, norm_w, bsz, seq, tt):
    m = bsz * seq
    nt = seq // tt
    row = lambda b, t: b * nt + t
    full = lambda a: pl.BlockSpec(a.shape, lambda b, t: (0, 0))
    expand = lambda v: jnp.repeat(v.astype(F32), SSD_P).reshape(1, SSD_WIDTH)
    dtb2 = _pad_lanes(jnp.concatenate([dt_bias, dt_bias]))
    j_i = lax.broadcasted_iota(I32, (128, SSD_WIDTH), 0)
    h_i = lax.broadcasted_iota(I32, (128, SSD_WIDTH), 1) // SSD_P
    e2 = ((j_i == h_i) | (j_i == h_i + SSD_HEADS)).astype(BF16)
    r_i = lax.broadcasted_iota(I32, (tt, tt), 0)
    c_i = lax.broadcasted_iota(I32, (tt, tt), 1)
    tril_bd = ((r_i >= c_i) & (r_i // CHUNK == c_i // CHUNK)).astype(BF16)
    return pl.pallas_call(
        functools.partial(_ssd_kernel, tt=tt),
        grid=(bsz, nt),
        in_specs=[pl.BlockSpec((tt, XBC_WIDTH), lambda b, t: (row(b, t), 0)),
                  pl.BlockSpec((tt, SSD_WIDTH), lambda b, t: (row(b, t), XBC_WIDTH // SSD_WIDTH)),
                  pl.BlockSpec((tt, 128), lambda b, t: (row(b, t), (XBC_WIDTH + SSD_WIDTH + ATTN_WIDTH) // 128)),
                  full(conv_w), full(conv_b), full(dtb2), pl.BlockSpec((1, SSD_WIDTH), lambda b, t: (0, 0)),
                  pl.BlockSpec((1, SSD_WIDTH), lambda b, t: (0, 0)), full(norm_w), full(e2), full(tril_bd)],
        out_specs=pl.BlockSpec((tt, SSD_WIDTH), lambda b, t: (row(b, t), 0)),
        out_shape=jax.ShapeDtypeStruct((m, SSD_WIDTH), BF16),
        scratch_shapes=[pltpu.VMEM((tt + 8, XBC_WIDTH), F32), pltpu.VMEM((8, XBC_WIDTH), F32),
                        pltpu.VMEM((SSD_GROUPS, SSD_N, SSD_WIDTH // SSD_GROUPS), F32),
                        pltpu.VMEM((tt, XBC_WIDTH), F32), pltpu.VMEM((tt, SSD_WIDTH), F32),
                        pltpu.VMEM((tt, SSD_WIDTH), F32),
                        pltpu.VMEM((tt, SSD_WIDTH), BF16), pltpu.VMEM((tt, SSD_WIDTH), BF16)],
        compiler_params=pltpu.CompilerParams(dimension_semantics=("arbitrary", "arbitrary"),
                                             vmem_limit_bytes=VMEM_LIMIT),
        name="ssd_mixer",
    )(nat, nat, nat, conv_w, conv_b, dtb2, expand(a_log), expand(d_skip), norm_w, e2, tril_bd)


def _out_kernel(oa_ref, os_ref, x_ref, wa_ref, ws_ref, g_ref, b_ref, o_ref, *, alpha):
    nsplit = 4
    rb = o_ref.shape[0] // nsplit
    for r in range(nsplit):
        rows = slice(r * rb, (r + 1) * rb)
        sub = _nn(oa_ref[rows, :], wa_ref[...]) + _nn(os_ref[rows, :], ws_ref[...])
        y = alpha * x_ref[rows, :] + sub
        mu = jnp.mean(y, axis=1, keepdims=True)
        yc = y - mu
        var = jnp.mean(yc * yc, axis=1, keepdims=True)
        o_ref[rows, :] = yc * lax.rsqrt(var + LN_EPS) * g_ref[...] + b_ref[...]


def _out_proj(oa, os_, x2, wa, ws, g, b, alpha, tm):
    m = x2.shape[0]
    full = lambda a: pl.BlockSpec(a.shape, lambda i: (0, 0))
    rowblk = lambda w: pl.BlockSpec((tm, w), lambda i: (i, 0))
    return pl.pallas_call(
        functools.partial(_out_kernel, alpha=alpha),
        grid=(m // tm,),
        in_specs=[rowblk(ATTN_WIDTH), rowblk(SSD_WIDTH), rowblk(D_MODEL), full(wa), full(ws), full(g), full(b)],
        out_specs=rowblk(D_MODEL),
        out_shape=jax.ShapeDtypeStruct((m, D_MODEL), F32),
        compiler_params=pltpu.CompilerParams(dimension_semantics=("arbitrary",), vmem_limit_bytes=VMEM_LIMIT),
        name="out_proj_ln",
    )(oa, os_, x2, wa, ws, g, b)


def _rope_tables(seq, rot):
    half = rot // 2
    inv = ROPE_THETA ** (-jnp.arange(half, dtype=F32) * 2.0 / rot)
    ang = inv[:, None] * jnp.arange(seq, dtype=F32)[None, :]
    return jnp.cos(ang), jnp.sin(ang)


def _pad_lanes(v, width=128):
    v = v.reshape(1, -1).astype(F32)
    return jnp.pad(v, ((0, 0), (0, width - v.shape[1])))


def _layer(h, w_in, w_out, conv_w, conv_b, dt_bias, a_log, d_skip, norm_w, ln_g, ln_b, alpha):
    bsz, seq, _ = h.shape
    m = bsz * seq
    assert seq % QB == 0 and seq % 512 == 0
    x2 = h.reshape(m, D_MODEL)

    o_q, o_k, o_v, o_za, o_qi, o_ki, o_wi, o_zs, o_xbc, o_dt = (
        0, 1024, 1280, 1536, 2560, 3584, 3648, 3664, 4688, 6736)
    w_all, layer = w_in
    wcol = lambda a, b: w_all[layer, :, a:b]
    wqkv_t = wcol(o_q, o_za).T.astype(BF16)
    widx_t = wcol(o_qi, o_zs).T.astype(BF16)
    wq = wqkv_t[:o_k]
    wkv = wqkv_t[o_k:]
    wqi = widx_t[:o_ki - o_qi]
    wkw = jnp.pad(widx_t[o_ki - o_qi:], ((0, 128 - (o_zs - o_ki)), (0, 0)))
    w_dt = wcol(o_dt, o_dt + SSD_HEADS)
    wn = jnp.concatenate([wcol(o_xbc, o_dt), wcol(o_zs, o_xbc), wcol(o_za, o_qi),
                          jnp.pad(jnp.concatenate([w_dt, w_dt], axis=1), ((0, 0), (0, 128 - 2 * SSD_HEADS)))],
                         axis=1).astype(BF16)

    ca, sa = _rope_tables(seq, HEAD_DIM // ROPE_DIV)
    ci, si = _rope_tables(seq, IDX_DIM // ROPE_DIV)

    qT, k, vT, qiT, kidx, wT, xb = _proj_t(x2, wq, wkv, wqi, wkw, ca, sa, ci, si, seq, tm=512)
    nat = _proj_n(xb, wn, tm=min(1024, m), tn=1408)

    o_attn = _attention(qT, qiT, wT, nat, (XBC_WIDTH + SSD_WIDTH) // ATTN_WIDTH, k, vT, kidx, bsz, seq)
    o_ssd = _ssd(nat, conv_w, conv_b.reshape(1, -1), dt_bias, a_log, d_skip, norm_w.reshape(1, -1),
                 bsz, seq, tt=256)

    wo_all, _ = w_out
    out = _out_proj(o_attn, o_ssd, x2, wo_all[layer, :ATTN_WIDTH].astype(BF16), wo_all[layer, ATTN_WIDTH:].astype(BF16),
                    ln_g.reshape(1, -1), ln_b.reshape(1, -1), alpha, tm=512)
    return out.reshape(bsz, seq, D_MODEL)


def kernel(x, w_in, w_out, conv_w, conv_b, dt_bias, a_log, d_skip, ssd_norm_w, ln_g, ln_b):
    depth = w_in.shape[0]
    alpha = (2.0 * depth) ** 0.25
    h = x
    for layer in range(depth):
        h = _layer(h, (w_in, layer), (w_out, layer), conv_w[layer], conv_b[layer], dt_bias[layer],
                   a_log[layer], d_skip[layer], ssd_norm_w[layer], ln_g[layer], ln_b[layer], alpha)
    return h
```

```python
import functools
import math

import jax
import jax.numpy as jnp
from jax import lax
from jax.experimental import pallas as pl
from jax.experimental.pallas import tpu as pltpu

F32 = jnp.float32
BF16 = jnp.bfloat16
I32 = jnp.int32

D_MODEL = 2048
CHUNK = 64
ATTN_WIDTH = 1024
SSD_WIDTH = 1024
HEAD_DIM = 128
Q_HEADS = 8
KV_HEADS = 2
ROPE_THETA = 500000.0
ROPE_DIV = 4
IDX_HEADS = 16
IDX_DIM = 64
TOPK_MAX = 256
SSD_P = 64
SSD_HEADS = 16
SSD_GROUPS = 4
SSD_N = 128
SSD_CONV = 4
XBC_WIDTH = SSD_WIDTH + 2 * SSD_GROUPS * SSD_N
LN_EPS = 1e-5
RMS_EPS = 1e-5

VMEM_LIMIT = 56 * 1024 * 1024
INT_MIN = -(2 ** 31)

QB = 256
KT = 128
KTA = 256
VPAD = 16
VROWS = HEAD_DIM + VPAD


def _nt(a, b):
    return lax.dot_general(a, b, (((1,), (1,)), ((), ())), preferred_element_type=F32)


def _nn(a, b):
    return jnp.dot(a, b, preferred_element_type=F32)


def _silu(v):
    return v * (1.0 / (1.0 + jnp.exp(-v)))


def _rope_rows(t, cos, sin, nheads, hd, half):
    pieces = []
    for h in range(nheads):
        b = h * hd
        x1 = t[b:b + half]
        x2 = t[b + half:b + 2 * half]
        pieces += [x1 * cos - x2 * sin, x2 * cos + x1 * sin, t[b + 2 * half:b + hd]]
    return jnp.concatenate(pieces, axis=0)


def _proj_t_kernel(x_ref, wq_ref, wkv_ref, wqi_ref, wkw_ref, ca_ref, sa_ref, ci_ref, si_ref,
                   q_ref, k_ref, v_ref, qi_ref, kidx_ref, w_ref, xb_ref):
    x = x_ref[...].astype(BF16)
    xb_ref[...] = x
    ca, sa, ci, si = ca_ref[...], sa_ref[...], ci_ref[...], si_ref[...]
    qscale = HEAD_DIM ** -0.5 * math.log2(math.e)
    tq = _nt(wq_ref[...], x)
    q_ref[...] = (_rope_rows(tq, ca, sa, Q_HEADS, HEAD_DIM, 16) * qscale).astype(BF16)
    tkv = _nt(wkv_ref[...], x)
    k_ref[...] = _rope_rows(tkv[:KV_HEADS * HEAD_DIM], ca, sa, KV_HEADS, HEAD_DIM, 16).T.astype(BF16)
    ones = jnp.ones((VPAD, KTA), BF16)
    for jj in range(v_ref.shape[0]):
        for g in range(KV_HEADS):
            vg = tkv[(KV_HEADS + g) * HEAD_DIM:(KV_HEADS + g + 1) * HEAD_DIM, jj * KTA:(jj + 1) * KTA]
            v_ref[jj, g * VROWS:(g + 1) * VROWS, :] = jnp.concatenate([vg.astype(BF16), ones], axis=0)
    tqi = _nt(wqi_ref[...], x)
    qi_ref[...] = _rope_rows(tqi, ci, si, IDX_HEADS, IDX_DIM, 8).astype(BF16)
    tkw = _nt(wkw_ref[...], x)
    kidx_ref[...] = _rope_rows(tkw[:IDX_DIM], ci, si, 1, IDX_DIM, 8).T.astype(BF16)
    w_ref[...] = tkw[IDX_DIM:IDX_DIM + IDX_HEADS] * (IDX_HEADS ** -0.5 * IDX_DIM ** -0.5)


def _proj_t(x2, wq, wkv, wqi, wkw, ca, sa, ci, si, seq, tm):
    m, kdim = x2.shape
    nper = seq // tm
    full = lambda a: pl.BlockSpec(a.shape, lambda i: (0, 0))
    tab = lambda a: pl.BlockSpec((a.shape[0], tm), lambda i: (0, i % nper))
    cols = lambda r: pl.BlockSpec((r, tm), lambda i: (0, i))
    rows = lambda c: pl.BlockSpec((tm, c), lambda i: (i, 0))
    vspec = pl.BlockSpec((tm // KTA, KV_HEADS * VROWS, KTA), lambda i: (i, 0, 0))
    sds = jax.ShapeDtypeStruct
    return pl.pallas_call(
        _proj_t_kernel,
        grid=(m // tm,),
        in_specs=[pl.BlockSpec((tm, kdim), lambda i: (i, 0)), full(wq), full(wkv), full(wqi), full(wkw),
                  tab(ca), tab(sa), tab(ci), tab(si)],
        out_specs=[cols(1024), rows(256), vspec, cols(1024), rows(IDX_DIM), cols(IDX_HEADS), rows(kdim)],
        out_shape=[sds((1024, m), BF16), sds((m, 256), BF16), sds((m // KTA, KV_HEADS * VROWS, KTA), BF16),
                   sds((1024, m), BF16), sds((m, IDX_DIM), BF16), sds((IDX_HEADS, m), F32), sds((m, kdim), BF16)],
        compiler_params=pltpu.CompilerParams(dimension_semantics=("arbitrary",), vmem_limit_bytes=VMEM_LIMIT),
        name="proj_t",
    )(x2, wq, wkv, wqi, wkw, ca, sa, ci, si)


def _proj_n_kernel(x_ref, w_ref, o_ref):
    o_ref[...] = _nn(x_ref[...], w_ref[...]).astype(o_ref.dtype)


def _proj_n(xb, w, tm, tn):
    m, kdim = xb.shape
    n = w.shape[1]
    return pl.pallas_call(
        _proj_n_kernel,
        grid=(n // tn, m // tm),
        in_specs=[pl.BlockSpec((tm, kdim), lambda j, i: (i, 0)), pl.BlockSpec((kdim, tn), lambda j, i: (0, j))],
        out_specs=pl.BlockSpec((tm, tn), lambda j, i: (i, j)),
        out_shape=jax.ShapeDtypeStruct((m, n), F32),
        compiler_params=pltpu.CompilerParams(dimension_semantics=("arbitrary", "arbitrary"),
                                             vmem_limit_bytes=VMEM_LIMIT),
        name="proj_n",
    )(xb, w)


def _fold64(x, op2):
    parts = [x[a * 64:(a + 1) * 64] for a in range(x.shape[0] // 64)]
    while len(parts) > 1:
        parts = [op2(parts[2 * a], parts[2 * a + 1]) for a in range(len(parts) // 2)]
    return parts[0]


def _fold8_tree(x):
    parts = [x[a * 8:(a + 1) * 8] for a in range(x.shape[0] // 8)]
    while len(parts) > 1:
        parts = [parts[2 * a] + parts[2 * a + 1] for a in range(len(parts) // 2)]
    return parts[0]


def _bit_transpose32(words):
    a = list(words)
    mask, j = 0x0000FFFF, 16
    while j:
        k = 0
        while k < 32:
            t = (a[k] ^ (a[k + j] >> j)) & mask
            a[k] = a[k] ^ t
            a[k + j] = a[k + j] ^ (t << j)
            k = (k + j + 1) & ~j
        j >>= 1
        mask = (mask ^ (mask << j)) & 0xFFFFFFFF
    return a


def _attn_kernel(qT_ref, qiT_ref, wT_ref, z_ref, k_ref, vT_ref, kidx_ref, o_ref,
                 key_ref, plane_ref, bias_ref, m_ref, l_ref, acc_ref, s_ref, mt_ref, *, topk):
    i = pl.program_id(1)
    nfull = 2 * i
    q_chunk = (i * QB + lax.broadcasted_iota(I32, (1, QB), 1)) // CHUNK

    def score_tile(j):
        rows = pl.ds(pl.multiple_of(j * KT, KT), KT)
        kt = kidx_ref[rows, :]
        acc = jnp.zeros((KT, QB), F32)
        for h in range(IDX_HEADS):
            qh = qiT_ref[h * IDX_DIM:(h + 1) * IDX_DIM, :]
            acc += jnp.maximum(_nn(kt, qh), 0.0) * wT_ref[h:h + 1, :]
        bits = pltpu.bitcast(acc, I32)
        return rows, bits ^ ((bits >> 31) & 0x7FFFFFFF)

    def store_planes(t256, keys):
        words = []
        for kk in keys:
            u = kk ^ INT_MIN
            words += [u[8 * a:8 * a + 8] for a in range(kk.shape[0] // 8)]
        planes = _bit_transpose32(words[::-1])
        prow = pl.ds(pl.multiple_of(t256 * 8, 8), 8)
        for b in range(32):
            plane_ref[b, prow, :] = planes[b]

    def planes_from_keys(t256):
        rows = pl.ds(pl.multiple_of(t256 * 2 * KT, 2 * KT), 2 * KT)
        store_planes(t256, [key_ref[rows, :]])

    @pl.when((pl.program_id(0) == 0) & (i == 0))
    def _():
        key_ref[...] = jnp.zeros(key_ref.shape, I32)

    def full_tiles(j2, c):
        planes_from_keys(jnp.maximum(j2 - 1, 0))
        for jj in range(2):
            rows, key = score_tile(2 * j2 + jj)
            key_ref[rows, :] = key
        return c

    lax.fori_loop(0, i, full_tiles, 0)
    planes_from_keys(jnp.maximum(i - 1, 0))
    keys = []
    for jj in range(QB // KT):
        j = nfull + jj
        rows, key = score_tile(j)
        k_chunk = (j * KT + lax.broadcasted_iota(I32, (KT, 1), 0)) // CHUNK
        key = jnp.where(k_chunk <= q_chunk, key, INT_MIN)
        key_ref[rows, :] = key
        keys.append(key)
    store_planes(i, keys)

    nprow = plane_ref.shape[1]
    t_row = lax.broadcasted_iota(I32, (nprow, QB), 0) >> 3
    qc_local = lax.broadcasted_iota(I32, (1, QB), 1) // CHUNK
    diag_bits = lax.shift_right_logical(jnp.full((1, QB), -1, I32), 32 - 8 * (qc_local + 1))
    act0 = jnp.where(t_row < i, -1, jnp.where(t_row == i, diag_bits, 0))

    def radix_select(nrow):
        act = act0[:nrow]
        above = jnp.zeros((1, QB), I32)
        thr_u = jnp.zeros((1, QB), I32)
        for b in range(32):
            ones = act & plane_ref[b, 0:nrow, :]
            c8 = _fold8_tree(lax.population_count(ones))
            c = above + jnp.sum(c8, axis=0, keepdims=True)
            take = c >= topk
            act = jnp.where(take, ones, act ^ ones)
            above = jnp.where(take, above, c)
            thr_u = thr_u | jnp.where(take, jnp.int32(-(2 ** 31) if b == 0 else 1 << (31 - b)), 0)
        if nrow < nprow:
            act = jnp.concatenate([act, jnp.zeros((nprow - nrow, QB), I32)], axis=0)
        return thr_u, above, act

    half = nprow // 2
    thr_u, above, act = lax.cond((i + 1) * 8 <= half, lambda: radix_select(half), lambda: radix_select(nprow))
    thr = jnp.maximum(thr_u ^ INT_MIN, INT_MIN + 1)
    need = topk - above
    n_tied = jnp.sum(_fold8_tree(lax.population_count(act)), axis=0, keepdims=True)
    any_tie = jnp.max(jnp.where(n_tied > need, 1, 0)) > 0

    @pl.when(jnp.logical_not(any_tie))
    def _():
        def to_bias(j, c):
            rows = pl.ds(pl.multiple_of(j * KTA, KTA), KTA)
            bias_ref[rows, :] = jnp.where(key_ref[rows, :] >= thr, 0.0, -jnp.inf)
            return c

        lax.fori_loop(0, i + 1, to_bias, 0)

    @pl.when(any_tie)
    def _():
        p_row = lax.broadcasted_iota(I32, (nprow, QB), 0)
        word_bit_clear = {7: 0x0000FFFF, 6: 0x00FF00FF, 5: 0x0F0F0F0F, 4: 0x33333333, 3: 0x55555555}
        cand, left, idx_thr = act, need, jnp.zeros((1, QB), I32)
        for b in range((bias_ref.shape[0] - 1).bit_length() - 1, -1, -1):
            if b >= 8:
                zmask = jnp.where(((p_row >> (3 + b - 8)) & 1) == 0, -1, 0)
            elif b >= 3:
                zmask = word_bit_clear[b]
            else:
                zmask = jnp.where(((p_row >> b) & 1) == 0, -1, 0)
            zeros = cand & zmask
            c0 = jnp.sum(_fold8_tree(lax.population_count(zeros)), axis=0, keepdims=True)
            low = c0 >= left
            cand = jnp.where(low, zeros, cand ^ zeros)
            left = jnp.where(low, left, left - c0)
            idx_thr = idx_thr | jnp.where(low, 0, 1 << b)

        def to_bias_tied(j, c):
            rows = pl.ds(pl.multiple_of(j * KTA, KTA), KTA)
            key = key_ref[rows, :]
            kidx_abs = j * KTA + lax.broadcasted_iota(I32, (KTA, 1), 0)
            keep = (key > thr) | ((key == thr) & (kidx_abs <= idx_thr))
            bias_ref[rows, :] = jnp.where(keep, 0.0, -jnp.inf)
            return c

        lax.fori_loop(0, i + 1, to_bias_tied, 0)

    m_ref[...] = jnp.full(m_ref.shape, -jnp.inf, F32)
    l_ref[...] = jnp.zeros(l_ref.shape, F32)
    acc_ref[...] = jnp.zeros(acc_ref.shape, F32)

    rep = Q_HEADS // KV_HEADS
    gq = rep * QB
    ntile = bias_ref.shape[0] // KTA
    jpad = jnp.minimum(i + 1, ntile - 1)

    @pl.when(i + 1 < ntile)
    def _():
        bias_ref[pl.ds(pl.multiple_of((i + 1) * KTA, KTA), KTA), :] = jnp.full((KTA, QB), -jnp.inf, F32)

    def masked_logits(j, slot):
        jc = jnp.minimum(j, jpad)
        rows = pl.ds(pl.multiple_of(jc * KTA, KTA), KTA)
        for g in range(KV_HEADS):
            qg = jnp.concatenate([qT_ref[(g * rep + r) * HEAD_DIM:(g * rep + r + 1) * HEAD_DIM, :]
                                  for r in range(rep)], axis=1)
            sg = _nn(k_ref[rows, g * HEAD_DIM:(g + 1) * HEAD_DIM], qg)
            for r in range(rep):
                h = g * rep + r
                sh = sg[:, r * QB:(r + 1) * QB] + bias_ref[rows, :]
                s_ref[slot, :, h * QB:(h + 1) * QB] = sh
                mt_ref[slot, h:h + 1, :] = jnp.max(_fold64(sh, jnp.maximum), axis=0, keepdims=True)

    def softmax_pv(j, slot):
        jc = jnp.minimum(j, jpad)
        for h in range(Q_HEADS):
            g, r = divmod(h, rep)
            hl = slice(r * QB, (r + 1) * QB)
            s = s_ref[slot, :, h * QB:(h + 1) * QB]
            m = m_ref[g:g + 1, hl]
            m_new = jnp.maximum(m, mt_ref[slot, h:h + 1, :])
            m_safe = jnp.where(m_new == -jnp.inf, 0.0, m_new)
            alpha = jnp.exp2(m - m_safe)
            p = jnp.exp2(s - m_safe)
            pv = _nn(vT_ref[jc, g * VROWS:(g + 1) * VROWS, :], p.astype(BF16))
            l_ref[g:g + 1, hl] = alpha * l_ref[g:g + 1, hl] + pv[HEAD_DIM:HEAD_DIM + 1, :]
            acc_ref[g, :, hl] = alpha * acc_ref[g, :, hl] + pv[:HEAD_DIM]
            m_ref[g:g + 1, hl] = m_new

    masked_logits(0, 0)

    def kv_pair(jj):
        a = 2 * jj
        masked_logits(a + 1, 1)
        softmax_pv(a, 0)
        masked_logits(a + 2, 0)
        softmax_pv(a + 1, 1)

    def kv_two_pairs(jq, c):
        kv_pair(2 * jq)
        kv_pair(2 * jq + 1)
        return c

    def kv_one_pair(jj, c):
        kv_pair(jj)
        return c

    npairs = (i + 2) // 2
    lax.fori_loop(0, npairs // 2, kv_two_pairs, 0)
    lax.fori_loop(2 * (npairs // 2), npairs, kv_one_pair, 0)
    for h in range(Q_HEADS):
        g, r = divmod(h, rep)
        inv_l = 1.0 / l_ref[g:g + 1, r * QB:(r + 1) * QB]
        oh = (acc_ref[g, :, r * QB:(r + 1) * QB] * inv_l).T
        zh = z_ref[:, h * HEAD_DIM:(h + 1) * HEAD_DIM]
        o_ref[:, h * HEAD_DIM:(h + 1) * HEAD_DIM] = (oh * _silu(zh)).astype(o_ref.dtype)


def _attention(qT, qiT, wT, znat, zcol, k, vT, kidx, bsz, seq):
    nq = seq // QB
    topk = min(TOPK_MAX, seq // 4)
    m = bsz * seq
    step = lambda r: pl.BlockSpec((r, QB), lambda b, i: (0, b * nq + i))
    return pl.pallas_call(
        functools.partial(_attn_kernel, topk=topk),
        grid=(bsz, nq),
        in_specs=[step(1024), step(1024), step(IDX_HEADS),
                  pl.BlockSpec((QB, ATTN_WIDTH), lambda b, i: (b * nq + i, zcol)),
                  pl.BlockSpec((seq, 256), lambda b, i: (b, 0)),
                  pl.BlockSpec((seq // KTA, KV_HEADS * VROWS, KTA), lambda b, i: (b, 0, 0)),
                  pl.BlockSpec((seq, IDX_DIM), lambda b, i: (b, 0))],
        out_specs=pl.BlockSpec((QB, ATTN_WIDTH), lambda b, i: (b * nq + i, 0)),
        out_shape=jax.ShapeDtypeStruct((m, ATTN_WIDTH), BF16),
        scratch_shapes=[pltpu.VMEM((seq, QB), I32), pltpu.VMEM((32, seq // 32, QB), I32),
                        pltpu.VMEM((seq, QB), F32),
                        pltpu.VMEM((KV_HEADS, Q_HEADS // KV_HEADS * QB), F32),
                        pltpu.VMEM((KV_HEADS, Q_HEADS // KV_HEADS * QB), F32),
                        pltpu.VMEM((KV_HEADS, HEAD_DIM, Q_HEADS // KV_HEADS * QB), F32),
                        pltpu.VMEM((2, KTA, Q_HEADS * QB), F32), pltpu.VMEM((2, Q_HEADS, QB), F32)],
        compiler_params=pltpu.CompilerParams(dimension_semantics=("arbitrary", "arbitrary"),
                                             vmem_limit_bytes=VMEM_LIMIT),
        name="dsa_attention",
    )(qT, qiT, wT, znat, k, vT, kidx)


def _ssd_kernel(xbc_ref, z_ref, dt_ref, cw_ref, cb_ref, dtb_ref, a_ref, dsk_ref, nw_ref, e2_ref, tril_ref, o_ref,
                buf_ref, tail_ref, state_ref, xc_ref, xdt_ref, acum_ref, ahi_ref, alo_ref, *, tt):
    t = pl.program_id(1)

    @pl.when(t == 0)
    def _():
        tail_ref[...] = jnp.zeros_like(tail_ref)
        state_ref[...] = jnp.zeros_like(state_ref)

    buf_ref[0:8, :] = tail_ref[...]
    buf_ref[8:8 + tt, :] = xbc_ref[...]
    tail_ref[...] = xbc_ref[tt - 8:tt, :]

    dtv = dt_ref[...] + dtb_ref[...]
    dtv = jnp.maximum(dtv, 0.0) + jnp.log(1.0 + jnp.exp(-jnp.abs(dtv)))
    dhi = dtv.astype(BF16).astype(F32)
    lane = lax.broadcasted_iota(I32, (1, 128), 1)
    dsplit = jnp.where(lane < SSD_HEADS, dhi, dtv - dhi).astype(BF16)
    acum_ref[...] = _nn(dsplit, e2_ref[...])
    neg_a = -jnp.exp(a_ref[...])

    for c in range(tt // CHUNK):
        r0 = c * CHUNK
        conv = cb_ref[...] + cw_ref[3:4, :] * buf_ref[8 + r0:8 + r0 + CHUNK, :]
        for jtap in range(SSD_CONV - 1):
            conv += cw_ref[jtap:jtap + 1, :] * buf_ref[5 + jtap + r0:5 + jtap + r0 + CHUNK, :]
        xc = _silu(conv)
        xc_ref[r0:r0 + CHUNK, :] = xc
        dt_e = acum_ref[r0:r0 + CHUNK, :]
        xdt_ref[r0:r0 + CHUNK, :] = xc[:, :SSD_WIDTH] * dt_e
        dta_e = dt_e * neg_a
        ahi = dta_e.astype(BF16)
        ahi_ref[r0:r0 + CHUNK, :] = ahi
        alo_ref[r0:r0 + CHUNK, :] = (dta_e - ahi.astype(F32)).astype(BF16)
    acum_ref[...] = _nn(tril_ref[...], ahi_ref[...]) + _nn(tril_ref[...], alo_ref[...])

    gw = SSD_WIDTH // SSD_GROUPS
    s_row = lax.broadcasted_iota(I32, (CHUNK, gw), 0)
    s_lane = lax.broadcasted_iota(I32, (CHUNK, gw), 1) & (CHUNK - 1)
    diag = s_row == s_lane
    tril = s_row >= s_lane
    bd_r = lax.broadcasted_iota(I32, (2 * SSD_P, 2 * SSD_P), 0) >> 6
    bd_c = lax.broadcasted_iota(I32, (2 * SSD_P, 2 * SSD_P), 1) >> 6
    blockdiag = bd_r == bd_c

    for c in range(tt // CHUNK):
        rows = slice(c * CHUNK, (c + 1) * CHUNK)
        for g in range(SSD_GROUPS):
            gl = slice(g * gw, (g + 1) * gw)
            bg = xc_ref[rows, SSD_WIDTH + g * SSD_N:SSD_WIDTH + (g + 1) * SSD_N]
            cg = xc_ref[rows, SSD_WIDTH + (SSD_GROUPS + g) * SSD_N:SSD_WIDTH + (SSD_GROUPS + g + 1) * SSD_N]
            bg16, cg16 = bg.astype(BF16), cg.astype(BF16)
            a_g = acum_ref[rows, gl]
            a_last = a_g[CHUNK - 1:CHUNK, :]
            xdt_g = xdt_ref[rows, gl]
            cb = _nt(cg16, jnp.concatenate([bg16] * (gw // CHUNK), axis=0))
            a_row = jnp.sum(jnp.where(diag, a_g, 0.0), axis=0, keepdims=True)
            lmat = jnp.exp(jnp.where(tril, a_g - a_row, -jnp.inf))
            mmat = (cb * lmat).astype(BF16)
            ys = []
            for q in range(gw // (2 * SSD_P)):
                ql = slice(q * 2 * SSD_P, (q + 1) * 2 * SSD_P)
                x2 = xdt_g[:, ql]
                wq = jnp.where(blockdiag, jnp.concatenate([x2, x2], axis=0), 0.0).astype(BF16)
                ys.append(_nn(mmat[:, ql], wq))
            y = jnp.concatenate(ys, axis=1)
            prev = state_ref[g]
            y += _nn(cg16, prev.astype(BF16)) * jnp.exp(a_g)
            y += dsk_ref[:, gl] * xc_ref[rows, gl]
            xdd = (xdt_g * jnp.exp(a_last - a_g)).astype(BF16)
            state_ref[g] = prev * jnp.exp(a_last) + _nn(bg.T.astype(BF16), xdd)
            gz = y * _silu(z_ref[rows, gl])
            gz = gz * lax.rsqrt(jnp.mean(gz * gz, axis=1, keepdims=True) + RMS_EPS)
            o_ref[rows, gl] = (gz * nw_ref[:, gl]).astype(o_ref.dtype)


def _ssd(nat, conv_w, conv_b, dt_bias, a_log, d_skip, norm_w, bsz, seq, tt):
    m = bsz * seq
    nt = seq // tt
    row = lambda b, t: b * nt + t
    full = lambda a: pl.BlockSpec(a.shape, lambda b, t: (0, 0))
    expand = lambda v: jnp.repeat(v.astype(F32), SSD_P).reshape(1, SSD_WIDTH)
    dtb2 = _pad_lanes(jnp.concatenate([dt_bias, dt_bias]))
    j_i = lax.broadcasted_iota(I32, (128, SSD_WIDTH), 0)
    h_i = lax.broadcasted_iota(I32, (128, SSD_WIDTH), 1) // SSD_P
    e2 = ((j_i == h_i) | (j_i == h_i + SSD_HEADS)).astype(BF16)
    r_i = lax.broadcasted_iota(I32, (tt, tt), 0)
    c_i = lax.broadcasted_iota(I32, (tt, tt), 1)
    tril_bd = ((r_i >= c_i) & (r_i // CHUNK == c_i // CHUNK)).astype(BF16)
    return pl.pallas_call(
        functools.partial(_ssd_kernel, tt=tt),
        grid=(bsz, nt),
        in_specs=[pl.BlockSpec((tt, XBC_WIDTH), lambda b, t: (row(b, t), 0)),
                  pl.BlockSpec((tt, SSD_WIDTH), lambda b, t: (row(b, t), XBC_WIDTH // SSD_WIDTH)),
                  pl.BlockSpec((tt, 128), lambda b, t: (row(b, t), (XBC_WIDTH + SSD_WIDTH + ATTN_WIDTH) // 128)),
                  full(conv_w), full(conv_b), full(dtb2), pl.BlockSpec((1, SSD_WIDTH), lambda b, t: (0, 0)),
                  pl.BlockSpec((1, SSD_WIDTH), lambda b, t: (0, 0)), full(norm_w), full(e2), full(tril_bd)],
        out_specs=pl.BlockSpec((tt, SSD_WIDTH), lambda b, t: (row(b, t), 0)),
        out_shape=jax.ShapeDtypeStruct((m, SSD_WIDTH), BF16),
        scratch_shapes=[pltpu.VMEM((tt + 8, XBC_WIDTH), F32), pltpu.VMEM((8, XBC_WIDTH), F32),
                        pltpu.VMEM((SSD_GROUPS, SSD_N, SSD_WIDTH // SSD_GROUPS), F32),
                        pltpu.VMEM((tt, XBC_WIDTH), F32), pltpu.VMEM((tt, SSD_WIDTH), F32),
                        pltpu.VMEM((tt, SSD_WIDTH), F32),
                        pltpu.VMEM((tt, SSD_WIDTH), BF16), pltpu.VMEM((tt, SSD_WIDTH), BF16)],
        compiler_params=pltpu.CompilerParams(dimension_semantics=("arbitrary", "arbitrary"),
                                             vmem_limit_bytes=VMEM_LIMIT),
        name="ssd_mixer",
    )(nat, nat, nat, conv_w, conv_b, dtb2, expand(a_log), expand(d_skip), norm_w, e2, tril_bd)


def _out_kernel(oa_ref, os_ref, x_ref, wa_ref, ws_ref, g_ref, b_ref, o_ref, *, alpha):
    nsplit = 4
    rb = o_ref.shape[0] // nsplit
    for r in range(nsplit):
        rows = slice(r * rb, (r + 1) * rb)
        sub = _nn(oa_ref[rows, :], wa_ref[...]) + _nn(os_ref[rows, :], ws_ref[...])
        y = alpha * x_ref[rows, :] + sub
        mu = jnp.mean(y, axis=1, keepdims=True)
        yc = y - mu
        var = jnp.mean(yc * yc, axis=1, keepdims=True)
        o_ref[rows, :] = yc * lax.rsqrt(var + LN_EPS) * g_ref[...] + b_ref[...]


def _out_proj(oa, os_, x2, wa, ws, g, b, alpha, tm):
    m = x2.shape[0]
    full = lambda a: pl.BlockSpec(a.shape, lambda i: (0, 0))
    rowblk = lambda w: pl.BlockSpec((tm, w), lambda i: (i, 0))
    return pl.pallas_call(
        functools.partial(_out_kernel, alpha=alpha),
        grid=(m // tm,),
        in_specs=[rowblk(ATTN_WIDTH), rowblk(SSD_WIDTH), rowblk(D_MODEL), full(wa), full(ws), full(g), full(b)],
        out_specs=rowblk(D_MODEL),
        out_shape=jax.ShapeDtypeStruct((m, D_MODEL), F32),
        compiler_params=pltpu.CompilerParams(dimension_semantics=("arbitrary",), vmem_limit_bytes=VMEM_LIMIT),
        name="out_proj_ln",
    )(oa, os_, x2, wa, ws, g, b)


def _rope_tables(seq, rot):
    half = rot // 2
    inv = ROPE_THETA ** (-jnp.arange(half, dtype=F32) * 2.0 / rot)
    ang = inv[:, None] * jnp.arange(seq, dtype=F32)[None, :]
    return jnp.cos(ang), jnp.sin(ang)


def _pad_lanes(v, width=128):
    v = v.reshape(1, -1).astype(F32)
    return jnp.pad(v, ((0, 0), (0, width - v.shape[1])))


def _layer(h, w_in, w_out, conv_w, conv_b, dt_bias, a_log, d_skip, norm_w, ln_g, ln_b, alpha):
    bsz, seq, _ = h.shape
    m = bsz * seq
    assert seq % QB == 0 and seq % 512 == 0
    x2 = h.reshape(m, D_MODEL)

    o_q, o_k, o_v, o_za, o_qi, o_ki, o_wi, o_zs, o_xbc, o_dt = (
        0, 1024, 1280, 1536, 2560, 3584, 3648, 3664, 4688, 6736)
    w_all, layer = w_in
    wcol = lambda a, b: w_all[layer, :, a:b]
    wqkv_t = wcol(o_q, o_za).T.astype(BF16)
    widx_t = wcol(o_qi, o_zs).T.astype(BF16)
    wq = wqkv_t[:o_k]
    wkv = wqkv_t[o_k:]
    wqi = widx_t[:o_ki - o_qi]
    wkw = jnp.pad(widx_t[o_ki - o_qi:], ((0, 128 - (o_zs - o_ki)), (0, 0)))
    w_dt = wcol(o_dt, o_dt + SSD_HEADS)
    wn = jnp.concatenate([wcol(o_xbc, o_dt), wcol(o_zs, o_xbc), wcol(o_za, o_qi),
                          jnp.pad(jnp.concatenate([w_dt, w_dt], axis=1), ((0, 0), (0, 128 - 2 * SSD_HEADS)))],
                         axis=1).astype(BF16)

    ca, sa = _rope_tables(seq, HEAD_DIM // ROPE_DIV)
    ci, si = _rope_tables(seq, IDX_DIM // ROPE_DIV)

    qT, k, vT, qiT, kidx, wT, xb = _proj_t(x2, wq, wkv, wqi, wkw, ca, sa, ci, si, seq, tm=512)
    nat = _proj_n(xb, wn, tm=min(1024, m), tn=1408)

    o_attn = _attention(qT, qiT, wT, nat, (XBC_WIDTH + SSD_WIDTH) // ATTN_WIDTH, k, vT, kidx, bsz, seq)
    o_ssd = _ssd(nat, conv_w, conv_b.reshape(1, -1), dt_bias, a_log, d_skip, norm_w.reshape(1, -1),
                 bsz, seq, tt=256)

    wo_all, _ = w_out
    out = _out_proj(o_attn, o_ssd, x2, wo_all[layer, :ATTN_WIDTH].astype(BF16), wo_all[layer, ATTN_WIDTH:].astype(BF16),
                    ln_g.reshape(1, -1), ln_b.reshape(1, -1), alpha, tm=512)
    return out.reshape(bsz, seq, D_MODEL)


def kernel(x, w_in, w_out, conv_w, conv_b, dt_bias, a_log, d_skip, ssd_norm_w, ln_g, ln_b):
    depth = w_in.shape[0]
    alpha = (2.0 * depth) ** 0.25
    h = x
    for layer in range(depth):
        h = _layer(h, (w_in, layer), (w_out, layer), conv_w[layer], conv_b[layer], dt_bias[layer],
                   a_log[layer], d_skip[layer], ssd_norm_w[layer], ln_g[layer], ln_b[layer], alpha)
    return h
```

```python
import functools
import math

import jax
import jax.numpy as jnp
from jax import lax
from jax.experimental import pallas as pl
from jax.experimental.pallas import tpu as pltpu

F32 = jnp.float32
BF16 = jnp.bfloat16
I32 = jnp.int32

D_MODEL = 2048
CHUNK = 64
ATTN_WIDTH = 1024
SSD_WIDTH = 1024
HEAD_DIM = 128
Q_HEADS = 8
KV_HEADS = 2
ROPE_THETA = 500000.0
ROPE_DIV = 4
IDX_HEADS = 16
IDX_DIM = 64
TOPK_MAX = 256
SSD_P = 64
SSD_HEADS = 16
SSD_GROUPS = 4
SSD_N = 128
SSD_CONV = 4
XBC_WIDTH = SSD_WIDTH + 2 * SSD_GROUPS * SSD_N
LN_EPS = 1e-5
RMS_EPS = 1e-5

VMEM_LIMIT = 56 * 1024 * 1024
INT_MIN = -(2 ** 31)

QB = 256
KT = 128
KTA = 256
VPAD = 16
VROWS = HEAD_DIM + VPAD


def _nt(a, b):
    return lax.dot_general(a, b, (((1,), (1,)), ((), ())), preferred_element_type=F32)


def _nn(a, b):
    return jnp.dot(a, b, preferred_element_type=F32)


def _silu(v):
    return v * (1.0 / (1.0 + jnp.exp(-v)))


def _rope_rows(t, cos, sin, nheads, hd, half):
    pieces = []
    for h in range(nheads):
        b = h * hd
        x1 = t[b:b + half]
        x2 = t[b + half:b + 2 * half]
        pieces += [x1 * cos - x2 * sin, x2 * cos + x1 * sin, t[b + 2 * half:b + hd]]
    return jnp.concatenate(pieces, axis=0)


def _proj_t_kernel(x_ref, wq_ref, wkv_ref, wqi_ref, wkw_ref, ca_ref, sa_ref, ci_ref, si_ref,
                   q_ref, k_ref, v_ref, qi_ref, kidx_ref, w_ref, xb_ref):
    x = x_ref[...].astype(BF16)
    xb_ref[...] = x
    ca, sa, ci, si = ca_ref[...], sa_ref[...], ci_ref[...], si_ref[...]
    qscale = HEAD_DIM ** -0.5 * math.log2(math.e)
    tq = _nt(wq_ref[...], x)
    q_ref[...] = (_rope_rows(tq, ca, sa, Q_HEADS, HEAD_DIM, 16) * qscale).astype(BF16)
    tkv = _nt(wkv_ref[...], x)
    k_ref[...] = _rope_rows(tkv[:KV_HEADS * HEAD_DIM], ca, sa, KV_HEADS, HEAD_DIM, 16).T.astype(BF16)
    ones = jnp.ones((VPAD, KTA), BF16)
    for jj in range(v_ref.shape[0]):
        for g in range(KV_HEADS):
            vg = tkv[(KV_HEADS + g) * HEAD_DIM:(KV_HEADS + g + 1) * HEAD_DIM, jj * KTA:(jj + 1) * KTA]
            v_ref[jj, g * VROWS:(g + 1) * VROWS, :] = jnp.concatenate([vg.astype(BF16), ones], axis=0)
    tqi = _nt(wqi_ref[...], x)
    qi_ref[...] = _rope_rows(tqi, ci, si, IDX_HEADS, IDX_DIM, 8).astype(BF16)
    tkw = _nt(wkw_ref[...], x)
    kidx_ref[...] = _rope_rows(tkw[:IDX_DIM], ci, si, 1, IDX_DIM, 8).T.astype(BF16)
    w_ref[...] = tkw[IDX_DIM:IDX_DIM + IDX_HEADS] * (IDX_HEADS ** -0.5 * IDX_DIM ** -0.5)


def _proj_t(x2, wq, wkv, wqi, wkw, ca, sa, ci, si, seq, tm):
    m, kdim = x2.shape
    nper = seq // tm
    full = lambda a: pl.BlockSpec(a.shape, lambda i: (0, 0))
    tab = lambda a: pl.BlockSpec((a.shape[0], tm), lambda i: (0, i % nper))
    cols = lambda r: pl.BlockSpec((r, tm), lambda i: (0, i))
    rows = lambda c: pl.BlockSpec((tm, c), lambda i: (i, 0))
    vspec = pl.BlockSpec((tm // KTA, KV_HEADS * VROWS, KTA), lambda i: (i, 0, 0))
    sds = jax.ShapeDtypeStruct
    return pl.pallas_call(
        _proj_t_kernel,
        grid=(m // tm,),
        in_specs=[pl.BlockSpec((tm, kdim), lambda i: (i, 0)), full(wq), full(wkv), full(wqi), full(wkw),
                  tab(ca), tab(sa), tab(ci), tab(si)],
        out_specs=[cols(1024), rows(256), vspec, cols(1024), rows(IDX_DIM), cols(IDX_HEADS), rows(kdim)],
        out_shape=[sds((1024, m), BF16), sds((m, 256), BF16), sds((m // KTA, KV_HEADS * VROWS, KTA), BF16),
                   sds((1024, m), BF16), sds((m, IDX_DIM), BF16), sds((IDX_HEADS, m), F32), sds((m, kdim), BF16)],
        compiler_params=pltpu.CompilerParams(dimension_semantics=("arbitrary",), vmem_limit_bytes=VMEM_LIMIT),
        name="proj_t",
    )(x2, wq, wkv, wqi, wkw, ca, sa, ci, si)


def _proj_n_kernel(x_ref, w_ref, o_ref):
    o_ref[...] = _nn(x_ref[...], w_ref[...]).astype(o_ref.dtype)


def _proj_n(xb, w, tm, tn):
    m, kdim = xb.shape
    n = w.shape[1]
    return pl.pallas_call(
        _proj_n_kernel,
        grid=(n // tn, m // tm),
        in_specs=[pl.BlockSpec((tm, kdim), lambda j, i: (i, 0)), pl.BlockSpec((kdim, tn), lambda j, i: (0, j))],
        out_specs=pl.BlockSpec((tm, tn), lambda j, i: (i, j)),
        out_shape=jax.ShapeDtypeStruct((m, n), F32),
        compiler_params=pltpu.CompilerParams(dimension_semantics=("arbitrary", "arbitrary"),
                                             vmem_limit_bytes=VMEM_LIMIT),
        name="proj_n",
    )(xb, w)


def _fold64(x, op2):
    parts = [x[a * 64:(a + 1) * 64] for a in range(x.shape[0] // 64)]
    while len(parts) > 1:
        parts = [op2(parts[2 * a], parts[2 * a + 1]) for a in range(len(parts) // 2)]
    return parts[0]


def _fold8_tree(x):
    parts = [x[a * 8:(a + 1) * 8] for a in range(x.shape[0] // 8)]
    while len(parts) > 1:
        parts = [parts[2 * a] + parts[2 * a + 1] for a in range(len(parts) // 2)]
    return parts[0]


def _bit_transpose32(words):
    a = list(words)
    mask, j = 0x0000FFFF, 16
    while j:
        k = 0
        while k < 32:
            t = (a[k] ^ (a[k + j] >> j)) & mask
            a[k] = a[k] ^ t
            a[k + j] = a[k + j] ^ (t << j)
            k = (k + j + 1) & ~j
        j >>= 1
        mask = (mask ^ (mask << j)) & 0xFFFFFFFF
    return a


def _attn_kernel(qT_ref, qiT_ref, wT_ref, z_ref, k_ref, vT_ref, kidx_ref, o_ref,
                 key_ref, plane_ref, bias_ref, m_ref, l_ref, acc_ref, s_ref, mt_ref, *, topk):
    i = pl.program_id(1)
    nfull = 2 * i
    q_chunk = (i * QB + lax.broadcasted_iota(I32, (1, QB), 1)) // CHUNK

    def score_tile(j):
        rows = pl.ds(pl.multiple_of(j * KT, KT), KT)
        kt = kidx_ref[rows, :]
        acc = jnp.zeros((KT, QB), F32)
        for h in range(IDX_HEADS):
            qh = qiT_ref[h * IDX_DIM:(h + 1) * IDX_DIM, :]
            acc += jnp.maximum(_nn(kt, qh), 0.0) * wT_ref[h:h + 1, :]
        bits = pltpu.bitcast(acc, I32)
        return rows, bits ^ ((bits >> 31) & 0x7FFFFFFF)

    def store_planes(t256, keys):
        words = []
        for kk in keys:
            u = kk ^ INT_MIN
            words += [u[8 * a:8 * a + 8] for a in range(kk.shape[0] // 8)]
        planes = _bit_transpose32(words[::-1])
        prow = pl.ds(pl.multiple_of(t256 * 8, 8), 8)
        for b in range(32):
            plane_ref[b, prow, :] = planes[b]

    def planes_from_keys(t256):
        rows = pl.ds(pl.multiple_of(t256 * 2 * KT, 2 * KT), 2 * KT)
        store_planes(t256, [key_ref[rows, :]])

    @pl.when((pl.program_id(0) == 0) & (i == 0))
    def _():
        key_ref[...] = jnp.zeros(key_ref.shape, I32)

    def key_block(j2):
        planes_from_keys(jnp.maximum(j2 - 1, 0))
        for jj in range(2):
            rows, key = score_tile(2 * j2 + jj)
            key_ref[rows, :] = key

    def two_blocks(jq, c):
        key_block(2 * jq)
        key_block(2 * jq + 1)
        return c

    def one_block(j2, c):
        key_block(j2)
        return c

    lax.fori_loop(0, i // 2, two_blocks, 0)
    lax.fori_loop(2 * (i // 2), i, one_block, 0)
    planes_from_keys(jnp.maximum(i - 1, 0))
    keys = []
    for jj in range(QB // KT):
        j = nfull + jj
        rows, key = score_tile(j)
        k_chunk = (j * KT + lax.broadcasted_iota(I32, (KT, 1), 0)) // CHUNK
        key = jnp.where(k_chunk <= q_chunk, key, INT_MIN)
        key_ref[rows, :] = key
        keys.append(key)
    store_planes(i, keys)

    nprow = plane_ref.shape[1]
    t_row = lax.broadcasted_iota(I32, (nprow, QB), 0) >> 3
    qc_local = lax.broadcasted_iota(I32, (1, QB), 1) // CHUNK
    diag_bits = lax.shift_right_logical(jnp.full((1, QB), -1, I32), 32 - 8 * (qc_local + 1))
    act0 = jnp.where(t_row < i, -1, jnp.where(t_row == i, diag_bits, 0))

    def radix_select(nrow):
        act = act0[:nrow]
        above = jnp.zeros((1, QB), I32)
        thr_u = jnp.zeros((1, QB), I32)
        for b in range(32):
            ones = act & plane_ref[b, 0:nrow, :]
            c8 = _fold8_tree(lax.population_count(ones))
            c = above + jnp.sum(c8, axis=0, keepdims=True)
            take = c >= topk
            act = jnp.where(take, ones, act ^ ones)
            above = jnp.where(take, above, c)
            thr_u = thr_u | jnp.where(take, jnp.int32(-(2 ** 31) if b == 0 else 1 << (31 - b)), 0)
        if nrow < nprow:
            act = jnp.concatenate([act, jnp.zeros((nprow - nrow, QB), I32)], axis=0)
        return thr_u, above, act

    half = nprow // 2
    thr_u, above, act = lax.cond((i + 1) * 8 <= half, lambda: radix_select(half), lambda: radix_select(nprow))
    thr = jnp.maximum(thr_u ^ INT_MIN, INT_MIN + 1)
    need = topk - above
    n_tied = jnp.sum(_fold8_tree(lax.population_count(act)), axis=0, keepdims=True)
    any_tie = jnp.max(jnp.where(n_tied > need, 1, 0)) > 0

    @pl.when(jnp.logical_not(any_tie))
    def _():
        def to_bias(j, c):
            rows = pl.ds(pl.multiple_of(j * KTA, KTA), KTA)
            bias_ref[rows, :] = jnp.where(key_ref[rows, :] >= thr, 0.0, -jnp.inf)
            return c

        lax.fori_loop(0, i + 1, to_bias, 0)

    @pl.when(any_tie)
    def _():
        p_row = lax.broadcasted_iota(I32, (nprow, QB), 0)
        word_bit_clear = {7: 0x0000FFFF, 6: 0x00FF00FF, 5: 0x0F0F0F0F, 4: 0x33333333, 3: 0x55555555}
        cand, left, idx_thr = act, need, jnp.zeros((1, QB), I32)
        for b in range((bias_ref.shape[0] - 1).bit_length() - 1, -1, -1):
            if b >= 8:
                zmask = jnp.where(((p_row >> (3 + b - 8)) & 1) == 0, -1, 0)
            elif b >= 3:
                zmask = word_bit_clear[b]
            else:
                zmask = jnp.where(((p_row >> b) & 1) == 0, -1, 0)
            zeros = cand & zmask
            c0 = jnp.sum(_fold8_tree(lax.population_count(zeros)), axis=0, keepdims=True)
            low = c0 >= left
            cand = jnp.where(low, zeros, cand ^ zeros)
            left = jnp.where(low, left, left - c0)
            idx_thr = idx_thr | jnp.where(low, 0, 1 << b)

        def to_bias_tied(j, c):
            rows = pl.ds(pl.multiple_of(j * KTA, KTA), KTA)
            key = key_ref[rows, :]
            kidx_abs = j * KTA + lax.broadcasted_iota(I32, (KTA, 1), 0)
            keep = (key > thr) | ((key == thr) & (kidx_abs <= idx_thr))
            bias_ref[rows, :] = jnp.where(keep, 0.0, -jnp.inf)
            return c

        lax.fori_loop(0, i + 1, to_bias_tied, 0)

    m_ref[...] = jnp.full(m_ref.shape, -jnp.inf, F32)
    l_ref[...] = jnp.zeros(l_ref.shape, F32)
    acc_ref[...] = jnp.zeros(acc_ref.shape, F32)

    rep = Q_HEADS // KV_HEADS
    gq = rep * QB
    ntile = bias_ref.shape[0] // KTA
    jpad = jnp.minimum(i + 1, ntile - 1)

    @pl.when(i + 1 < ntile)
    def _():
        bias_ref[pl.ds(pl.multiple_of((i + 1) * KTA, KTA), KTA), :] = jnp.full((KTA, QB), -jnp.inf, F32)

    def masked_logits(j, slot):
        jc = jnp.minimum(j, jpad)
        rows = pl.ds(pl.multiple_of(jc * KTA, KTA), KTA)
        for g in range(KV_HEADS):
            qg = jnp.concatenate([qT_ref[(g * rep + r) * HEAD_DIM:(g * rep + r + 1) * HEAD_DIM, :]
                                  for r in range(rep)], axis=1)
            sg = _nn(k_ref[rows, g * HEAD_DIM:(g + 1) * HEAD_DIM], qg)
            for r in range(rep):
                h = g * rep + r
                sh = sg[:, r * QB:(r + 1) * QB] + bias_ref[rows, :]
                s_ref[slot, :, h * QB:(h + 1) * QB] = sh
                mt_ref[slot, h:h + 1, :] = jnp.max(_fold64(sh, jnp.maximum), axis=0, keepdims=True)

    def softmax_pv(j, slot):
        jc = jnp.minimum(j, jpad)
        for h in range(Q_HEADS):
            g, r = divmod(h, rep)
            hl = slice(r * QB, (r + 1) * QB)
            s = s_ref[slot, :, h * QB:(h + 1) * QB]
            m = m_ref[g:g + 1, hl]
            m_new = jnp.maximum(m, mt_ref[slot, h:h + 1, :])
            m_safe = jnp.where(m_new == -jnp.inf, 0.0, m_new)
            alpha = jnp.exp2(m - m_safe)
            p = jnp.exp2(s - m_safe)
            pv = _nn(vT_ref[jc, g * VROWS:(g + 1) * VROWS, :], p.astype(BF16))
            l_ref[g:g + 1, hl] = alpha * l_ref[g:g + 1, hl] + pv[HEAD_DIM:HEAD_DIM + 1, :]
            acc_ref[g, :, hl] = alpha * acc_ref[g, :, hl] + pv[:HEAD_DIM]
            m_ref[g:g + 1, hl] = m_new

    masked_logits(0, 0)

    def kv_pair(jj):
        a = 2 * jj
        masked_logits(a + 1, 1)
        softmax_pv(a, 0)
        masked_logits(a + 2, 0)
        softmax_pv(a + 1, 1)

    def kv_two_pairs(jq, c):
        kv_pair(2 * jq)
        kv_pair(2 * jq + 1)
        return c

    def kv_one_pair(jj, c):
        kv_pair(jj)
        return c

    npairs = (i + 2) // 2
    lax.fori_loop(0, npairs // 2, kv_two_pairs, 0)
    lax.fori_loop(2 * (npairs // 2), npairs, kv_one_pair, 0)
    for h in range(Q_HEADS):
        g, r = divmod(h, rep)
        inv_l = 1.0 / l_ref[g:g + 1, r * QB:(r + 1) * QB]
        oh = (acc_ref[g, :, r * QB:(r + 1) * QB] * inv_l).T
        zh = z_ref[:, h * HEAD_DIM:(h + 1) * HEAD_DIM]
        o_ref[:, h * HEAD_DIM:(h + 1) * HEAD_DIM] = (oh * _silu(zh)).astype(o_ref.dtype)


def _attention(qT, qiT, wT, znat, zcol, k, vT, kidx, bsz, seq):
    nq = seq // QB
    topk = min(TOPK_MAX, seq // 4)
    m = bsz * seq
    step = lambda r: pl.BlockSpec((r, QB), lambda b, i: (0, b * nq + i))
    return pl.pallas_call(
        functools.partial(_attn_kernel, topk=topk),
        grid=(bsz, nq),
        in_specs=[step(1024), step(1024), step(IDX_HEADS),
                  pl.BlockSpec((QB, ATTN_WIDTH), lambda b, i: (b * nq + i, zcol)),
                  pl.BlockSpec((seq, 256), lambda b, i: (b, 0)),
                  pl.BlockSpec((seq // KTA, KV_HEADS * VROWS, KTA), lambda b, i: (b, 0, 0)),
                  pl.BlockSpec((seq, IDX_DIM), lambda b, i: (b, 0))],
        out_specs=pl.BlockSpec((QB, ATTN_WIDTH), lambda b, i: (b * nq + i, 0)),
        out_shape=jax.ShapeDtypeStruct((m, ATTN_WIDTH), BF16),
        scratch_shapes=[pltpu.VMEM((seq, QB), I32), pltpu.VMEM((32, seq // 32, QB), I32),
                        pltpu.VMEM((seq, QB), F32),
                        pltpu.VMEM((KV_HEADS, Q_HEADS // KV_HEADS * QB), F32),
                        pltpu.VMEM((KV_HEADS, Q_HEADS // KV_HEADS * QB), F32),
                        pltpu.VMEM((KV_HEADS, HEAD_DIM, Q_HEADS // KV_HEADS * QB), F32),
                        pltpu.VMEM((2, KTA, Q_HEADS * QB), F32), pltpu.VMEM((2, Q_HEADS, QB), F32)],
        compiler_params=pltpu.CompilerParams(dimension_semantics=("arbitrary", "arbitrary"),
                                             vmem_limit_bytes=VMEM_LIMIT),
        name="dsa_attention",
    )(qT, qiT, wT, znat, k, vT, kidx)


def _ssd_kernel(xbc_ref, z_ref, dt_ref, cw_ref, cb_ref, dtb_ref, a_ref, dsk_ref, nw_ref, e2_ref, tril_ref, o_ref,
                buf_ref, tail_ref, state_ref, xc_ref, xdt_ref, acum_ref, ahi_ref, alo_ref, *, tt):
    t = pl.program_id(1)

    @pl.when(t == 0)
    def _():
        tail_ref[...] = jnp.zeros_like(tail_ref)
        state_ref[...] = jnp.zeros_like(state_ref)

    buf_ref[0:8, :] = tail_ref[...]
    buf_ref[8:8 + tt, :] = xbc_ref[...]
    tail_ref[...] = xbc_ref[tt - 8:tt, :]

    dtv = dt_ref[...] + dtb_ref[...]
    dtv = jnp.maximum(dtv, 0.0) + jnp.log(1.0 + jnp.exp(-jnp.abs(dtv)))
    dhi = dtv.astype(BF16).astype(F32)
    lane = lax.broadcasted_iota(I32, (1, 128), 1)
    dsplit = jnp.where(lane < SSD_HEADS, dhi, dtv - dhi).astype(BF16)
    acum_ref[...] = _nn(dsplit, e2_ref[...])
    neg_a = -jnp.exp(a_ref[...])

    for c in range(tt // CHUNK):
        r0 = c * CHUNK
        conv = cb_ref[...] + cw_ref[3:4, :] * buf_ref[8 + r0:8 + r0 + CHUNK, :]
        for jtap in range(SSD_CONV - 1):
            conv += cw_ref[jtap:jtap + 1, :] * buf_ref[5 + jtap + r0:5 + jtap + r0 + CHUNK, :]
        xc = _silu(conv)
        xc_ref[r0:r0 + CHUNK, :] = xc
        dt_e = acum_ref[r0:r0 + CHUNK, :]
        xdt_ref[r0:r0 + CHUNK, :] = xc[:, :SSD_WIDTH] * dt_e
        dta_e = dt_e * neg_a
        ahi = dta_e.astype(BF16)
        ahi_ref[r0:r0 + CHUNK, :] = ahi
        alo_ref[r0:r0 + CHUNK, :] = (dta_e - ahi.astype(F32)).astype(BF16)
    acum_ref[...] = _nn(tril_ref[...], ahi_ref[...]) + _nn(tril_ref[...], alo_ref[...])

    gw = SSD_WIDTH // SSD_GROUPS
    s_row = lax.broadcasted_iota(I32, (CHUNK, gw), 0)
    s_lane = lax.broadcasted_iota(I32, (CHUNK, gw), 1) & (CHUNK - 1)
    diag = s_row == s_lane
    tril = s_row >= s_lane
    bd_r = lax.broadcasted_iota(I32, (2 * SSD_P, 2 * SSD_P), 0) >> 6
    bd_c = lax.broadcasted_iota(I32, (2 * SSD_P, 2 * SSD_P), 1) >> 6
    blockdiag = bd_r == bd_c

    for c in range(tt // CHUNK):
        rows = slice(c * CHUNK, (c + 1) * CHUNK)
        for g in range(SSD_GROUPS):
            gl = slice(g * gw, (g + 1) * gw)
            bg = xc_ref[rows, SSD_WIDTH + g * SSD_N:SSD_WIDTH + (g + 1) * SSD_N]
            cg = xc_ref[rows, SSD_WIDTH + (SSD_GROUPS + g) * SSD_N:SSD_WIDTH + (SSD_GROUPS + g + 1) * SSD_N]
            bg16, cg16 = bg.astype(BF16), cg.astype(BF16)
            a_g = acum_ref[rows, gl]
            a_last = a_g[CHUNK - 1:CHUNK, :]
            xdt_g = xdt_ref[rows, gl]
            cb = _nt(cg16, jnp.concatenate([bg16] * (gw // CHUNK), axis=0))
            a_row = jnp.sum(jnp.where(diag, a_g, 0.0), axis=0, keepdims=True)
            lmat = jnp.exp(jnp.where(tril, a_g - a_row, -jnp.inf))
            mmat = (cb * lmat).astype(BF16)
            ys = []
            for q in range(gw // (2 * SSD_P)):
                ql = slice(q * 2 * SSD_P, (q + 1) * 2 * SSD_P)
                x2 = xdt_g[:, ql]
                wq = jnp.where(blockdiag, jnp.concatenate([x2, x2], axis=0), 0.0).astype(BF16)
                ys.append(_nn(mmat[:, ql], wq))
            y = jnp.concatenate(ys, axis=1)
            prev = state_ref[g]
            y += _nn(cg16, prev.astype(BF16)) * jnp.exp(a_g)
            y += dsk_ref[:, gl] * xc_ref[rows, gl]
            xdd = (xdt_g * jnp.exp(a_last - a_g)).astype(BF16)
            state_ref[g] = prev * jnp.exp(a_last) + _nn(bg.T.astype(BF16), xdd)
            gz = y * _silu(z_ref[rows, gl])
            gz = gz * lax.rsqrt(jnp.mean(gz * gz, axis=1, keepdims=True) + RMS_EPS)
            o_ref[rows, gl] = (gz * nw_ref[:, gl]).astype(o_ref.dtype)


def _ssd(nat, conv_w, conv_b, dt_bias, a_log, d_skip, norm_w, bsz, seq, tt):
    m = bsz * seq
    nt = seq // tt
    row = lambda b, t: b * nt + t
    full = lambda a: pl.BlockSpec(a.shape, lambda b, t: (0, 0))
    expand = lambda v: jnp.repeat(v.astype(F32), SSD_P).reshape(1, SSD_WIDTH)
    dtb2 = _pad_lanes(jnp.concatenate([dt_bias, dt_bias]))
    j_i = lax.broadcasted_iota(I32, (128, SSD_WIDTH), 0)
    h_i = lax.broadcasted_iota(I32, (128, SSD_WIDTH), 1) // SSD_P
    e2 = ((j_i == h_i) | (j_i == h_i + SSD_HEADS)).astype(BF16)
    r_i = lax.broadcasted_iota(I32, (tt, tt), 0)
    c_i = lax.broadcasted_iota(I32, (tt, tt), 1)
    tril_bd = ((r_i >= c_i) & (r_i // CHUNK == c_i // CHUNK)).astype(BF16)
    return pl.pallas_call(
        functools.partial(_ssd_kernel, tt=tt),
        grid=(bsz, nt),
        in_specs=[pl.BlockSpec((tt, XBC_WIDTH), lambda b, t: (row(b, t), 0)),
                  pl.BlockSpec((tt, SSD_WIDTH), lambda b, t: (row(b, t), XBC_WIDTH // SSD_WIDTH)),
                  pl.BlockSpec((tt, 128), lambda b, t: (row(b, t), (XBC_WIDTH + SSD_WIDTH + ATTN_WIDTH) // 128)),
                  full(conv_w), full(conv_b), full(dtb2), pl.BlockSpec((1, SSD_WIDTH), lambda b, t: (0, 0)),
                  pl.BlockSpec((1, SSD_WIDTH), lambda b, t: (0, 0)), full(norm_w), full(e2), full(tril_bd)],
        out_specs=pl.BlockSpec((tt, SSD_WIDTH), lambda b, t: (row(b, t), 0)),
        out_shape=jax.ShapeDtypeStruct((m, SSD_WIDTH), BF16),
        scratch_shapes=[pltpu.VMEM((tt + 8, XBC_WIDTH), F32), pltpu.VMEM((8, XBC_WIDTH), F32),
                        pltpu.VMEM((SSD_GROUPS, SSD_N, SSD_WIDTH // SSD_GROUPS), F32),
                        pltpu.VMEM((tt, XBC_WIDTH), F32), pltpu.VMEM((tt, SSD_WIDTH), F32),
                        pltpu.VMEM((tt, SSD_WIDTH), F32),
                        pltpu.VMEM((tt, SSD_WIDTH), BF16), pltpu.VMEM((tt, SSD_WIDTH), BF16)],
        compiler_params=pltpu.CompilerParams(dimension_semantics=("arbitrary", "arbitrary"),
                                             vmem_limit_bytes=VMEM_LIMIT),
        name="ssd_mixer",
    )(nat, nat, nat, conv_w, conv_b, dtb2, expand(a_log), expand(d_skip), norm_w, e2, tril_bd)


def _out_kernel(oa_ref, os_ref, x_ref, wa_ref, ws_ref, g_ref, b_ref, o_ref, *, alpha):
    nsplit = 4
    rb = o_ref.shape[0] // nsplit
    for r in range(nsplit):
        rows = slice(r * rb, (r + 1) * rb)
        sub = _nn(oa_ref[rows, :], wa_ref[...]) + _nn(os_ref[rows, :], ws_ref[...])
        y = alpha * x_ref[rows, :] + sub
        mu = jnp.mean(y, axis=1, keepdims=True)
        yc = y - mu
        var = jnp.mean(yc * yc, axis=1, keepdims=True)
        o_ref[rows, :] = yc * lax.rsqrt(var + LN_EPS) * g_ref[...] + b_ref[...]


def _out_proj(oa, os_, x2, wa, ws, g, b, alpha, tm):
    m = x2.shape[0]
    full = lambda a: pl.BlockSpec(a.shape, lambda i: (0, 0))
    rowblk = lambda w: pl.BlockSpec((tm, w), lambda i: (i, 0))
    return pl.pallas_call(
        functools.partial(_out_kernel, alpha=alpha),
        grid=(m // tm,),
        in_specs=[rowblk(ATTN_WIDTH), rowblk(SSD_WIDTH), rowblk(D_MODEL), full(wa), full(ws), full(g), full(b)],
        out_specs=rowblk(D_MODEL),
        out_shape=jax.ShapeDtypeStruct((m, D_MODEL), F32),
        compiler_params=pltpu.CompilerParams(dimension_semantics=("arbitrary",), vmem_limit_bytes=VMEM_LIMIT),
        name="out_proj_ln",
    )(oa, os_, x2, wa, ws, g, b)


def _rope_tables(seq, rot):
    half = rot // 2
    inv = ROPE_THETA ** (-jnp.arange(half, dtype=F32) * 2.0 / rot)
    ang = inv[:, None] * jnp.arange(seq, dtype=F32)[None, :]
    return jnp.cos(ang), jnp.sin(ang)


def _pad_lanes(v, width=128):
    v = v.reshape(1, -1).astype(F32)
    return jnp.pad(v, ((0, 0), (0, width - v.shape[1])))


def _layer(h, w_in, w_out, conv_w, conv_b, dt_bias, a_log, d_skip, norm_w, ln_g, ln_b, alpha):
    bsz, seq, _ = h.shape
    m = bsz * seq
    assert seq % QB == 0 and seq % 512 == 0
    x2 = h.reshape(m, D_MODEL)

    o_q, o_k, o_v, o_za, o_qi, o_ki, o_wi, o_zs, o_xbc, o_dt = (
        0, 1024, 1280, 1536, 2560, 3584, 3648, 3664, 4688, 6736)
    w_all, layer = w_in
    wcol = lambda a, b: w_all[layer, :, a:b]
    wqkv_t = wcol(o_q, o_za).T.astype(BF16)
    widx_t = wcol(o_qi, o_zs).T.astype(BF16)
    wq = wqkv_t[:o_k]
    wkv = wqkv_t[o_k:]
    wqi = widx_t[:o_ki - o_qi]
    wkw = jnp.pad(widx_t[o_ki - o_qi:], ((0, 128 - (o_zs - o_ki)), (0, 0)))
    w_dt = wcol(o_dt, o_dt + SSD_HEADS)
    wn = jnp.concatenate([wcol(o_xbc, o_dt), wcol(o_zs, o_xbc), wcol(o_za, o_qi),
                          jnp.pad(jnp.concatenate([w_dt, w_dt], axis=1), ((0, 0), (0, 128 - 2 * SSD_HEADS)))],
                         axis=1).astype(BF16)

    ca, sa = _rope_tables(seq, HEAD_DIM // ROPE_DIV)
    ci, si = _rope_tables(seq, IDX_DIM // ROPE_DIV)

    qT, k, vT, qiT, kidx, wT, xb = _proj_t(x2, wq, wkv, wqi, wkw, ca, sa, ci, si, seq, tm=512)
    nat = _proj_n(xb, wn, tm=min(1024, m), tn=1408)

    o_attn = _attention(qT, qiT, wT, nat, (XBC_WIDTH + SSD_WIDTH) // ATTN_WIDTH, k, vT, kidx, bsz, seq)
    o_ssd = _ssd(nat, conv_w, conv_b.reshape(1, -1), dt_bias, a_log, d_skip, norm_w.reshape(1, -1),
                 bsz, seq, tt=256)

    wo_all, _ = w_out
    out = _out_proj(o_attn, o_ssd, x2, wo_all[layer, :ATTN_WIDTH].astype(BF16), wo_all[layer, ATTN_WIDTH:].astype(BF16),
                    ln_g.reshape(1, -1), ln_b.reshape(1, -1), alpha, tm=512)
    return out.reshape(bsz, seq, D_MODEL)


def kernel(x, w_in, w_out, conv_w, conv_b, dt_bias, a_log, d_skip, ssd_norm_w, ln_g, ln_b):
    depth = w_in.shape[0]
    alpha = (2.0 * depth) ** 0.25
    h = x
    for layer in range(depth):
        h = _layer(h, (w_in, layer), (w_out, layer), conv_w[layer], conv_b[layer], dt_bias[layer],
                   a_log[layer], d_skip[layer], ssd_norm_w[layer], ln_g[layer], ln_b[layer], alpha)
    return h
```

```python
import functools
import math

import jax
import jax.numpy as jnp
from jax import lax
from jax.experimental import pallas as pl
from jax.experimental.pallas import tpu as pltpu

F32 = jnp.float32
BF16 = jnp.bfloat16
I32 = jnp.int32

D_MODEL = 2048
CHUNK = 64
ATTN_WIDTH = 1024
SSD_WIDTH = 1024
HEAD_DIM = 128
Q_HEADS = 8
KV_HEADS = 2
ROPE_THETA = 500000.0
ROPE_DIV = 4
IDX_HEADS = 16
IDX_DIM = 64
TOPK_MAX = 256
SSD_P = 64
SSD_HEADS = 16
SSD_GROUPS = 4
SSD_N = 128
SSD_CONV = 4
XBC_WIDTH = SSD_WIDTH + 2 * SSD_GROUPS * SSD_N
LN_EPS = 1e-5
RMS_EPS = 1e-5

VMEM_LIMIT = 56 * 1024 * 1024
INT_MIN = -(2 ** 31)

QB = 256
KT = 128
KTA = 256
VPAD = 16
VROWS = HEAD_DIM + VPAD


def _nt(a, b):
    return lax.dot_general(a, b, (((1,), (1,)), ((), ())), preferred_element_type=F32)


def _nn(a, b):
    return jnp.dot(a, b, preferred_element_type=F32)


def _silu(v):
    return v * (1.0 / (1.0 + jnp.exp(-v)))


def _rope_rows(t, cos, sin, nheads, hd, half):
    pieces = []
    for h in range(nheads):
        b = h * hd
        x1 = t[b:b + half]
        x2 = t[b + half:b + 2 * half]
        pieces += [x1 * cos - x2 * sin, x2 * cos + x1 * sin, t[b + 2 * half:b + hd]]
    return jnp.concatenate(pieces, axis=0)


def _proj_t_kernel(x_ref, wq_ref, wkv_ref, wqi_ref, wkw_ref, ca_ref, sa_ref, ci_ref, si_ref,
                   q_ref, k_ref, v_ref, qi_ref, kidx_ref, w_ref, xb_ref):
    x = x_ref[...].astype(BF16)
    xb_ref[...] = x
    ca, sa, ci, si = ca_ref[...], sa_ref[...], ci_ref[...], si_ref[...]
    qscale = HEAD_DIM ** -0.5 * math.log2(math.e)
    tq = _nt(wq_ref[...], x)
    q_ref[...] = (_rope_rows(tq, ca, sa, Q_HEADS, HEAD_DIM, 16) * qscale).astype(BF16)
    tkv = _nt(wkv_ref[...], x)
    k_ref[...] = _rope_rows(tkv[:KV_HEADS * HEAD_DIM], ca, sa, KV_HEADS, HEAD_DIM, 16).T.astype(BF16)
    ones = jnp.ones((VPAD, KTA), BF16)
    for jj in range(v_ref.shape[0]):
        for g in range(KV_HEADS):
            vg = tkv[(KV_HEADS + g) * HEAD_DIM:(KV_HEADS + g + 1) * HEAD_DIM, jj * KTA:(jj + 1) * KTA]
            v_ref[jj, g * VROWS:(g + 1) * VROWS, :] = jnp.concatenate([vg.astype(BF16), ones], axis=0)
    tqi = _nt(wqi_ref[...], x)
    qi_ref[...] = _rope_rows(tqi, ci, si, IDX_HEADS, IDX_DIM, 8).astype(BF16)
    tkw = _nt(wkw_ref[...], x)
    kidx_ref[...] = _rope_rows(tkw[:IDX_DIM], ci, si, 1, IDX_DIM, 8).T.astype(BF16)
    w_ref[...] = tkw[IDX_DIM:IDX_DIM + IDX_HEADS] * (IDX_HEADS ** -0.5 * IDX_DIM ** -0.5)


def _proj_t(x2, wq, wkv, wqi, wkw, ca, sa, ci, si, seq, tm):
    m, kdim = x2.shape
    nper = seq // tm
    full = lambda a: pl.BlockSpec(a.shape, lambda i: (0, 0))
    tab = lambda a: pl.BlockSpec((a.shape[0], tm), lambda i: (0, i % nper))
    cols = lambda r: pl.BlockSpec((r, tm), lambda i: (0, i))
    rows = lambda c: pl.BlockSpec((tm, c), lambda i: (i, 0))
    vspec = pl.BlockSpec((tm // KTA, KV_HEADS * VROWS, KTA), lambda i: (i, 0, 0))
    sds = jax.ShapeDtypeStruct
    return pl.pallas_call(
        _proj_t_kernel,
        grid=(m // tm,),
        in_specs=[pl.BlockSpec((tm, kdim), lambda i: (i, 0)), full(wq), full(wkv), full(wqi), full(wkw),
                  tab(ca), tab(sa), tab(ci), tab(si)],
        out_specs=[cols(1024), rows(256), vspec, cols(1024), rows(IDX_DIM), cols(IDX_HEADS), rows(kdim)],
        out_shape=[sds((1024, m), BF16), sds((m, 256), BF16), sds((m // KTA, KV_HEADS * VROWS, KTA), BF16),
                   sds((1024, m), BF16), sds((m, IDX_DIM), BF16), sds((IDX_HEADS, m), F32), sds((m, kdim), BF16)],
        compiler_params=pltpu.CompilerParams(dimension_semantics=("arbitrary",), vmem_limit_bytes=VMEM_LIMIT),
        name="proj_t",
    )(x2, wq, wkv, wqi, wkw, ca, sa, ci, si)


def _proj_n_kernel(x_ref, w_ref, o_ref):
    o_ref[...] = _nn(x_ref[...], w_ref[...]).astype(o_ref.dtype)


def _proj_n(xb, w, tm, tn):
    m, kdim = xb.shape
    n = w.shape[1]
    return pl.pallas_call(
        _proj_n_kernel,
        grid=(n // tn, m // tm),
        in_specs=[pl.BlockSpec((tm, kdim), lambda j, i: (i, 0)), pl.BlockSpec((kdim, tn), lambda j, i: (0, j))],
        out_specs=pl.BlockSpec((tm, tn), lambda j, i: (i, j)),
        out_shape=jax.ShapeDtypeStruct((m, n), F32),
        compiler_params=pltpu.CompilerParams(dimension_semantics=("arbitrary", "arbitrary"),
                                             vmem_limit_bytes=VMEM_LIMIT),
        name="proj_n",
    )(xb, w)


def _fold64(x, op2):
    parts = [x[a * 64:(a + 1) * 64] for a in range(x.shape[0] // 64)]
    while len(parts) > 1:
        parts = [op2(parts[2 * a], parts[2 * a + 1]) for a in range(len(parts) // 2)]
    return parts[0]


def _fold8_tree(x):
    parts = [x[a * 8:(a + 1) * 8] for a in range(x.shape[0] // 8)]
    while len(parts) > 1:
        paired = [parts[2 * a] + parts[2 * a + 1] for a in range(len(parts) // 2)]
        parts = paired + parts[2 * (len(parts) // 2):]
    return parts[0]


def _bit_transpose32(words):
    a = list(words)
    mask, j = 0x0000FFFF, 16
    while j:
        k = 0
        while k < 32:
            t = (a[k] ^ (a[k + j] >> j)) & mask
            a[k] = a[k] ^ t
            a[k + j] = a[k + j] ^ (t << j)
            k = (k + j + 1) & ~j
        j >>= 1
        mask = (mask ^ (mask << j)) & 0xFFFFFFFF
    return a


def _attn_kernel(qT_ref, qiT_ref, wT_ref, z_ref, k_ref, vT_ref, kidx_ref, o_ref,
                 key_ref, plane_ref, bias_ref, m_ref, l_ref, acc_ref, s_ref, mt_ref, *, topk):
    i = pl.program_id(1)
    nfull = 2 * i
    q_chunk = (i * QB + lax.broadcasted_iota(I32, (1, QB), 1)) // CHUNK

    def score_tile(j):
        rows = pl.ds(pl.multiple_of(j * KT, KT), KT)
        kt = kidx_ref[rows, :]
        acc = jnp.zeros((KT, QB), F32)
        for h in range(IDX_HEADS):
            qh = qiT_ref[h * IDX_DIM:(h + 1) * IDX_DIM, :]
            acc += jnp.maximum(_nn(kt, qh), 0.0) * wT_ref[h:h + 1, :]
        bits = pltpu.bitcast(acc, I32)
        return rows, bits ^ ((bits >> 31) & 0x7FFFFFFF)

    def store_planes(t256, keys):
        words = []
        for kk in keys:
            u = kk ^ INT_MIN
            words += [u[8 * a:8 * a + 8] for a in range(kk.shape[0] // 8)]
        planes = _bit_transpose32(words[::-1])
        prow = pl.ds(pl.multiple_of(t256 * 8, 8), 8)
        for b in range(32):
            plane_ref[b, prow, :] = planes[b]

    def planes_from_keys(t256):
        rows = pl.ds(pl.multiple_of(t256 * 2 * KT, 2 * KT), 2 * KT)
        store_planes(t256, [key_ref[rows, :]])

    for jj in range(QB // KT):
        j = nfull + jj
        rows, key = score_tile(j)
        k_chunk = (j * KT + lax.broadcasted_iota(I32, (KT, 1), 0)) // CHUNK
        key_ref[rows, :] = jnp.where(k_chunk <= q_chunk, key, INT_MIN)

    def key_block(j2):
        planes_from_keys(jnp.where(j2 == 0, i, j2 - 1))
        for jj in range(2):
            rows, key = score_tile(2 * j2 + jj)
            key_ref[rows, :] = key

    def two_blocks(jq, c):
        key_block(2 * jq)
        key_block(2 * jq + 1)
        return c

    def one_block(j2, c):
        key_block(j2)
        return c

    lax.fori_loop(0, i // 2, two_blocks, 0)
    lax.fori_loop(2 * (i // 2), i, one_block, 0)
    planes_from_keys(jnp.where(i == 0, 0, i - 1))

    nprow = plane_ref.shape[1]
    t_row = lax.broadcasted_iota(I32, (nprow, QB), 0) >> 3
    qc_local = lax.broadcasted_iota(I32, (1, QB), 1) // CHUNK
    diag_bits = lax.shift_right_logical(jnp.full((1, QB), -1, I32), 32 - 8 * (qc_local + 1))
    act0 = jnp.where(t_row < i, -1, jnp.where(t_row == i, diag_bits, 0))

    def radix_select(nrow):
        act = act0[:nrow]
        above = jnp.zeros((1, QB), I32)
        thr_u = jnp.zeros((1, QB), I32)
        for b in range(32):
            ones = act & plane_ref[b, 0:nrow, :]
            c8 = _fold8_tree(lax.population_count(ones))
            c = above + jnp.sum(c8, axis=0, keepdims=True)
            take = c >= topk
            act = jnp.where(take, ones, act ^ ones)
            above = jnp.where(take, above, c)
            thr_u = thr_u | jnp.where(take, jnp.int32(-(2 ** 31) if b == 0 else 1 << (31 - b)), 0)
        if nrow < nprow:
            act = jnp.concatenate([act, jnp.zeros((nprow - nrow, QB), I32)], axis=0)
        return thr_u, above, act

    sizes = sorted({max(8, (nprow * (c + 1) // 4) // 8 * 8) for c in range(4)})
    klass = sum(((i + 1) * 8 > s).astype(I32) for s in sizes[:-1])
    thr_u, above, act = lax.switch(klass, [functools.partial(radix_select, s) for s in sizes])
    thr = jnp.maximum(thr_u ^ INT_MIN, INT_MIN + 1)
    need = topk - above
    n_tied = jnp.sum(_fold8_tree(lax.population_count(act)), axis=0, keepdims=True)
    any_tie = jnp.max(jnp.where(n_tied > need, 1, 0)) > 0

    @pl.when(jnp.logical_not(any_tie))
    def _():
        def to_bias(j, c):
            rows = pl.ds(pl.multiple_of(j * KTA, KTA), KTA)
            bias_ref[rows, :] = jnp.where(key_ref[rows, :] >= thr, 0.0, -jnp.inf)
            return c

        lax.fori_loop(0, i + 1, to_bias, 0)

    @pl.when(any_tie)
    def _():
        p_row = lax.broadcasted_iota(I32, (nprow, QB), 0)
        word_bit_clear = {7: 0x0000FFFF, 6: 0x00FF00FF, 5: 0x0F0F0F0F, 4: 0x33333333, 3: 0x55555555}
        cand, left, idx_thr = act, need, jnp.zeros((1, QB), I32)
        for b in range((bias_ref.shape[0] - 1).bit_length() - 1, -1, -1):
            if b >= 8:
                zmask = jnp.where(((p_row >> (3 + b - 8)) & 1) == 0, -1, 0)
            elif b >= 3:
                zmask = word_bit_clear[b]
            else:
                zmask = jnp.where(((p_row >> b) & 1) == 0, -1, 0)
            zeros = cand & zmask
            c0 = jnp.sum(_fold8_tree(lax.population_count(zeros)), axis=0, keepdims=True)
            low = c0 >= left
            cand = jnp.where(low, zeros, cand ^ zeros)
            left = jnp.where(low, left, left - c0)
            idx_thr = idx_thr | jnp.where(low, 0, 1 << b)

        def to_bias_tied(j, c):
            rows = pl.ds(pl.multiple_of(j * KTA, KTA), KTA)
            key = key_ref[rows, :]
            kidx_abs = j * KTA + lax.broadcasted_iota(I32, (KTA, 1), 0)
            keep = (key > thr) | ((key == thr) & (kidx_abs <= idx_thr))
            bias_ref[rows, :] = jnp.where(keep, 0.0, -jnp.inf)
            return c

        lax.fori_loop(0, i + 1, to_bias_tied, 0)

    m_ref[...] = jnp.full(m_ref.shape, -jnp.inf, F32)
    l_ref[...] = jnp.zeros(l_ref.shape, F32)
    acc_ref[...] = jnp.zeros(acc_ref.shape, F32)

    rep = Q_HEADS // KV_HEADS
    gq = rep * QB
    ntile = bias_ref.shape[0] // KTA
    jpad = jnp.minimum(i + 1, ntile - 1)

    @pl.when(i + 1 < ntile)
    def _():
        bias_ref[pl.ds(pl.multiple_of((i + 1) * KTA, KTA), KTA), :] = jnp.full((KTA, QB), -jnp.inf, F32)

    def masked_logits(j, slot):
        jc = jnp.minimum(j, jpad)
        rows = pl.ds(pl.multiple_of(jc * KTA, KTA), KTA)
        for g in range(KV_HEADS):
            qg = jnp.concatenate([qT_ref[(g * rep + r) * HEAD_DIM:(g * rep + r + 1) * HEAD_DIM, :]
                                  for r in range(rep)], axis=1)
            sg = _nn(k_ref[rows, g * HEAD_DIM:(g + 1) * HEAD_DIM], qg)
            for r in range(rep):
                h = g * rep + r
                sh = sg[:, r * QB:(r + 1) * QB] + bias_ref[rows, :]
                s_ref[slot, :, h * QB:(h + 1) * QB] = sh
                mt_ref[slot, h:h + 1, :] = jnp.max(_fold64(sh, jnp.maximum), axis=0, keepdims=True)

    def softmax_pv(j, slot):
        jc = jnp.minimum(j, jpad)
        for h in range(Q_HEADS):
            g, r = divmod(h, rep)
            hl = slice(r * QB, (r + 1) * QB)
            s = s_ref[slot, :, h * QB:(h + 1) * QB]
            m = m_ref[g:g + 1, hl]
            m_new = jnp.maximum(m, mt_ref[slot, h:h + 1, :])
            m_safe = jnp.where(m_new == -jnp.inf, 0.0, m_new)
            alpha = jnp.exp2(m - m_safe)
            p = jnp.exp2(s - m_safe)
            pv = _nn(vT_ref[jc, g * VROWS:(g + 1) * VROWS, :], p.astype(BF16))
            l_ref[g:g + 1, hl] = alpha * l_ref[g:g + 1, hl] + pv[HEAD_DIM:HEAD_DIM + 1, :]
            acc_ref[g, :, hl] = alpha * acc_ref[g, :, hl] + pv[:HEAD_DIM]
            m_ref[g:g + 1, hl] = m_new

    masked_logits(0, 0)

    def kv_pair(jj):
        a = 2 * jj
        masked_logits(a + 1, 1)
        softmax_pv(a, 0)
        masked_logits(a + 2, 0)
        softmax_pv(a + 1, 1)

    def kv_two_pairs(jq, c):
        kv_pair(2 * jq)
        kv_pair(2 * jq + 1)
        return c

    def kv_one_pair(jj, c):
        kv_pair(jj)
        return c

    npairs = (i + 2) // 2
    lax.fori_loop(0, npairs // 2, kv_two_pairs, 0)
    lax.fori_loop(2 * (npairs // 2), npairs, kv_one_pair, 0)
    for h in range(Q_HEADS):
        g, r = divmod(h, rep)
        inv_l = 1.0 / l_ref[g:g + 1, r * QB:(r + 1) * QB]
        oh = (acc_ref[g, :, r * QB:(r + 1) * QB] * inv_l).T
        zh = z_ref[:, h * HEAD_DIM:(h + 1) * HEAD_DIM]
        o_ref[:, h * HEAD_DIM:(h + 1) * HEAD_DIM] = (oh * _silu(zh)).astype(o_ref.dtype)


def _attention(qT, qiT, wT, znat, zcol, k, vT, kidx, bsz, seq):
    nq = seq // QB
    topk = min(TOPK_MAX, seq // 4)
    m = bsz * seq
    step = lambda r: pl.BlockSpec((r, QB), lambda b, i: (0, b * nq + i))
    return pl.pallas_call(
        functools.partial(_attn_kernel, topk=topk),
        grid=(bsz, nq),
        in_specs=[step(1024), step(1024), step(IDX_HEADS),
                  pl.BlockSpec((QB, ATTN_WIDTH), lambda b, i: (b * nq + i, zcol)),
                  pl.BlockSpec((seq, 256), lambda b, i: (b, 0)),
                  pl.BlockSpec((seq // KTA, KV_HEADS * VROWS, KTA), lambda b, i: (b, 0, 0)),
                  pl.BlockSpec((seq, IDX_DIM), lambda b, i: (b, 0))],
        out_specs=pl.BlockSpec((QB, ATTN_WIDTH), lambda b, i: (b * nq + i, 0)),
        out_shape=jax.ShapeDtypeStruct((m, ATTN_WIDTH), BF16),
        scratch_shapes=[pltpu.VMEM((seq, QB), I32), pltpu.VMEM((32, seq // 32, QB), I32),
                        pltpu.VMEM((seq, QB), F32),
                        pltpu.VMEM((KV_HEADS, Q_HEADS // KV_HEADS * QB), F32),
                        pltpu.VMEM((KV_HEADS, Q_HEADS // KV_HEADS * QB), F32),
                        pltpu.VMEM((KV_HEADS, HEAD_DIM, Q_HEADS // KV_HEADS * QB), F32),
                        pltpu.VMEM((2, KTA, Q_HEADS * QB), F32), pltpu.VMEM((2, Q_HEADS, QB), F32)],
        compiler_params=pltpu.CompilerParams(dimension_semantics=("arbitrary", "arbitrary"),
                                             vmem_limit_bytes=VMEM_LIMIT),
        name="dsa_attention",
    )(qT, qiT, wT, znat, k, vT, kidx)


def _ssd_kernel(xbc_ref, z_ref, dt_ref, cw_ref, cb_ref, dtb_ref, a_ref, dsk_ref, nw_ref, e2_ref, tril_ref, o_ref,
                buf_ref, tail_ref, state_ref, xc_ref, xdt_ref, acum_ref, ahi_ref, alo_ref, *, tt):
    t = pl.program_id(1)

    @pl.when(t == 0)
    def _():
        tail_ref[...] = jnp.zeros_like(tail_ref)
        state_ref[...] = jnp.zeros_like(state_ref)

    buf_ref[0:8, :] = tail_ref[...]
    buf_ref[8:8 + tt, :] = xbc_ref[...]
    tail_ref[...] = xbc_ref[tt - 8:tt, :]

    dtv = dt_ref[...] + dtb_ref[...]
    dtv = jnp.maximum(dtv, 0.0) + jnp.log(1.0 + jnp.exp(-jnp.abs(dtv)))
    dhi = dtv.astype(BF16).astype(F32)
    lane = lax.broadcasted_iota(I32, (1, 128), 1)
    dsplit = jnp.where(lane < SSD_HEADS, dhi, dtv - dhi).astype(BF16)
    acum_ref[...] = _nn(dsplit, e2_ref[...])
    neg_a = -jnp.exp(a_ref[...])

    for c in range(tt // CHUNK):
        r0 = c * CHUNK
        conv = cb_ref[...] + cw_ref[3:4, :] * buf_ref[8 + r0:8 + r0 + CHUNK, :]
        for jtap in range(SSD_CONV - 1):
            conv += cw_ref[jtap:jtap + 1, :] * buf_ref[5 + jtap + r0:5 + jtap + r0 + CHUNK, :]
        xc = _silu(conv)
        xc_ref[r0:r0 + CHUNK, :] = xc
        dt_e = acum_ref[r0:r0 + CHUNK, :]
        xdt_ref[r0:r0 + CHUNK, :] = xc[:, :SSD_WIDTH] * dt_e
        dta_e = dt_e * neg_a
        ahi = dta_e.astype(BF16)
        ahi_ref[r0:r0 + CHUNK, :] = ahi
        alo_ref[r0:r0 + CHUNK, :] = (dta_e - ahi.astype(F32)).astype(BF16)
    acum_ref[...] = _nn(tril_ref[...], ahi_ref[...]) + _nn(tril_ref[...], alo_ref[...])

    gw = SSD_WIDTH // SSD_GROUPS
    s_row = lax.broadcasted_iota(I32, (CHUNK, gw), 0)
    s_lane = lax.broadcasted_iota(I32, (CHUNK, gw), 1) & (CHUNK - 1)
    diag = s_row == s_lane
    tril = s_row >= s_lane
    bd_r = lax.broadcasted_iota(I32, (2 * SSD_P, 2 * SSD_P), 0) >> 6
    bd_c = lax.broadcasted_iota(I32, (2 * SSD_P, 2 * SSD_P), 1) >> 6
    blockdiag = bd_r == bd_c

    for c in range(tt // CHUNK):
        rows = slice(c * CHUNK, (c + 1) * CHUNK)
        for g in range(SSD_GROUPS):
            gl = slice(g * gw, (g + 1) * gw)
            bg = xc_ref[rows, SSD_WIDTH + g * SSD_N:SSD_WIDTH + (g + 1) * SSD_N]
            cg = xc_ref[rows, SSD_WIDTH + (SSD_GROUPS + g) * SSD_N:SSD_WIDTH + (SSD_GROUPS + g + 1) * SSD_N]
            bg16, cg16 = bg.astype(BF16), cg.astype(BF16)
            a_g = acum_ref[rows, gl]
            a_last = a_g[CHUNK - 1:CHUNK, :]
            xdt_g = xdt_ref[rows, gl]
            cb = _nt(cg16, jnp.concatenate([bg16] * (gw // CHUNK), axis=0))
            a_row = jnp.sum(jnp.where(diag, a_g, 0.0), axis=0, keepdims=True)
            lmat = jnp.exp(jnp.where(tril, a_g - a_row, -jnp.inf))
            mmat = (cb * lmat).astype(BF16)
            ys = []
            for q in range(gw // (2 * SSD_P)):
                ql = slice(q * 2 * SSD_P, (q + 1) * 2 * SSD_P)
                x2 = xdt_g[:, ql]
                wq = jnp.where(blockdiag, jnp.concatenate([x2, x2], axis=0), 0.0).astype(BF16)
                ys.append(_nn(mmat[:, ql], wq))
            y = jnp.concatenate(ys, axis=1)
            prev = state_ref[g]
            y += _nn(cg16, prev.astype(BF16)) * jnp.exp(a_g)
            y += dsk_ref[:, gl] * xc_ref[rows, gl]
            xdd = (xdt_g * jnp.exp(a_last - a_g)).astype(BF16)
            state_ref[g] = prev * jnp.exp(a_last) + _nn(bg.T.astype(BF16), xdd)
            gz = y * _silu(z_ref[rows, gl])
            gz = gz * lax.rsqrt(jnp.mean(gz * gz, axis=1, keepdims=True) + RMS_EPS)
            o_ref[rows, gl] = (gz * nw_ref[:, gl]).astype(o_ref.dtype)


def _ssd(nat, conv_w, conv_b, dt_bias, a_log, d_skip, norm_w, bsz, seq, tt):
    m = bsz * seq
    nt = seq // tt
    row = lambda b, t: b * nt + t
    full = lambda a: pl.BlockSpec(a.shape, lambda b, t: (0, 0))
    expand = lambda v: jnp.repeat(v.astype(F32), SSD_P).reshape(1, SSD_WIDTH)
    dtb2 = _pad_lanes(jnp.concatenate([dt_bias, dt_bias]))
    j_i = lax.broadcasted_iota(I32, (128, SSD_WIDTH), 0)
    h_i = lax.broadcasted_iota(I32, (128, SSD_WIDTH), 1) // SSD_P
    e2 = ((j_i == h_i) | (j_i == h_i + SSD_HEADS)).astype(BF16)
    r_i = lax.broadcasted_iota(I32, (tt, tt), 0)
    c_i = lax.broadcasted_iota(I32, (tt, tt), 1)
    tril_bd = ((r_i >= c_i) & (r_i // CHUNK == c_i // CHUNK)).astype(BF16)
    return pl.pallas_call(
        functools.partial(_ssd_kernel, tt=tt),
        grid=(bsz, nt),
        in_specs=[pl.BlockSpec((tt, XBC_WIDTH), lambda b, t: (row(b, t), 0)),
                  pl.BlockSpec((tt, SSD_WIDTH), lambda b, t: (row(b, t), XBC_WIDTH // SSD_WIDTH)),
                  pl.BlockSpec((tt, 128), lambda b, t: (row(b, t), (XBC_WIDTH + SSD_WIDTH + ATTN_WIDTH) // 128)),
                  full(conv_w), full(conv_b), full(dtb2), pl.BlockSpec((1, SSD_WIDTH), lambda b, t: (0, 0)),
                  pl.BlockSpec((1, SSD_WIDTH), lambda b, t: (0, 0)), full(norm_w), full(e2), full(tril_bd)],
        out_specs=pl.BlockSpec((tt, SSD_WIDTH), lambda b, t: (row(b, t), 0)),
        out_shape=jax.ShapeDtypeStruct((m, SSD_WIDTH), BF16),
        scratch_shapes=[pltpu.VMEM((tt + 8, XBC_WIDTH), F32), pltpu.VMEM((8, XBC_WIDTH), F32),
                        pltpu.VMEM((SSD_GROUPS, SSD_N, SSD_WIDTH // SSD_GROUPS), F32),
                        pltpu.VMEM((tt, XBC_WIDTH), F32), pltpu.VMEM((tt, SSD_WIDTH), F32),
                        pltpu.VMEM((tt, SSD_WIDTH), F32),
                        pltpu.VMEM((tt, SSD_WIDTH), BF16), pltpu.VMEM((tt, SSD_WIDTH), BF16)],
        compiler_params=pltpu.CompilerParams(dimension_semantics=("arbitrary", "arbitrary"),
                                             vmem_limit_bytes=VMEM_LIMIT),
        name="ssd_mixer",
    )(nat, nat, nat, conv_w, conv_b, dtb2, expand(a_log), expand(d_skip), norm_w, e2, tril_bd)


def _out_kernel(oa_ref, os_ref, x_ref, wa_ref, ws_ref, g_ref, b_ref, o_ref, *, alpha):
    nsplit = 4
    rb = o_ref.shape[0] // nsplit
    for r in range(nsplit):
        rows = slice(r * rb, (r + 1) * rb)
        sub = _nn(oa_ref[rows, :], wa_ref[...]) + _nn(os_ref[rows, :], ws_ref[...])
        y = alpha * x_ref[rows, :] + sub
        mu = jnp.mean(y, axis=1, keepdims=True)
        yc = y - mu
        var = jnp.mean(yc * yc, axis=1, keepdims=True)
        o_ref[rows, :] = yc * lax.rsqrt(var + LN_EPS) * g_ref[...] + b_ref[...]


def _out_proj(oa, os_, x2, wa, ws, g, b, alpha, tm):
    m = x2.shape[0]
    full = lambda a: pl.BlockSpec(a.shape, lambda i: (0, 0))
    rowblk = lambda w: pl.BlockSpec((tm, w), lambda i: (i, 0))
    return pl.pallas_call(
        functools.partial(_out_kernel, alpha=alpha),
        grid=(m // tm,),
        in_specs=[rowblk(ATTN_WIDTH), rowblk(SSD_WIDTH), rowblk(D_MODEL), full(wa), full(ws), full(g), full(b)],
        out_specs=rowblk(D_MODEL),
        out_shape=jax.ShapeDtypeStruct((m, D_MODEL), F32),
        compiler_params=pltpu.CompilerParams(dimension_semantics=("arbitrary",), vmem_limit_bytes=VMEM_LIMIT),
        name="out_proj_ln",
    )(oa, os_, x2, wa, ws, g, b)


def _rope_tables(seq, rot):
    half = rot // 2
    inv = ROPE_THETA ** (-jnp.arange(half, dtype=F32) * 2.0 / rot)
    ang = inv[:, None] * jnp.arange(seq, dtype=F32)[None, :]
    return jnp.cos(ang), jnp.sin(ang)


def _pad_lanes(v, width=128):
    v = v.reshape(1, -1).astype(F32)
    return jnp.pad(v, ((0, 0), (0, width - v.shape[1])))


def _layer(h, w_in, w_out, conv_w, conv_b, dt_bias, a_log, d_skip, norm_w, ln_g, ln_b, alpha):
    bsz, seq, _ = h.shape
    m = bsz * seq
    assert seq % QB == 0 and seq % 512 == 0
    x2 = h.reshape(m, D_MODEL)

    o_q, o_k, o_v, o_za, o_qi, o_ki, o_wi, o_zs, o_xbc, o_dt = (
        0, 1024, 1280, 1536, 2560, 3584, 3648, 3664, 4688, 6736)
    w_all, layer = w_in
    wcol = lambda a, b: w_all[layer, :, a:b]
    wqkv_t = wcol(o_q, o_za).T.astype(BF16)
    widx_t = wcol(o_qi, o_zs).T.astype(BF16)
    wq = wqkv_t[:o_k]
    wkv = wqkv_t[o_k:]
    wqi = widx_t[:o_ki - o_qi]
    wkw = jnp.pad(widx_t[o_ki - o_qi:], ((0, 128 - (o_zs - o_ki)), (0, 0)))
    w_dt = wcol(o_dt, o_dt + SSD_HEADS)
    wn = jnp.concatenate([wcol(o_xbc, o_dt), wcol(o_zs, o_xbc), wcol(o_za, o_qi),
                          jnp.pad(jnp.concatenate([w_dt, w_dt], axis=1), ((0, 0), (0, 128 - 2 * SSD_HEADS)))],
                         axis=1).astype(BF16)

    ca, sa = _rope_tables(seq, HEAD_DIM // ROPE_DIV)
    ci, si = _rope_tables(seq, IDX_DIM // ROPE_DIV)

    qT, k, vT, qiT, kidx, wT, xb = _proj_t(x2, wq, wkv, wqi, wkw, ca, sa, ci, si, seq, tm=512)
    nat = _proj_n(xb, wn, tm=min(1024, m), tn=1408)

    o_attn = _attention(qT, qiT, wT, nat, (XBC_WIDTH + SSD_WIDTH) // ATTN_WIDTH, k, vT, kidx, bsz, seq)
    o_ssd = _ssd(nat, conv_w, conv_b.reshape(1, -1), dt_bias, a_log, d_skip, norm_w.reshape(1, -1),
                 bsz, seq, tt=256)

    wo_all, _ = w_out
    out = _out_proj(o_attn, o_ssd, x2, wo_all[layer, :ATTN_WIDTH].astype(BF16), wo_all[layer, ATTN_WIDTH:].astype(BF16),
                    ln_g.reshape(1, -1), ln_b.reshape(1, -1), alpha, tm=512)
    return out.reshape(bsz, seq, D_MODEL)


def kernel(x, w_in, w_out, conv_w, conv_b, dt_bias, a_log, d_skip, ssd_norm_w, ln_g, ln_b):
    depth = w_in.shape[0]
    alpha = (2.0 * depth) ** 0.25
    h = x
    for layer in range(depth):
        h = _layer(h, (w_in, layer), (w_out, layer), conv_w[layer], conv_b[layer], dt_bias[layer],
                   a_log[layer], d_skip[layer], ssd_norm_w[layer], ln_g[layer], ln_b[layer], alpha)
    return h
```

```python
import functools
import math

import jax
import jax.numpy as jnp
from jax import lax
from jax.experimental import pallas as pl
from jax.experimental.pallas import tpu as pltpu

F32 = jnp.float32
BF16 = jnp.bfloat16
I32 = jnp.int32

D_MODEL = 2048
CHUNK = 64
ATTN_WIDTH = 1024
SSD_WIDTH = 1024
HEAD_DIM = 128
Q_HEADS = 8
KV_HEADS = 2
ROPE_THETA = 500000.0
ROPE_DIV = 4
IDX_HEADS = 16
IDX_DIM = 64
TOPK_MAX = 256
SSD_P = 64
SSD_HEADS = 16
SSD_GROUPS = 4
SSD_N = 128
SSD_CONV = 4
XBC_WIDTH = SSD_WIDTH + 2 * SSD_GROUPS * SSD_N
LN_EPS = 1e-5
RMS_EPS = 1e-5

VMEM_LIMIT = 56 * 1024 * 1024
INT_MIN = -(2 ** 31)

QB = 256
KT = 128
KTA = 256
VPAD = 16
VROWS = HEAD_DIM + VPAD


def _nt(a, b):
    return lax.dot_general(a, b, (((1,), (1,)), ((), ())), preferred_element_type=F32)


def _nn(a, b):
    return jnp.dot(a, b, preferred_element_type=F32)


def _silu(v):
    return v * (1.0 / (1.0 + jnp.exp(-v)))


def _rope_rows(t, cos, sin, nheads, hd, half):
    pieces = []
    for h in range(nheads):
        b = h * hd
        x1 = t[b:b + half]
        x2 = t[b + half:b + 2 * half]
        pieces += [x1 * cos - x2 * sin, x2 * cos + x1 * sin, t[b + 2 * half:b + hd]]
    return jnp.concatenate(pieces, axis=0)


def _proj_t_kernel(x_ref, wq_ref, wkv_ref, wqi_ref, wkw_ref, ca_ref, sa_ref, ci_ref, si_ref,
                   q_ref, k_ref, v_ref, qi_ref, kidx_ref, w_ref, xb_ref):
    x = x_ref[...].astype(BF16)
    xb_ref[...] = x
    ca, sa, ci, si = ca_ref[...], sa_ref[...], ci_ref[...], si_ref[...]
    qscale = HEAD_DIM ** -0.5 * math.log2(math.e)
    tq = _nt(wq_ref[...], x)
    q_ref[...] = (_rope_rows(tq, ca, sa, Q_HEADS, HEAD_DIM, 16) * qscale).astype(BF16)
    tkv = _nt(wkv_ref[...], x)
    k_ref[...] = _rope_rows(tkv[:KV_HEADS * HEAD_DIM], ca, sa, KV_HEADS, HEAD_DIM, 16).T.astype(BF16)
    ones = jnp.ones((VPAD, KTA), BF16)
    for jj in range(v_ref.shape[0]):
        for g in range(KV_HEADS):
            vg = tkv[(KV_HEADS + g) * HEAD_DIM:(KV_HEADS + g + 1) * HEAD_DIM, jj * KTA:(jj + 1) * KTA]
            v_ref[jj, g * VROWS:(g + 1) * VROWS, :] = jnp.concatenate([vg.astype(BF16), ones], axis=0)
    tqi = _nt(wqi_ref[...], x)
    qi_ref[...] = _rope_rows(tqi, ci, si, IDX_HEADS, IDX_DIM, 8).astype(BF16)
    tkw = _nt(wkw_ref[...], x)
    kidx_ref[...] = _rope_rows(tkw[:IDX_DIM], ci, si, 1, IDX_DIM, 8).T.astype(BF16)
    w_ref[...] = tkw[IDX_DIM:IDX_DIM + IDX_HEADS] * (IDX_HEADS ** -0.5 * IDX_DIM ** -0.5)


def _proj_t(x2, wq, wkv, wqi, wkw, ca, sa, ci, si, seq, tm):
    m, kdim = x2.shape
    nper = seq // tm
    full = lambda a: pl.BlockSpec(a.shape, lambda i: (0, 0))
    tab = lambda a: pl.BlockSpec((a.shape[0], tm), lambda i: (0, i % nper))
    cols = lambda r: pl.BlockSpec((r, tm), lambda i: (0, i))
    rows = lambda c: pl.BlockSpec((tm, c), lambda i: (i, 0))
    vspec = pl.BlockSpec((tm // KTA, KV_HEADS * VROWS, KTA), lambda i: (i, 0, 0))
    sds = jax.ShapeDtypeStruct
    return pl.pallas_call(
        _proj_t_kernel,
        grid=(m // tm,),
        in_specs=[pl.BlockSpec((tm, kdim), lambda i: (i, 0)), full(wq), full(wkv), full(wqi), full(wkw),
                  tab(ca), tab(sa), tab(ci), tab(si)],
        out_specs=[cols(1024), rows(256), vspec, cols(1024), rows(IDX_DIM), cols(IDX_HEADS), rows(kdim)],
        out_shape=[sds((1024, m), BF16), sds((m, 256), BF16), sds((m // KTA, KV_HEADS * VROWS, KTA), BF16),
                   sds((1024, m), BF16), sds((m, IDX_DIM), BF16), sds((IDX_HEADS, m), F32), sds((m, kdim), BF16)],
        compiler_params=pltpu.CompilerParams(dimension_semantics=("arbitrary",), vmem_limit_bytes=VMEM_LIMIT),
        name="proj_t",
    )(x2, wq, wkv, wqi, wkw, ca, sa, ci, si)


def _proj_n_kernel(x_ref, w_ref, o_ref):
    o_ref[...] = _nn(x_ref[...], w_ref[...]).astype(o_ref.dtype)


def _proj_n(xb, w, tm, tn):
    m, kdim = xb.shape
    n = w.shape[1]
    return pl.pallas_call(
        _proj_n_kernel,
        grid=(n // tn, m // tm),
        in_specs=[pl.BlockSpec((tm, kdim), lambda j, i: (i, 0)), pl.BlockSpec((kdim, tn), lambda j, i: (0, j))],
        out_specs=pl.BlockSpec((tm, tn), lambda j, i: (i, j)),
        out_shape=jax.ShapeDtypeStruct((m, n), F32),
        compiler_params=pltpu.CompilerParams(dimension_semantics=("arbitrary", "arbitrary"),
                                             vmem_limit_bytes=VMEM_LIMIT),
        name="proj_n",
    )(xb, w)


def _fold64(x, op2):
    parts = [x[a * 64:(a + 1) * 64] for a in range(x.shape[0] // 64)]
    while len(parts) > 1:
        parts = [op2(parts[2 * a], parts[2 * a + 1]) for a in range(len(parts) // 2)]
    return parts[0]


def _fold8_tree(x):
    parts = [x[a * 8:(a + 1) * 8] for a in range(x.shape[0] // 8)]
    while len(parts) > 1:
        paired = [parts[2 * a] + parts[2 * a + 1] for a in range(len(parts) // 2)]
        parts = paired + parts[2 * (len(parts) // 2):]
    return parts[0]


def _bit_transpose32(words):
    a = list(words)
    mask, j = 0x0000FFFF, 16
    while j:
        k = 0
        while k < 32:
            t = (a[k] ^ (a[k + j] >> j)) & mask
            a[k] = a[k] ^ t
            a[k + j] = a[k + j] ^ (t << j)
            k = (k + j + 1) & ~j
        j >>= 1
        mask = (mask ^ (mask << j)) & 0xFFFFFFFF
    return a


def _attn_kernel(qT_ref, qiT_ref, wT_ref, z_ref, k_ref, vT_ref, kidx_ref, o_ref,
                 key_ref, plane_ref, bias_ref, m_ref, l_ref, acc_ref, s_ref, mt_ref, *, topk):
    i = pl.program_id(1)
    nfull = 2 * i
    q_chunk = (i * QB + lax.broadcasted_iota(I32, (1, QB), 1)) // CHUNK

    def score_tile(j):
        rows = pl.ds(pl.multiple_of(j * KT, KT), KT)
        kt = kidx_ref[rows, :]
        acc = jnp.zeros((KT, QB), F32)
        for h in range(IDX_HEADS):
            qh = qiT_ref[h * IDX_DIM:(h + 1) * IDX_DIM, :]
            acc += jnp.maximum(_nn(kt, qh), 0.0) * wT_ref[h:h + 1, :]
        bits = pltpu.bitcast(acc, I32)
        return rows, bits ^ ((bits >> 31) & 0x7FFFFFFF)

    def store_planes(t256, keys):
        words = []
        for kk in keys:
            u = kk ^ INT_MIN
            words += [u[8 * a:8 * a + 8] for a in range(kk.shape[0] // 8)]
        planes = _bit_transpose32(words[::-1])
        prow = pl.ds(pl.multiple_of(t256 * 8, 8), 8)
        for b in range(32):
            plane_ref[b, prow, :] = planes[b]

    def planes_from_keys(t256):
        rows = pl.ds(pl.multiple_of(t256 * 2 * KT, 2 * KT), 2 * KT)
        store_planes(t256, [key_ref[rows, :]])

    @pl.when((pl.program_id(0) == 0) & (i == 0))
    def _():
        plane_ref[...] = jnp.zeros(plane_ref.shape, I32)

    for jj in range(QB // KT):
        j = nfull + jj
        rows, key = score_tile(j)
        k_chunk = (j * KT + lax.broadcasted_iota(I32, (KT, 1), 0)) // CHUNK
        key_ref[rows, :] = jnp.where(k_chunk <= q_chunk, key, INT_MIN)

    def key_block(j2):
        planes_from_keys(jnp.where(j2 == 0, i, j2 - 1))
        for jj in range(2):
            rows, key = score_tile(2 * j2 + jj)
            key_ref[rows, :] = key

    def two_blocks(jq, c):
        key_block(2 * jq)
        key_block(2 * jq + 1)
        return c

    def one_block(j2, c):
        key_block(j2)
        return c

    lax.fori_loop(0, i // 2, two_blocks, 0)
    lax.fori_loop(2 * (i // 2), i, one_block, 0)
    planes_from_keys(jnp.where(i == 0, 0, i - 1))

    nprow = plane_ref.shape[1]
    t_row = lax.broadcasted_iota(I32, (nprow, QB), 0) >> 3
    qc_local = lax.broadcasted_iota(I32, (1, QB), 1) // CHUNK
    diag_bits = lax.shift_right_logical(jnp.full((1, QB), -1, I32), 32 - 8 * (qc_local + 1))
    act0 = jnp.where(t_row < i, -1, jnp.where(t_row == i, diag_bits, 0))

    def radix_select(nrow):
        act = act0[:nrow]
        above = jnp.zeros((1, QB), I32)
        thr_u = jnp.zeros((1, QB), I32)
        for b in range(32):
            ones = act & plane_ref[b, 0:nrow, :]
            c8 = _fold8_tree(lax.population_count(ones))
            c = above + jnp.sum(c8, axis=0, keepdims=True)
            take = c >= topk
            act = jnp.where(take, ones, act ^ ones)
            above = jnp.where(take, above, c)
            thr_u = thr_u | jnp.where(take, jnp.int32(-(2 ** 31) if b == 0 else 1 << (31 - b)), 0)
        if nrow < nprow:
            act = jnp.concatenate([act, jnp.zeros((nprow - nrow, QB), I32)], axis=0)
        return thr_u, above, act

    sizes = sorted({max(8, (nprow * (c + 1) // 4) // 8 * 8) for c in range(4)})
    klass = sum(((i + 1) * 8 > s).astype(I32) for s in sizes[:-1])
    thr_u, above, act = lax.switch(klass, [functools.partial(radix_select, s) for s in sizes])
    thr = jnp.maximum(thr_u ^ INT_MIN, INT_MIN + 1)
    need = topk - above
    n_tied = jnp.sum(_fold8_tree(lax.population_count(act)), axis=0, keepdims=True)
    any_tie = jnp.max(jnp.where(n_tied > need, 1, 0)) > 0

    @pl.when(jnp.logical_not(any_tie))
    def _():
        def to_bias(j, c):
            rows = pl.ds(pl.multiple_of(j * KTA, KTA), KTA)
            bias_ref[rows, :] = jnp.where(key_ref[rows, :] >= thr, 0.0, -jnp.inf)
            return c

        lax.fori_loop(0, i + 1, to_bias, 0)

    @pl.when(any_tie)
    def _():
        p_row = lax.broadcasted_iota(I32, (nprow, QB), 0)
        word_bit_clear = {7: 0x0000FFFF, 6: 0x00FF00FF, 5: 0x0F0F0F0F, 4: 0x33333333, 3: 0x55555555}
        cand, left, idx_thr = act, need, jnp.zeros((1, QB), I32)
        for b in range((bias_ref.shape[0] - 1).bit_length() - 1, -1, -1):
            if b >= 8:
                zmask = jnp.where(((p_row >> (3 + b - 8)) & 1) == 0, -1, 0)
            elif b >= 3:
                zmask = word_bit_clear[b]
            else:
                zmask = jnp.where(((p_row >> b) & 1) == 0, -1, 0)
            zeros = cand & zmask
            c0 = jnp.sum(_fold8_tree(lax.population_count(zeros)), axis=0, keepdims=True)
            low = c0 >= left
            cand = jnp.where(low, zeros, cand ^ zeros)
            left = jnp.where(low, left, left - c0)
            idx_thr = idx_thr | jnp.where(low, 0, 1 << b)

        def to_bias_tied(j, c):
            rows = pl.ds(pl.multiple_of(j * KTA, KTA), KTA)
            key = key_ref[rows, :]
            kidx_abs = j * KTA + lax.broadcasted_iota(I32, (KTA, 1), 0)
            keep = (key > thr) | ((key == thr) & (kidx_abs <= idx_thr))
            bias_ref[rows, :] = jnp.where(keep, 0.0, -jnp.inf)
            return c

        lax.fori_loop(0, i + 1, to_bias_tied, 0)

    m_ref[...] = jnp.full(m_ref.shape, -jnp.inf, F32)
    l_ref[...] = jnp.zeros(l_ref.shape, F32)
    acc_ref[...] = jnp.zeros(acc_ref.shape, F32)

    rep = Q_HEADS // KV_HEADS
    gq = rep * QB
    ntile = bias_ref.shape[0] // KTA
    jpad = jnp.minimum(i + 1, ntile - 1)

    @pl.when(i + 1 < ntile)
    def _():
        bias_ref[pl.ds(pl.multiple_of((i + 1) * KTA, KTA), KTA), :] = jnp.full((KTA, QB), -jnp.inf, F32)

    def masked_logits(j, slot):
        jc = jnp.minimum(j, jpad)
        rows = pl.ds(pl.multiple_of(jc * KTA, KTA), KTA)
        for g in range(KV_HEADS):
            qg = jnp.concatenate([qT_ref[(g * rep + r) * HEAD_DIM:(g * rep + r + 1) * HEAD_DIM, :]
                                  for r in range(rep)], axis=1)
            sg = _nn(k_ref[rows, g * HEAD_DIM:(g + 1) * HEAD_DIM], qg)
            for r in range(rep):
                h = g * rep + r
                sh = sg[:, r * QB:(r + 1) * QB] + bias_ref[rows, :]
                s_ref[slot, :, h * QB:(h + 1) * QB] = sh
                mt_ref[slot, h:h + 1, :] = jnp.max(_fold64(sh, jnp.maximum), axis=0, keepdims=True)

    def softmax_pv(j, slot):
        jc = jnp.minimum(j, jpad)
        for h in range(Q_HEADS):
            g, r = divmod(h, rep)
            hl = slice(r * QB, (r + 1) * QB)
            s = s_ref[slot, :, h * QB:(h + 1) * QB]
            m = m_ref[g:g + 1, hl]
            m_new = jnp.maximum(m, mt_ref[slot, h:h + 1, :])
            m_safe = jnp.where(m_new == -jnp.inf, 0.0, m_new)
            alpha = jnp.exp2(m - m_safe)
            p = jnp.exp2(s - m_safe)
            pv = _nn(vT_ref[jc, g * VROWS:(g + 1) * VROWS, :], p.astype(BF16))
            l_ref[g:g + 1, hl] = alpha * l_ref[g:g + 1, hl] + pv[HEAD_DIM:HEAD_DIM + 1, :]
            acc_ref[g, :, hl] = alpha * acc_ref[g, :, hl] + pv[:HEAD_DIM]
            m_ref[g:g + 1, hl] = m_new

    masked_logits(0, 0)

    def kv_pair(jj):
        a = 2 * jj
        masked_logits(a + 1, 1)
        softmax_pv(a, 0)
        masked_logits(a + 2, 0)
        softmax_pv(a + 1, 1)

    def kv_two_pairs(jq, c):
        kv_pair(2 * jq)
        kv_pair(2 * jq + 1)
        return c

    def kv_one_pair(jj, c):
        kv_pair(jj)
        return c

    npairs = (i + 2) // 2
    lax.fori_loop(0, npairs // 2, kv_two_pairs, 0)
    lax.fori_loop(2 * (npairs // 2), npairs, kv_one_pair, 0)
    for h in range(Q_HEADS):
        g, r = divmod(h, rep)
        inv_l = 1.0 / l_ref[g:g + 1, r * QB:(r + 1) * QB]
        oh = (acc_ref[g, :, r * QB:(r + 1) * QB] * inv_l).T
        zh = z_ref[:, h * HEAD_DIM:(h + 1) * HEAD_DIM]
        o_ref[:, h * HEAD_DIM:(h + 1) * HEAD_DIM] = (oh * _silu(zh)).astype(o_ref.dtype)


def _attention(qT, qiT, wT, znat, zcol, k, vT, kidx, bsz, seq):
    nq = seq // QB
    topk = min(TOPK_MAX, seq // 4)
    m = bsz * seq
    step = lambda r: pl.BlockSpec((r, QB), lambda b, i: (0, b * nq + i))
    return pl.pallas_call(
        functools.partial(_attn_kernel, topk=topk),
        grid=(bsz, nq),
        in_specs=[step(1024), step(1024), step(IDX_HEADS),
                  pl.BlockSpec((QB, ATTN_WIDTH), lambda b, i: (b * nq + i, zcol)),
                  pl.BlockSpec((seq, 256), lambda b, i: (b, 0)),
                  pl.BlockSpec((seq // KTA, KV_HEADS * VROWS, KTA), lambda b, i: (b, 0, 0)),
                  pl.BlockSpec((seq, IDX_DIM), lambda b, i: (b, 0))],
        out_specs=pl.BlockSpec((QB, ATTN_WIDTH), lambda b, i: (b * nq + i, 0)),
        out_shape=jax.ShapeDtypeStruct((m, ATTN_WIDTH), BF16),
        scratch_shapes=[pltpu.VMEM((seq, QB), I32), pltpu.VMEM((32, seq // 32, QB), I32),
                        pltpu.VMEM((seq, QB), F32),
                        pltpu.VMEM((KV_HEADS, Q_HEADS // KV_HEADS * QB), F32),
                        pltpu.VMEM((KV_HEADS, Q_HEADS // KV_HEADS * QB), F32),
                        pltpu.VMEM((KV_HEADS, HEAD_DIM, Q_HEADS // KV_HEADS * QB), F32),
                        pltpu.VMEM((2, KTA, Q_HEADS * QB), F32), pltpu.VMEM((2, Q_HEADS, QB), F32)],
        compiler_params=pltpu.CompilerParams(dimension_semantics=("arbitrary", "arbitrary"),
                                             vmem_limit_bytes=VMEM_LIMIT),
        name="dsa_attention",
    )(qT, qiT, wT, znat, k, vT, kidx)


def _ssd_kernel(xbc_ref, z_ref, dt_ref, cw_ref, cb_ref, dtb_ref, a_ref, dsk_ref, nw_ref, e2_ref, tril_ref, o_ref,
                buf_ref, tail_ref, state_ref, xc_ref, xdt_ref, acum_ref, ahi_ref, alo_ref, *, tt):
    t = pl.program_id(1)

    @pl.when(t == 0)
    def _():
        tail_ref[...] = jnp.zeros_like(tail_ref)
        state_ref[...] = jnp.zeros_like(state_ref)

    buf_ref[0:8, :] = tail_ref[...]
    buf_ref[8:8 + tt, :] = xbc_ref[...]
    tail_ref[...] = xbc_ref[tt - 8:tt, :]

    dtv = dt_ref[...] + dtb_ref[...]
    dtv = jnp.maximum(dtv, 0.0) + jnp.log(1.0 + jnp.exp(-jnp.abs(dtv)))
    dhi = dtv.astype(BF16).astype(F32)
    lane = lax.broadcasted_iota(I32, (1, 128), 1)
    dsplit = jnp.where(lane < SSD_HEADS, dhi, dtv - dhi).astype(BF16)
    acum_ref[...] = _nn(dsplit, e2_ref[...])
    neg_a = -jnp.exp(a_ref[...])

    for c in range(tt // CHUNK):
        r0 = c * CHUNK
        conv = cb_ref[...] + cw_ref[3:4, :] * buf_ref[8 + r0:8 + r0 + CHUNK, :]
        for jtap in range(SSD_CONV - 1):
            conv += cw_ref[jtap:jtap + 1, :] * buf_ref[5 + jtap + r0:5 + jtap + r0 + CHUNK, :]
        xc = _silu(conv)
        xc_ref[r0:r0 + CHUNK, :] = xc
        dt_e = acum_ref[r0:r0 + CHUNK, :]
        xdt_ref[r0:r0 + CHUNK, :] = xc[:, :SSD_WIDTH] * dt_e
        dta_e = dt_e * neg_a
        ahi = dta_e.astype(BF16)
        ahi_ref[r0:r0 + CHUNK, :] = ahi
        alo_ref[r0:r0 + CHUNK, :] = (dta_e - ahi.astype(F32)).astype(BF16)
    acum_ref[...] = _nn(tril_ref[...], ahi_ref[...]) + _nn(tril_ref[...], alo_ref[...])

    gw = SSD_WIDTH // SSD_GROUPS
    s_row = lax.broadcasted_iota(I32, (CHUNK, gw), 0)
    s_lane = lax.broadcasted_iota(I32, (CHUNK, gw), 1) & (CHUNK - 1)
    diag = s_row == s_lane
    tril = s_row >= s_lane
    bd_r = lax.broadcasted_iota(I32, (2 * SSD_P, 2 * SSD_P), 0) >> 6
    bd_c = lax.broadcasted_iota(I32, (2 * SSD_P, 2 * SSD_P), 1) >> 6
    blockdiag = bd_r == bd_c

    for c in range(tt // CHUNK):
        rows = slice(c * CHUNK, (c + 1) * CHUNK)
        for g in range(SSD_GROUPS):
            gl = slice(g * gw, (g + 1) * gw)
            bg = xc_ref[rows, SSD_WIDTH + g * SSD_N:SSD_WIDTH + (g + 1) * SSD_N]
            cg = xc_ref[rows, SSD_WIDTH + (SSD_GROUPS + g) * SSD_N:SSD_WIDTH + (SSD_GROUPS + g + 1) * SSD_N]
            bg16, cg16 = bg.astype(BF16), cg.astype(BF16)
            a_g = acum_ref[rows, gl]
            a_last = a_g[CHUNK - 1:CHUNK, :]
            xdt_g = xdt_ref[rows, gl]
            cb = _nt(cg16, jnp.concatenate([bg16] * (gw // CHUNK), axis=0))
            a_row = jnp.sum(jnp.where(diag, a_g, 0.0), axis=0, keepdims=True)
            lmat = jnp.exp(jnp.where(tril, a_g - a_row, -jnp.inf))
            mmat = (cb * lmat).astype(BF16)
            ys = []
            for q in range(gw // (2 * SSD_P)):
                ql = slice(q * 2 * SSD_P, (q + 1) * 2 * SSD_P)
                x2 = xdt_g[:, ql]
                wq = jnp.where(blockdiag, jnp.concatenate([x2, x2], axis=0), 0.0).astype(BF16)
                ys.append(_nn(mmat[:, ql], wq))
            y = jnp.concatenate(ys, axis=1)
            prev = state_ref[g]
            y += _nn(cg16, prev.astype(BF16)) * jnp.exp(a_g)
            y += dsk_ref[:, gl] * xc_ref[rows, gl]
            xdd = (xdt_g * jnp.exp(a_last - a_g)).astype(BF16)
            state_ref[g] = prev * jnp.exp(a_last) + _nn(bg.T.astype(BF16), xdd)
            gz = y * _silu(z_ref[rows, gl])
            gz = gz * lax.rsqrt(jnp.mean(gz * gz, axis=1, keepdims=True) + RMS_EPS)
            o_ref[rows, gl] = (gz * nw_ref[:, gl]).astype(o_ref.dtype)


def _ssd(nat, conv_w, conv_b, dt_bias, a_log, d_skip, norm_w, bsz, seq, tt):
    m = bsz * seq
    nt = seq // tt
    row = lambda b, t: b * nt + t
    full = lambda a: pl.BlockSpec(a.shape, lambda b, t: (0, 0))
    expand = lambda v: jnp.repeat(v.astype(F32), SSD_P).reshape(1, SSD_WIDTH)
    dtb2 = _pad_lanes(jnp.concatenate([dt_bias, dt_bias]))
    j_i = lax.broadcasted_iota(I32, (128, SSD_WIDTH), 0)
    h_i = lax.broadcasted_iota(I32, (128, SSD_WIDTH), 1) // SSD_P
    e2 = ((j_i == h_i) | (j_i == h_i + SSD_HEADS)).astype(BF16)
    r_i = lax.broadcasted_iota(I32, (tt, tt), 0)
    c_i = lax.broadcasted_iota(I32, (tt, tt), 1)
    tril_bd = ((r_i >= c_i) & (r_i // CHUNK == c_i // CHUNK)).astype(BF16)
    return pl.pallas_call(
        functools.partial(_ssd_kernel, tt=tt),
        grid=(bsz, nt),
        in_specs=[pl.BlockSpec((tt, XBC_WIDTH), lambda b, t: (row(b, t), 0)),
                  pl.BlockSpec((tt, SSD_WIDTH), lambda b, t: (row(b, t), XBC_WIDTH // SSD_WIDTH)),
                  pl.BlockSpec((tt, 128), lambda b, t: (row(b, t), (XBC_WIDTH + SSD_WIDTH + ATTN_WIDTH) // 128)),
                  full(conv_w), full(conv_b), full(dtb2), pl.BlockSpec((1, SSD_WIDTH), lambda b, t: (0, 0)),
                  pl.BlockSpec((1, SSD_WIDTH), lambda b, t: (0, 0)), full(norm_w), full(e2), full(tril_bd)],
        out_specs=pl.BlockSpec((tt, SSD_WIDTH), lambda b, t: (row(b, t), 0)),
        out_shape=jax.ShapeDtypeStruct((m, SSD_WIDTH), BF16),
        scratch_shapes=[pltpu.VMEM((tt + 8, XBC_WIDTH), F32), pltpu.VMEM((8, XBC_WIDTH), F32),
                        pltpu.VMEM((SSD_GROUPS, SSD_N, SSD_WIDTH // SSD_GROUPS), F32),
                        pltpu.VMEM((tt, XBC_WIDTH), F32), pltpu.VMEM((tt, SSD_WIDTH), F32),
                        pltpu.VMEM((tt, SSD_WIDTH), F32),
                        pltpu.VMEM((tt, SSD_WIDTH), BF16), pltpu.VMEM((tt, SSD_WIDTH), BF16)],
        compiler_params=pltpu.CompilerParams(dimension_semantics=("arbitrary", "arbitrary"),
                                             vmem_limit_bytes=VMEM_LIMIT),
        name="ssd_mixer",
    )(nat, nat, nat, conv_w, conv_b, dtb2, expand(a_log), expand(d_skip), norm_w, e2, tril_bd)


def _out_kernel(oa_ref, os_ref, x_ref, wa_ref, ws_ref, g_ref, b_ref, o_ref, *, alpha):
    nsplit = 4
    rb = o_ref.shape[0] // nsplit
    for r in range(nsplit):
        rows = slice(r * rb, (r + 1) * rb)
        sub = _nn(oa_ref[rows, :], wa_ref[...]) + _nn(os_ref[rows, :], ws_ref[...])
        y = alpha * x_ref[rows, :] + sub
        mu = jnp.mean(y, axis=1, keepdims=True)
        yc = y - mu
        var = jnp.mean(yc * yc, axis=1, keepdims=True)
        o_ref[rows, :] = yc * lax.rsqrt(var + LN_EPS) * g_ref[...] + b_ref[...]


def _out_proj(oa, os_, x2, wa, ws, g, b, alpha, tm):
    m = x2.shape[0]
    full = lambda a: pl.BlockSpec(a.shape, lambda i: (0, 0))
    rowblk = lambda w: pl.BlockSpec((tm, w), lambda i: (i, 0))
    return pl.pallas_call(
        functools.partial(_out_kernel, alpha=alpha),
        grid=(m // tm,),
        in_specs=[rowblk(ATTN_WIDTH), rowblk(SSD_WIDTH), rowblk(D_MODEL), full(wa), full(ws), full(g), full(b)],
        out_specs=rowblk(D_MODEL),
        out_shape=jax.ShapeDtypeStruct((m, D_MODEL), F32),
        compiler_params=pltpu.CompilerParams(dimension_semantics=("arbitrary",), vmem_limit_bytes=VMEM_LIMIT),
        name="out_proj_ln",
    )(oa, os_, x2, wa, ws, g, b)


def _rope_tables(seq, rot):
    half = rot // 2
    inv = ROPE_THETA ** (-jnp.arange(half, dtype=F32) * 2.0 / rot)
    ang = inv[:, None] * jnp.arange(seq, dtype=F32)[None, :]
    return jnp.cos(ang), jnp.sin(ang)


def _pad_lanes(v, width=128):
    v = v.reshape(1, -1).astype(F32)
    return jnp.pad(v, ((0, 0), (0, width - v.shape[1])))


def _layer(h, w_in, w_out, conv_w, conv_b, dt_bias, a_log, d_skip, norm_w, ln_g, ln_b, alpha):
    bsz, seq, _ = h.shape
    m = bsz * seq
    assert seq % QB == 0 and seq % 512 == 0
    x2 = h.reshape(m, D_MODEL)

    o_q, o_k, o_v, o_za, o_qi, o_ki, o_wi, o_zs, o_xbc, o_dt = (
        0, 1024, 1280, 1536, 2560, 3584, 3648, 3664, 4688, 6736)
    w_all, layer = w_in
    wcol = lambda a, b: w_all[layer, :, a:b]
    wqkv_t = wcol(o_q, o_za).T.astype(BF16)
    widx_t = wcol(o_qi, o_zs).T.astype(BF16)
    wq = wqkv_t[:o_k]
    wkv = wqkv_t[o_k:]
    wqi = widx_t[:o_ki - o_qi]
    wkw = jnp.pad(widx_t[o_ki - o_qi:], ((0, 128 - (o_zs - o_ki)), (0, 0)))
    w_dt = wcol(o_dt, o_dt + SSD_HEADS)
    wn = jnp.concatenate([wcol(o_xbc, o_dt), wcol(o_zs, o_xbc), wcol(o_za, o_qi),
                          jnp.pad(jnp.concatenate([w_dt, w_dt], axis=1), ((0, 0), (0, 128 - 2 * SSD_HEADS)))],
                         axis=1).astype(BF16)

    ca, sa = _rope_tables(seq, HEAD_DIM // ROPE_DIV)
    ci, si = _rope_tables(seq, IDX_DIM // ROPE_DIV)

    qT, k, vT, qiT, kidx, wT, xb = _proj_t(x2, wq, wkv, wqi, wkw, ca, sa, ci, si, seq, tm=512)
    nat = _proj_n(xb, wn, tm=min(1024, m), tn=1408)

    o_attn = _attention(qT, qiT, wT, nat, (XBC_WIDTH + SSD_WIDTH) // ATTN_WIDTH, k, vT, kidx, bsz, seq)
    o_ssd = _ssd(nat, conv_w, conv_b.reshape(1, -1), dt_bias, a_log, d_skip, norm_w.reshape(1, -1),
                 bsz, seq, tt=256)

    wo_all, _ = w_out
    out = _out_proj(o_attn, o_ssd, x2, wo_all[layer, :ATTN_WIDTH].astype(BF16), wo_all[layer, ATTN_WIDTH:].astype(BF16),
                    ln_g.reshape(1, -1), ln_b.reshape(1, -1), alpha, tm=512)
    return out.reshape(bsz, seq, D_MODEL)


def kernel(x, w_in, w_out, conv_w, conv_b, dt_bias, a_log, d_skip, ssd_norm_w, ln_g, ln_b):
    depth = w_in.shape[0]
    alpha = (2.0 * depth) ** 0.25
    h = x
    for layer in range(depth):
        h = _layer(h, (w_in, layer), (w_out, layer), conv_w[layer], conv_b[layer], dt_bias[layer],
                   a_log[layer], d_skip[layer], ssd_norm_w[layer], ln_g[layer], ln_b[layer], alpha)
    return h
```

```python
import functools
import math

import jax
import jax.numpy as jnp
from jax import lax
from jax.experimental import pallas as pl
from jax.experimental.pallas import tpu as pltpu

F32 = jnp.float32
BF16 = jnp.bfloat16
I32 = jnp.int32

D_MODEL = 2048
CHUNK = 64
ATTN_WIDTH = 1024
SSD_WIDTH = 1024
HEAD_DIM = 128
Q_HEADS = 8
KV_HEADS = 2
ROPE_THETA = 500000.0
ROPE_DIV = 4
IDX_HEADS = 16
IDX_DIM = 64
TOPK_MAX = 256
SSD_P = 64
SSD_HEADS = 16
SSD_GROUPS = 4
SSD_N = 128
SSD_CONV = 4
XBC_WIDTH = SSD_WIDTH + 2 * SSD_GROUPS * SSD_N
LN_EPS = 1e-5
RMS_EPS = 1e-5

VMEM_LIMIT = 56 * 1024 * 1024
INT_MIN = -(2 ** 31)

QB = 256
KT = 128
KTA = 256
VPAD = 16
VROWS = HEAD_DIM + VPAD


def _nt(a, b):
    return lax.dot_general(a, b, (((1,), (1,)), ((), ())), preferred_element_type=F32)


def _nn(a, b):
    return jnp.dot(a, b, preferred_element_type=F32)


def _silu(v):
    return v * (1.0 / (1.0 + jnp.exp(-v)))


def _rope_rows(t, cos, sin, nheads, hd, half):
    pieces = []
    for h in range(nheads):
        b = h * hd
        x1 = t[b:b + half]
        x2 = t[b + half:b + 2 * half]
        pieces += [x1 * cos - x2 * sin, x2 * cos + x1 * sin, t[b + 2 * half:b + hd]]
    return jnp.concatenate(pieces, axis=0)


def _proj_t_kernel(x_ref, wq_ref, wkv_ref, wqi_ref, wkw_ref, ca_ref, sa_ref, ci_ref, si_ref,
                   q_ref, k_ref, v_ref, qi_ref, kidx_ref, w_ref, xb_ref):
    x = x_ref[...].astype(BF16)
    xb_ref[...] = x
    ca, sa, ci, si = ca_ref[...], sa_ref[...], ci_ref[...], si_ref[...]
    qscale = HEAD_DIM ** -0.5 * math.log2(math.e)
    tq = _nt(wq_ref[...], x)
    q_ref[...] = (_rope_rows(tq, ca, sa, Q_HEADS, HEAD_DIM, 16) * qscale).astype(BF16)
    tkv = _nt(wkv_ref[...], x)
    k_ref[...] = _rope_rows(tkv[:KV_HEADS * HEAD_DIM], ca, sa, KV_HEADS, HEAD_DIM, 16).T.astype(BF16)
    ones = jnp.ones((VPAD, KTA), BF16)
    for jj in range(v_ref.shape[0]):
        for g in range(KV_HEADS):
            vg = tkv[(KV_HEADS + g) * HEAD_DIM:(KV_HEADS + g + 1) * HEAD_DIM, jj * KTA:(jj + 1) * KTA]
            v_ref[jj, g * VROWS:(g + 1) * VROWS, :] = jnp.concatenate([vg.astype(BF16), ones], axis=0)
    tqi = _nt(wqi_ref[...], x)
    qi_ref[...] = _rope_rows(tqi, ci, si, IDX_HEADS, IDX_DIM, 8).astype(BF16)
    tkw = _nt(wkw_ref[...], x)
    kidx_ref[...] = _rope_rows(tkw[:IDX_DIM], ci, si, 1, IDX_DIM, 8).T.astype(BF16)
    w_ref[...] = tkw[IDX_DIM:IDX_DIM + IDX_HEADS] * (IDX_HEADS ** -0.5 * IDX_DIM ** -0.5)


def _proj_t(x2, wq, wkv, wqi, wkw, ca, sa, ci, si, seq, tm):
    m, kdim = x2.shape
    nper = seq // tm
    full = lambda a: pl.BlockSpec(a.shape, lambda i: (0, 0), pipeline_mode=pl.Buffered(1))
    tab = lambda a: pl.BlockSpec((a.shape[0], tm), lambda i: (0, i % nper))
    cols = lambda r: pl.BlockSpec((r, tm), lambda i: (0, i))
    rows = lambda c: pl.BlockSpec((tm, c), lambda i: (i, 0))
    vspec = pl.BlockSpec((tm // KTA, KV_HEADS * VROWS, KTA), lambda i: (i, 0, 0))
    sds = jax.ShapeDtypeStruct
    return pl.pallas_call(
        _proj_t_kernel,
        grid=(m // tm,),
        in_specs=[pl.BlockSpec((tm, kdim), lambda i: (i, 0)), full(wq), full(wkv), full(wqi), full(wkw),
                  tab(ca), tab(sa), tab(ci), tab(si)],
        out_specs=[cols(1024), rows(256), vspec, cols(1024), rows(IDX_DIM), cols(IDX_HEADS), rows(kdim)],
        out_shape=[sds((1024, m), BF16), sds((m, 256), BF16), sds((m // KTA, KV_HEADS * VROWS, KTA), BF16),
                   sds((1024, m), BF16), sds((m, IDX_DIM), BF16), sds((IDX_HEADS, m), F32), sds((m, kdim), BF16)],
        compiler_params=pltpu.CompilerParams(dimension_semantics=("arbitrary",), vmem_limit_bytes=VMEM_LIMIT),
        name="proj_t",
    )(x2, wq, wkv, wqi, wkw, ca, sa, ci, si)


def _proj_n_kernel(x_ref, w_ref, o_ref):
    o_ref[...] = _nn(x_ref[...], w_ref[...]).astype(o_ref.dtype)


def _proj_n(xb, w, tm, tn):
    m, kdim = xb.shape
    n = w.shape[1]
    return pl.pallas_call(
        _proj_n_kernel,
        grid=(n // tn, m // tm),
        in_specs=[pl.BlockSpec((tm, kdim), lambda j, i: (i, 0)), pl.BlockSpec((kdim, tn), lambda j, i: (0, j))],
        out_specs=pl.BlockSpec((tm, tn), lambda j, i: (i, j)),
        out_shape=jax.ShapeDtypeStruct((m, n), F32),
        compiler_params=pltpu.CompilerParams(dimension_semantics=("arbitrary", "arbitrary"),
                                             vmem_limit_bytes=VMEM_LIMIT),
        name="proj_n",
    )(xb, w)


def _fold64(x, op2):
    parts = [x[a * 64:(a + 1) * 64] for a in range(x.shape[0] // 64)]
    while len(parts) > 1:
        parts = [op2(parts[2 * a], parts[2 * a + 1]) for a in range(len(parts) // 2)]
    return parts[0]


def _fold8_tree(x):
    parts = [x[a * 8:(a + 1) * 8] for a in range(x.shape[0] // 8)]
    while len(parts) > 1:
        paired = [parts[2 * a] + parts[2 * a + 1] for a in range(len(parts) // 2)]
        parts = paired + parts[2 * (len(parts) // 2):]
    return parts[0]


def _bit_transpose32(words):
    a = list(words)
    mask, j = 0x0000FFFF, 16
    while j:
        k = 0
        while k < 32:
            t = (a[k] ^ (a[k + j] >> j)) & mask
            a[k] = a[k] ^ t
            a[k + j] = a[k + j] ^ (t << j)
            k = (k + j + 1) & ~j
        j >>= 1
        mask = (mask ^ (mask << j)) & 0xFFFFFFFF
    return a


def _attn_kernel(qT_ref, qiT_ref, wT_ref, z_ref, k_ref, vT_ref, kidx_ref, o_ref,
                 key_ref, plane_ref, bias_ref, m_ref, l_ref, acc_ref, s_ref, mt_ref, *, topk):
    i = pl.program_id(1)
    nfull = 2 * i
    q_chunk = (i * QB + lax.broadcasted_iota(I32, (1, QB), 1)) // CHUNK

    def score_tile(j):
        rows = pl.ds(pl.multiple_of(j * KT, KT), KT)
        kt = kidx_ref[rows, :]
        acc = jnp.zeros((KT, QB), F32)
        for h in range(IDX_HEADS):
            qh = qiT_ref[h * IDX_DIM:(h + 1) * IDX_DIM, :]
            acc += jnp.maximum(_nn(kt, qh), 0.0) * wT_ref[h:h + 1, :]
        bits = pltpu.bitcast(acc, I32)
        return rows, bits ^ ((bits >> 31) & 0x7FFFFFFF)

    def store_planes(t256, keys):
        words = []
        for kk in keys:
            u = kk ^ INT_MIN
            words += [u[8 * a:8 * a + 8] for a in range(kk.shape[0] // 8)]
        planes = _bit_transpose32(words[::-1])
        prow = pl.ds(pl.multiple_of(t256 * 8, 8), 8)
        for b in range(32):
            plane_ref[b, prow, :] = planes[b]

    def planes_from_keys(t256):
        rows = pl.ds(pl.multiple_of(t256 * 2 * KT, 2 * KT), 2 * KT)
        store_planes(t256, [key_ref[rows, :]])

    @pl.when((pl.program_id(0) == 0) & (i == 0))
    def _():
        plane_ref[...] = jnp.zeros(plane_ref.shape, I32)

    for jj in range(QB // KT):
        j = nfull + jj
        rows, key = score_tile(j)
        k_chunk = (j * KT + lax.broadcasted_iota(I32, (KT, 1), 0)) // CHUNK
        key_ref[rows, :] = jnp.where(k_chunk <= q_chunk, key, INT_MIN)

    def key_block(j2):
        planes_from_keys(jnp.where(j2 == 0, i, j2 - 1))
        for jj in range(2):
            rows, key = score_tile(2 * j2 + jj)
            key_ref[rows, :] = key

    def two_blocks(jq, c):
        key_block(2 * jq)
        key_block(2 * jq + 1)
        return c

    def one_block(j2, c):
        key_block(j2)
        return c

    lax.fori_loop(0, i // 2, two_blocks, 0)
    lax.fori_loop(2 * (i // 2), i, one_block, 0)
    planes_from_keys(jnp.where(i == 0, 0, i - 1))

    nprow = plane_ref.shape[1]
    t_row = lax.broadcasted_iota(I32, (nprow, QB), 0) >> 3
    qc_local = lax.broadcasted_iota(I32, (1, QB), 1) // CHUNK
    diag_bits = lax.shift_right_logical(jnp.full((1, QB), -1, I32), 32 - 8 * (qc_local + 1))
    act0 = jnp.where(t_row < i, -1, jnp.where(t_row == i, diag_bits, 0))

    def radix_select(nrow):
        act = act0[:nrow]
        above = jnp.zeros((1, QB), I32)
        thr_u = jnp.zeros((1, QB), I32)
        for b in range(32):
            ones = act & plane_ref[b, 0:nrow, :]
            c8 = _fold8_tree(lax.population_count(ones))
            c = above + jnp.sum(c8, axis=0, keepdims=True)
            take = c >= topk
            act = jnp.where(take, ones, act ^ ones)
            above = jnp.where(take, above, c)
            thr_u = thr_u | jnp.where(take, jnp.int32(-(2 ** 31) if b == 0 else 1 << (31 - b)), 0)
        if nrow < nprow:
            act = jnp.concatenate([act, jnp.zeros((nprow - nrow, QB), I32)], axis=0)
        return thr_u, above, act

    sizes = sorted({max(8, (nprow * (c + 1) // 4) // 8 * 8) for c in range(4)})
    klass = sum(((i + 1) * 8 > s).astype(I32) for s in sizes[:-1])
    thr_u, above, act = lax.switch(klass, [functools.partial(radix_select, s) for s in sizes])
    thr = jnp.maximum(thr_u ^ INT_MIN, INT_MIN + 1)
    need = topk - above
    n_tied = jnp.sum(_fold8_tree(lax.population_count(act)), axis=0, keepdims=True)
    any_tie = jnp.max(jnp.where(n_tied > need, 1, 0)) > 0

    @pl.when(jnp.logical_not(any_tie))
    def _():
        def to_bias(j, c):
            rows = pl.ds(pl.multiple_of(j * KTA, KTA), KTA)
            bias_ref[rows, :] = jnp.where(key_ref[rows, :] >= thr, 0.0, -jnp.inf)
            return c

        lax.fori_loop(0, i + 1, to_bias, 0)

    @pl.when(any_tie)
    def _():
        p_row = lax.broadcasted_iota(I32, (nprow, QB), 0)
        word_bit_clear = {7: 0x0000FFFF, 6: 0x00FF00FF, 5: 0x0F0F0F0F, 4: 0x33333333, 3: 0x55555555}
        cand, left, idx_thr = act, need, jnp.zeros((1, QB), I32)
        for b in range((bias_ref.shape[0] - 1).bit_length() - 1, -1, -1):
            if b >= 8:
                zmask = jnp.where(((p_row >> (3 + b - 8)) & 1) == 0, -1, 0)
            elif b >= 3:
                zmask = word_bit_clear[b]
            else:
                zmask = jnp.where(((p_row >> b) & 1) == 0, -1, 0)
            zeros = cand & zmask
            c0 = jnp.sum(_fold8_tree(lax.population_count(zeros)), axis=0, keepdims=True)
            low = c0 >= left
            cand = jnp.where(low, zeros, cand ^ zeros)
            left = jnp.where(low, left, left - c0)
            idx_thr = idx_thr | jnp.where(low, 0, 1 << b)

        def to_bias_tied(j, c):
            rows = pl.ds(pl.multiple_of(j * KTA, KTA), KTA)
            key = key_ref[rows, :]
            kidx_abs = j * KTA + lax.broadcasted_iota(I32, (KTA, 1), 0)
            keep = (key > thr) | ((key == thr) & (kidx_abs <= idx_thr))
            bias_ref[rows, :] = jnp.where(keep, 0.0, -jnp.inf)
            return c

        lax.fori_loop(0, i + 1, to_bias_tied, 0)

    m_ref[...] = jnp.full(m_ref.shape, -jnp.inf, F32)
    l_ref[...] = jnp.zeros(l_ref.shape, F32)
    acc_ref[...] = jnp.zeros(acc_ref.shape, F32)

    rep = Q_HEADS // KV_HEADS
    gq = rep * QB
    ntile = bias_ref.shape[0] // KTA
    jpad = jnp.minimum(i + 1, ntile - 1)

    @pl.when(i + 1 < ntile)
    def _():
        bias_ref[pl.ds(pl.multiple_of((i + 1) * KTA, KTA), KTA), :] = jnp.full((KTA, QB), -jnp.inf, F32)

    def masked_logits(j, slot):
        jc = jnp.minimum(j, jpad)
        rows = pl.ds(pl.multiple_of(jc * KTA, KTA), KTA)
        for g in range(KV_HEADS):
            qg = jnp.concatenate([qT_ref[(g * rep + r) * HEAD_DIM:(g * rep + r + 1) * HEAD_DIM, :]
                                  for r in range(rep)], axis=1)
            sg = _nn(k_ref[rows, g * HEAD_DIM:(g + 1) * HEAD_DIM], qg)
            for r in range(rep):
                h = g * rep + r
                sh = sg[:, r * QB:(r + 1) * QB] + bias_ref[rows, :]
                s_ref[slot, :, h * QB:(h + 1) * QB] = sh
                mt_ref[slot, h:h + 1, :] = jnp.max(_fold64(sh, jnp.maximum), axis=0, keepdims=True)

    def softmax_pv(j, slot):
        jc = jnp.minimum(j, jpad)
        for h in range(Q_HEADS):
            g, r = divmod(h, rep)
            hl = slice(r * QB, (r + 1) * QB)
            s = s_ref[slot, :, h * QB:(h + 1) * QB]
            m = m_ref[g:g + 1, hl]
            m_new = jnp.maximum(m, mt_ref[slot, h:h + 1, :])
            m_safe = jnp.where(m_new == -jnp.inf, 0.0, m_new)
            alpha = jnp.exp2(m - m_safe)
            p = jnp.exp2(s - m_safe)
            pv = _nn(vT_ref[jc, g * VROWS:(g + 1) * VROWS, :], p.astype(BF16))
            l_ref[g:g + 1, hl] = alpha * l_ref[g:g + 1, hl] + pv[HEAD_DIM:HEAD_DIM + 1, :]
            acc_ref[g, :, hl] = alpha * acc_ref[g, :, hl] + pv[:HEAD_DIM]
            m_ref[g:g + 1, hl] = m_new

    masked_logits(0, 0)

    def kv_pair(jj):
        a = 2 * jj
        masked_logits(a + 1, 1)
        softmax_pv(a, 0)
        masked_logits(a + 2, 0)
        softmax_pv(a + 1, 1)

    def kv_two_pairs(jq, c):
        kv_pair(2 * jq)
        kv_pair(2 * jq + 1)
        return c

    def kv_one_pair(jj, c):
        kv_pair(jj)
        return c

    npairs = (i + 2) // 2
    lax.fori_loop(0, npairs // 2, kv_two_pairs, 0)
    lax.fori_loop(2 * (npairs // 2), npairs, kv_one_pair, 0)
    for h in range(Q_HEADS):
        g, r = divmod(h, rep)
        inv_l = 1.0 / l_ref[g:g + 1, r * QB:(r + 1) * QB]
        oh = (acc_ref[g, :, r * QB:(r + 1) * QB] * inv_l).T
        zh = z_ref[:, h * HEAD_DIM:(h + 1) * HEAD_DIM]
        o_ref[:, h * HEAD_DIM:(h + 1) * HEAD_DIM] = (oh * _silu(zh)).astype(o_ref.dtype)


def _attention(qT, qiT, wT, znat, zcol, k, vT, kidx, bsz, seq):
    nq = seq // QB
    topk = min(TOPK_MAX, seq // 4)
    m = bsz * seq
    step = lambda r: pl.BlockSpec((r, QB), lambda b, i: (0, b * nq + i))
    return pl.pallas_call(
        functools.partial(_attn_kernel, topk=topk),
        grid=(bsz, nq),
        in_specs=[step(1024), step(1024), step(IDX_HEADS),
                  pl.BlockSpec((QB, ATTN_WIDTH), lambda b, i: (b * nq + i, zcol)),
                  pl.BlockSpec((seq, 256), lambda b, i: (b, 0)),
                  pl.BlockSpec((seq // KTA, KV_HEADS * VROWS, KTA), lambda b, i: (b, 0, 0)),
                  pl.BlockSpec((seq, IDX_DIM), lambda b, i: (b, 0))],
        out_specs=pl.BlockSpec((QB, ATTN_WIDTH), lambda b, i: (b * nq + i, 0)),
        out_shape=jax.ShapeDtypeStruct((m, ATTN_WIDTH), BF16),
        scratch_shapes=[pltpu.VMEM((seq, QB), I32), pltpu.VMEM((32, seq // 32, QB), I32),
                        pltpu.VMEM((seq, QB), F32),
                        pltpu.VMEM((KV_HEADS, Q_HEADS // KV_HEADS * QB), F32),
                        pltpu.VMEM((KV_HEADS, Q_HEADS // KV_HEADS * QB), F32),
                        pltpu.VMEM((KV_HEADS, HEAD_DIM, Q_HEADS // KV_HEADS * QB), F32),
                        pltpu.VMEM((2, KTA, Q_HEADS * QB), F32), pltpu.VMEM((2, Q_HEADS, QB), F32)],
        compiler_params=pltpu.CompilerParams(dimension_semantics=("arbitrary", "arbitrary"),
                                             vmem_limit_bytes=VMEM_LIMIT),
        name="dsa_attention",
    )(qT, qiT, wT, znat, k, vT, kidx)


def _ssd_kernel(xbc_ref, z_ref, dt_ref, cw_ref, cb_ref, dtb_ref, a_ref, dsk_ref, nw_ref, e2_ref, tril_ref, o_ref,
                buf_ref, tail_ref, state_ref, xc_ref, xdt_ref, acum_ref, ahi_ref, alo_ref, *, tt):
    t = pl.program_id(1)

    @pl.when(t == 0)
    def _():
        tail_ref[...] = jnp.zeros_like(tail_ref)
        state_ref[...] = jnp.zeros_like(state_ref)

    buf_ref[0:8, :] = tail_ref[...]
    buf_ref[8:8 + tt, :] = xbc_ref[...]
    tail_ref[...] = xbc_ref[tt - 8:tt, :]

    dtv = dt_ref[...] + dtb_ref[...]
    dtv = jnp.maximum(dtv, 0.0) + jnp.log(1.0 + jnp.exp(-jnp.abs(dtv)))
    dhi = dtv.astype(BF16).astype(F32)
    lane = lax.broadcasted_iota(I32, (1, 128), 1)
    dsplit = jnp.where(lane < SSD_HEADS, dhi, dtv - dhi).astype(BF16)
    acum_ref[...] = _nn(dsplit, e2_ref[...])
    neg_a = -jnp.exp(a_ref[...])

    for c in range(tt // CHUNK):
        r0 = c * CHUNK
        conv = cb_ref[...] + cw_ref[3:4, :] * buf_ref[8 + r0:8 + r0 + CHUNK, :]
        for jtap in range(SSD_CONV - 1):
            conv += cw_ref[jtap:jtap + 1, :] * buf_ref[5 + jtap + r0:5 + jtap + r0 + CHUNK, :]
        xc = _silu(conv)
        xc_ref[r0:r0 + CHUNK, :] = xc
        dt_e = acum_ref[r0:r0 + CHUNK, :]
        xdt_ref[r0:r0 + CHUNK, :] = xc[:, :SSD_WIDTH] * dt_e
        dta_e = dt_e * neg_a
        ahi = dta_e.astype(BF16)
        ahi_ref[r0:r0 + CHUNK, :] = ahi
        alo_ref[r0:r0 + CHUNK, :] = (dta_e - ahi.astype(F32)).astype(BF16)
    acum_ref[...] = _nn(tril_ref[...], ahi_ref[...]) + _nn(tril_ref[...], alo_ref[...])

    gw = SSD_WIDTH // SSD_GROUPS
    s_row = lax.broadcasted_iota(I32, (CHUNK, gw), 0)
    s_lane = lax.broadcasted_iota(I32, (CHUNK, gw), 1) & (CHUNK - 1)
    diag = s_row == s_lane
    tril = s_row >= s_lane
    bd_r = lax.broadcasted_iota(I32, (2 * SSD_P, 2 * SSD_P), 0) >> 6
    bd_c = lax.broadcasted_iota(I32, (2 * SSD_P, 2 * SSD_P), 1) >> 6
    blockdiag = bd_r == bd_c

    for c in range(tt // CHUNK):
        rows = slice(c * CHUNK, (c + 1) * CHUNK)
        for g in range(SSD_GROUPS):
            gl = slice(g * gw, (g + 1) * gw)
            bg = xc_ref[rows, SSD_WIDTH + g * SSD_N:SSD_WIDTH + (g + 1) * SSD_N]
            cg = xc_ref[rows, SSD_WIDTH + (SSD_GROUPS + g) * SSD_N:SSD_WIDTH + (SSD_GROUPS + g + 1) * SSD_N]
            bg16, cg16 = bg.astype(BF16), cg.astype(BF16)
            a_g = acum_ref[rows, gl]
            a_last = a_g[CHUNK - 1:CHUNK, :]
            xdt_g = xdt_ref[rows, gl]
            cb = _nt(cg16, jnp.concatenate([bg16] * (gw // CHUNK), axis=0))
            a_row = jnp.sum(jnp.where(diag, a_g, 0.0), axis=0, keepdims=True)
            lmat = jnp.exp(jnp.where(tril, a_g - a_row, -jnp.inf))
            mmat = (cb * lmat).astype(BF16)
            ys = []
            for q in range(gw // (2 * SSD_P)):
                ql = slice(q * 2 * SSD_P, (q + 1) * 2 * SSD_P)
                x2 = xdt_g[:, ql]
                wq = jnp.where(blockdiag, jnp.concatenate([x2, x2], axis=0), 0.0).astype(BF16)
                ys.append(_nn(mmat[:, ql], wq))
            y = jnp.concatenate(ys, axis=1)
            prev = state_ref[g]
            y += _nn(cg16, prev.astype(BF16)) * jnp.exp(a_g)
            y += dsk_ref[:, gl] * xc_ref[rows, gl]
            xdd = (xdt_g * jnp.exp(a_last - a_g)).astype(BF16)
            state_ref[g] = prev * jnp.exp(a_last) + _nn(bg.T.astype(BF16), xdd)
            gz = y * _silu(z_ref[rows, gl])
            gz = gz * lax.rsqrt(jnp.mean(gz * gz, axis=1, keepdims=True) + RMS_EPS)
            o_ref[rows, gl] = (gz * nw_ref[:, gl]).astype(o_ref.dtype)


def _ssd(nat, conv_w, conv_b, dt_bias, a_log, d_skip, norm_w, bsz, seq, tt):
    m = bsz * seq
    nt = seq // tt
    row = lambda b, t: b * nt + t
    full = lambda a: pl.BlockSpec(a.shape, lambda b, t: (0, 0))
    expand = lambda v: jnp.repeat(v.astype(F32), SSD_P).reshape(1, SSD_WIDTH)
    dtb2 = _pad_lanes(jnp.concatenate([dt_bias, dt_bias]))
    j_i = lax.broadcasted_iota(I32, (128, SSD_WIDTH), 0)
    h_i = lax.broadcasted_iota(I32, (128, SSD_WIDTH), 1) // SSD_P
    e2 = ((j_i == h_i) | (j_i == h_i + SSD_HEADS)).astype(BF16)
    r_i = lax.broadcasted_iota(I32, (tt, tt), 0)
    c_i = lax.broadcasted_iota(I32, (tt, tt), 1)
    tril_bd = ((r_i >= c_i) & (r_i // CHUNK == c_i // CHUNK)).astype(BF16)
    return pl.pallas_call(
        functools.partial(_ssd_kernel, tt=tt),
        grid=(bsz, nt),
        in_specs=[pl.BlockSpec((tt, XBC_WIDTH), lambda b, t: (row(b, t), 0)),
                  pl.BlockSpec((tt, SSD_WIDTH), lambda b, t: (row(b, t), XBC_WIDTH // SSD_WIDTH)),
                  pl.BlockSpec((tt, 128), lambda b, t: (row(b, t), (XBC_WIDTH + SSD_WIDTH + ATTN_WIDTH) // 128)),
                  full(conv_w), full(conv_b), full(dtb2), pl.BlockSpec((1, SSD_WIDTH), lambda b, t: (0, 0)),
                  pl.BlockSpec((1, SSD_WIDTH), lambda b, t: (0, 0)), full(norm_w), full(e2), full(tril_bd)],
        out_specs=pl.BlockSpec((tt, SSD_WIDTH), lambda b, t: (row(b, t), 0)),
        out_shape=jax.ShapeDtypeStruct((m, SSD_WIDTH), BF16),
        scratch_shapes=[pltpu.VMEM((tt + 8, XBC_WIDTH), F32), pltpu.VMEM((8, XBC_WIDTH), F32),
                        pltpu.VMEM((SSD_GROUPS, SSD_N, SSD_WIDTH // SSD_GROUPS), F32),
                        pltpu.VMEM((tt, XBC_WIDTH), F32), pltpu.VMEM((tt, SSD_WIDTH), F32),
                        pltpu.VMEM((tt, SSD_WIDTH), F32),
                        pltpu.VMEM((tt, SSD_WIDTH), BF16), pltpu.VMEM((tt, SSD_WIDTH), BF16)],
        compiler_params=pltpu.CompilerParams(dimension_semantics=("arbitrary", "arbitrary"),
                                             vmem_limit_bytes=VMEM_LIMIT),
        name="ssd_mixer",
    )(nat, nat, nat, conv_w, conv_b, dtb2, expand(a_log), expand(d_skip), norm_w, e2, tril_bd)


def _out_kernel(oa_ref, os_ref, x_ref, wa_ref, ws_ref, g_ref, b_ref, o_ref, *, alpha):
    nsplit = 4
    rb = o_ref.shape[0] // nsplit
    for r in range(nsplit):
        rows = slice(r * rb, (r + 1) * rb)
        sub = _nn(oa_ref[rows, :], wa_ref[...]) + _nn(os_ref[rows, :], ws_ref[...])
        y = alpha * x_ref[rows, :] + sub
        mu = jnp.mean(y, axis=1, keepdims=True)
        yc = y - mu
        var = jnp.mean(yc * yc, axis=1, keepdims=True)
        o_ref[rows, :] = yc * lax.rsqrt(var + LN_EPS) * g_ref[...] + b_ref[...]


def _out_proj(oa, os_, x2, wa, ws, g, b, alpha, tm):
    m = x2.shape[0]
    full = lambda a: pl.BlockSpec(a.shape, lambda i: (0, 0))
    rowblk = lambda w: pl.BlockSpec((tm, w), lambda i: (i, 0))
    return pl.pallas_call(
        functools.partial(_out_kernel, alpha=alpha),
        grid=(m // tm,),
        in_specs=[rowblk(ATTN_WIDTH), rowblk(SSD_WIDTH), rowblk(D_MODEL), full(wa), full(ws), full(g), full(b)],
        out_specs=rowblk(D_MODEL),
        out_shape=jax.ShapeDtypeStruct((m, D_MODEL), F32),
        compiler_params=pltpu.CompilerParams(dimension_semantics=("arbitrary",), vmem_limit_bytes=VMEM_LIMIT),
        name="out_proj_ln",
    )(oa, os_, x2, wa, ws, g, b)


def _rope_tables(seq, rot):
    half = rot // 2
    inv = ROPE_THETA ** (-jnp.arange(half, dtype=F32) * 2.0 / rot)
    ang = inv[:, None] * jnp.arange(seq, dtype=F32)[None, :]
    return jnp.cos(ang), jnp.sin(ang)


def _pad_lanes(v, width=128):
    v = v.reshape(1, -1).astype(F32)
    return jnp.pad(v, ((0, 0), (0, width - v.shape[1])))


def _layer(h, w_in, w_out, conv_w, conv_b, dt_bias, a_log, d_skip, norm_w, ln_g, ln_b, alpha):
    bsz, seq, _ = h.shape
    m = bsz * seq
    assert seq % QB == 0 and seq % 512 == 0
    x2 = h.reshape(m, D_MODEL)

    o_q, o_k, o_v, o_za, o_qi, o_ki, o_wi, o_zs, o_xbc, o_dt = (
        0, 1024, 1280, 1536, 2560, 3584, 3648, 3664, 4688, 6736)
    w_all, layer = w_in
    wcol = lambda a, b: w_all[layer, :, a:b]
    wqkv_t = wcol(o_q, o_za).T.astype(BF16)
    widx_t = wcol(o_qi, o_zs).T.astype(BF16)
    wq = wqkv_t[:o_k]
    wkv = wqkv_t[o_k:]
    wqi = widx_t[:o_ki - o_qi]
    wkw = jnp.pad(widx_t[o_ki - o_qi:], ((0, 128 - (o_zs - o_ki)), (0, 0)))
    w_dt = wcol(o_dt, o_dt + SSD_HEADS)
    wn = jnp.concatenate([wcol(o_xbc, o_dt), wcol(o_zs, o_xbc), wcol(o_za, o_qi),
                          jnp.pad(jnp.concatenate([w_dt, w_dt], axis=1), ((0, 0), (0, 128 - 2 * SSD_HEADS)))],
                         axis=1).astype(BF16)

    ca, sa = _rope_tables(seq, HEAD_DIM // ROPE_DIV)
    ci, si = _rope_tables(seq, IDX_DIM // ROPE_DIV)

    qT, k, vT, qiT, kidx, wT, xb = _proj_t(x2, wq, wkv, wqi, wkw, ca, sa, ci, si, seq, tm=min(1024, seq))
    nat = _proj_n(xb, wn, tm=min(1024, m), tn=1408)

    o_attn = _attention(qT, qiT, wT, nat, (XBC_WIDTH + SSD_WIDTH) // ATTN_WIDTH, k, vT, kidx, bsz, seq)
    o_ssd = _ssd(nat, conv_w, conv_b.reshape(1, -1), dt_bias, a_log, d_skip, norm_w.reshape(1, -1),
                 bsz, seq, tt=512)

    wo_all, _ = w_out
    out = _out_proj(o_attn, o_ssd, x2, wo_all[layer, :ATTN_WIDTH].astype(BF16), wo_all[layer, ATTN_WIDTH:].astype(BF16),
                    ln_g.reshape(1, -1), ln_b.reshape(1, -1), alpha, tm=512)
    return out.reshape(bsz, seq, D_MODEL)


def kernel(x, w_in, w_out, conv_w, conv_b, dt_bias, a_log, d_skip, ssd_norm_w, ln_g, ln_b):
    depth = w_in.shape[0]
    alpha = (2.0 * depth) ** 0.25
    h = x
    for layer in range(depth):
        h = _layer(h, (w_in, layer), (w_out, layer), conv_w[layer], conv_b[layer], dt_bias[layer],
                   a_log[layer], d_skip[layer], ssd_norm_w[layer], ln_g[layer], ln_b[layer], alpha)
    return h
```
